```python
import jax
import jax.numpy as jnp
from jax import lax
import numpy as np

D_MODEL = 1024
BATCH = 16
SEQ = 4096
DEPTH = 2
DEC_BATCH = 8
DEC_SEQ = 4096
PAST_LEN = 128

N_EVEN = (DEPTH + 1) // 2
N_ODD = DEPTH // 2

GLA_HEADS = 4
GLA_DK = 64
GLA_DV = 128
GLA_QK = GLA_HEADS * GLA_DK
GLA_VW = GLA_HEADS * GLA_DV
GLA_GATE_RANK = 16
GLA_TAU = 16.0
GLA_CHUNK = 64

DIL_HEADS = 8
DIL_DH = 64
DIL_W = DIL_HEADS * DIL_DH
DIL_PATTERNS = ((128, 1), (512, 4), (2048, 16))
MASK_VALUE = -1e30

EVEN_SIZES = (GLA_QK, GLA_QK, GLA_VW, GLA_VW, GLA_GATE_RANK, GLA_GATE_RANK, DIL_W, DIL_W, DIL_W)
EVEN_IN = sum(EVEN_SIZES)
EVEN_MIX = GLA_VW + DIL_W

MLA_HEADS = 16
MLA_NOPE = 64
MLA_ROPE = 32
MLA_V = 64
MLA_Q_LORA = 384
MLA_KV_LORA = 128
MLA_QBLOCK = 128
ROPE_BASE = 10000.0
ODD_IN = MLA_Q_LORA + MLA_KV_LORA + MLA_ROPE
ODD_MIX = MLA_HEADS * MLA_V

MOE_GROUPS = 4
MOE_EXP_PER_GROUP = 4
MOE_EXPERTS = MOE_GROUPS * MOE_EXP_PER_GROUP
MOE_TOP_K = 2
MOE_D_FF = 512

DEEPNORM_ALPHA = (2 * DEPTH) ** 0.25
DEEPNORM_BETA = (8 * DEPTH) ** -0.25
LN_EPS = 1e-5
RMS_EPS = 1e-6

kernel_name = 'hybrid_gla_dilated_mla_hmoe_encoder'


def layer_norm(x, g, b):
    xf = x.astype(jnp.float32)
    mu = jnp.mean(xf, axis=-1, keepdims=True)
    var = jnp.mean(jnp.square(xf - mu), axis=-1, keepdims=True)
    return ((xf - mu) * lax.rsqrt(var + LN_EPS) * g + b).astype(x.dtype)


def rms_norm(x, g):
    xf = x.astype(jnp.float32)
    return (xf * lax.rsqrt(jnp.mean(xf * xf, axis=-1, keepdims=True) + RMS_EPS) * g).astype(x.dtype)


def gla_scan_chunked(q, k, v, log_a):
    B, S, H, dk = q.shape
    dv = v.shape[-1]
    C = GLA_CHUNK
    n = S // C

    def chunks(t):
        return t.reshape(B, n, C, H, t.shape[-1]).transpose(1, 0, 3, 2, 4)

    qc, kc, vc, ac = chunks(q), chunks(k), chunks(v), chunks(log_a)
    b = jnp.cumsum(ac, axis=3)
    b_last = b[:, :, :, -1, :]
    q_dec = qc * jnp.exp(b)
    k_dec = kc * jnp.exp(-b)
    k_end = kc * jnp.exp(b_last[:, :, :, None, :] - b)
    causal = jnp.tril(jnp.ones((C, C), dtype=bool))
    scores = jnp.where(causal, jnp.einsum('nbhik,nbhjk->nbhij', q_dec, k_dec), 0.0)
    o_intra = jnp.einsum('nbhij,nbhjv->nbhiv', scores, vc)

    def step(state, inp):
        q_d, k_e, v_c, bl = inp
        o_inter = jnp.einsum('bhik,bhkv->bhiv', q_d, state)
        state = jnp.exp(bl)[..., None] * state + jnp.einsum('bhjk,bhjv->bhkv', k_e, v_c)
        return state, o_inter

    state0 = jnp.zeros((B, H, dk, dv), dtype=q.dtype)
    _, o_inter = lax.scan(step, state0, (q_dec, k_end, vc, b_last))
    o = o_intra + o_inter
    return o.transpose(1, 0, 3, 2, 4).reshape(B, S, H, dv)


def gla_mixer(q, k, v, r, a_f, a_b, wa_f, ba_f, wa_b, ba_b, norm_g):
    B, S, _ = q.shape

    def heads(t, d):
        return t.reshape(B, S, GLA_HEADS, d).astype(jnp.float32)

    def log_decay(a_low, w, bias):
        z = (a_low @ w + bias).astype(jnp.float32)
        return heads(jax.nn.log_sigmoid(z) / GLA_TAU, GLA_DK)

    def flip(t):
        return jnp.flip(t, axis=1)

    qh = heads(q, GLA_DK) * GLA_DK ** -0.5
    kh = heads(k, GLA_DK)
    vh = heads(v, GLA_DV)
    o_fwd = gla_scan_chunked(qh, kh, vh, log_decay(a_f, wa_f, ba_f))
    o_bwd = flip(gla_scan_chunked(flip(qh), flip(kh), flip(vh), flip(log_decay(a_b, wa_b, ba_b))))
    o = rms_norm(o_fwd + o_bwd, norm_g.reshape(GLA_HEADS, GLA_DV))
    return o.reshape(B, S, GLA_VW).astype(q.dtype) * jax.nn.silu(r)


def dilated_branch(q, k, v, dilation, radius, slopes):
    B, S, H, dh = q.shape
    L = S // dilation
    nb = -(-L // radius)
    Lp = nb * radius

    def strided(t):
        t = t.reshape(B, L, dilation, H, dh).transpose(0, 2, 1, 3, 4)
        return jnp.pad(t, ((0, 0), (0, 0), (0, Lp - L), (0, 0), (0, 0)))

    def key_blocks(t):
        tp = jnp.pad(strided(t), ((0, 0), (0, 0), (radius, radius), (0, 0), (0, 0)))
        tb = tp.reshape(B, dilation, nb + 2, radius, H, dh)
        return jnp.concatenate([tb[:, :, :-2], tb[:, :, 1:-1], tb[:, :, 2:]], axis=3)

    qb = strided(q).reshape(B, dilation, nb, radius, H, dh)
    kb = key_blocks(k)
    vb = key_blocks(v)
    rel = jnp.arange(3 * radius)[None, :] - radius - jnp.arange(radius)[:, None]
    key_pos = jnp.arange(nb)[:, None] * radius + jnp.arange(3 * radius)[None, :] - radius
    valid = (jnp.abs(rel) <= radius)[None] & ((key_pos >= 0) & (key_pos < L))[:, None, :]
    dist = (dilation * jnp.abs(rel)).astype(jnp.float32)
    alibi = -slopes[:, None, None] * dist
    s = jnp.einsum('bdnihe,bdnjhe->bdnhij', qb, kb) + alibi
    s = jnp.where(valid[:, None, :, :], s, MASK_VALUE)
    lse = jax.nn.logsumexp(s, axis=-1)
    p = jnp.exp(s - lse[..., None])
    o = jnp.einsum('bdnhij,bdnjhe->bdnihe', p, vb)
    o = o.reshape(B, dilation, Lp, H, dh)[:, :, :L].transpose(0, 2, 1, 3, 4).reshape(B, S, H, dh)
    lse = lse.transpose(0, 1, 2, 4, 3).reshape(B, dilation, Lp, H)[:, :, :L]
    lse = lse.transpose(0, 2, 1, 3).reshape(B, S, H)
    return o, lse


def dilated_mixer(q, k, v):
    B, S, _ = q.shape
    qh = q.reshape(B, S, DIL_HEADS, DIL_DH).astype(jnp.float32) * DIL_DH ** -0.5
    kh = k.reshape(B, S, DIL_HEADS, DIL_DH).astype(jnp.float32)
    vh = v.reshape(B, S, DIL_HEADS, DIL_DH).astype(jnp.float32)
    slopes = 2.0 ** (-8.0 * jnp.arange(1, DIL_HEADS + 1, dtype=jnp.float32) / DIL_HEADS)
    outs, lses = [], []
    for window, dilation in DIL_PATTERNS:
        o, lse = dilated_branch(qh, kh, vh, dilation, window // (2 * dilation), slopes)
        outs.append(o)
        lses.append(lse)
    w = jax.nn.softmax(jnp.stack(lses, axis=0), axis=0)
    out = jnp.sum(w[..., None] * jnp.stack(outs, axis=0), axis=0)
    return out.reshape(B, S, DIL_W).astype(q.dtype)


def even_mixer(x, w_in, wa_f, ba_f, wa_b, ba_b, gla_norm_g, w_out):
    h = x @ w_in
    idx = [int(i) for i in np.cumsum(EVEN_SIZES)[:-1]]
    q_a, k_a, v_a, r_a, a_f, a_b, q_b, k_b, v_b = jnp.split(h, idx, axis=-1)
    o_a = gla_mixer(q_a, k_a, v_a, r_a, a_f, a_b, wa_f, ba_f, wa_b, ba_b, gla_norm_g)
    o_b = dilated_mixer(q_b, k_b, v_b)
    return jnp.concatenate([o_a, o_b], axis=-1) @ w_out


def rope_tables(S):
    inv = 1.0 / (ROPE_BASE ** (jnp.arange(0, MLA_ROPE, 2, dtype=jnp.float32) / MLA_ROPE))
    ang = jnp.arange(S, dtype=jnp.float32)[:, None] * inv[None, :]
    return jnp.cos(ang), jnp.sin(ang)


def apply_rope(x, cos, sin):
    half = x.shape[-1] // 2
    x1 = x[..., :half].astype(jnp.float32)
    x2 = x[..., half:].astype(jnp.float32)
    c = cos[None, :, None, :]
    s = sin[None, :, None, :]
    return jnp.concatenate([x1 * c - x2 * s, x1 * s + x2 * c], axis=-1).astype(x.dtype)


def mla_mixer(x, w_in, q_norm_g, kv_norm_g, w_uq, w_ukv, w_out):
    B, S, _ = x.shape
    h = x @ w_in
    c_q, c_kv, k_rope = jnp.split(h, [MLA_Q_LORA, MLA_Q_LORA + MLA_KV_LORA], axis=-1)
    q = (rms_norm(c_q, q_norm_g) @ w_uq).reshape(B, S, MLA_HEADS, MLA_NOPE + MLA_ROPE)
    kv = (rms_norm(c_kv, kv_norm_g) @ w_ukv).reshape(B, S, MLA_HEADS, MLA_NOPE + MLA_V)
    cos, sin = rope_tables(S)
    q_nope = q[..., :MLA_NOPE]
    q_rope = apply_rope(q[..., MLA_NOPE:], cos, sin)
    k_rope = apply_rope(k_rope[:, :, None, :], cos, sin)[:, :, 0]
    k_nope = kv[..., :MLA_NOPE]
    v = kv[..., MLA_NOPE:]
    scale = (MLA_NOPE + MLA_ROPE) ** -0.5
    nq = S // MLA_QBLOCK

    def blocks(t):
        return t.reshape(B, nq, MLA_QBLOCK, MLA_HEADS, t.shape[-1]).transpose(1, 0, 2, 3, 4)

    def attend(qb):
        qn, qr = qb
        s = jnp.einsum('bqhe,bkhe->bhqk', qn, k_nope) + jnp.einsum('bqhr,bkr->bhqk', qr, k_rope)
        p = jax.nn.softmax(s.astype(jnp.float32) * scale, axis=-1)
        return jnp.einsum('bhqk,bkhe->bqhe', p.astype(v.dtype), v)

    o = lax.map(attend, (blocks(q_nope), blocks(q_rope)))
    o = o.transpose(1, 0, 2, 3, 4).reshape(B, S, ODD_MIX)
    return o @ w_out


def hier_moe(x, wg, bg, we, be, w1, w3, w2):
    B, S, D = x.shape
    t = x.reshape(B * S, D)
    g_prob = jax.nn.softmax((t @ wg + bg).astype(jnp.float32), axis=-1)
    g_val, g_idx = lax.top_k(g_prob, 1)
    e_logits = (t @ we + be).astype(jnp.float32).reshape(-1, MOE_GROUPS, MOE_EXP_PER_GROUP)
    e_logits = jnp.take_along_axis(e_logits, g_idx[:, :, None], axis=1)[:, 0]
    e_val, e_idx = lax.top_k(jax.nn.softmax(e_logits, axis=-1), MOE_TOP_K)
    e_val = e_val / jnp.sum(e_val, axis=-1, keepdims=True)
    expert_id = g_idx * MOE_EXP_PER_GROUP + e_idx
    combine = jnp.sum(jax.nn.one_hot(expert_id, MOE_EXPERTS, dtype=jnp.float32)
                      * (g_val * e_val)[..., None], axis=1).astype(t.dtype)
    out = jnp.zeros_like(t)
    for e in range(MOE_EXPERTS):
        hdn = jax.nn.silu(t @ w1[e]) * (t @ w3[e])
        out = out + combine[:, e:e + 1] * (hdn @ w2[e])
    return out.reshape(B, S, D)


def trunk(x, ev_w_in, ev_wa_f, ev_ba_f, ev_wa_b, ev_ba_b, ev_gla_norm, ev_w_out,
          od_w_in, od_q_norm, od_kv_norm, od_w_uq, od_w_ukv, od_w_out,
          ln1_g, ln1_b, ln2_g, ln2_b,
          moe_wg, moe_bg, moe_we, moe_be, moe_w1, moe_w3, moe_w2):
    for i in range(DEPTH):
        j = i // 2
        if i % 2 == 0:
            m = even_mixer(x, ev_w_in[j], ev_wa_f[j], ev_ba_f[j], ev_wa_b[j], ev_ba_b[j],
                           ev_gla_norm[j], ev_w_out[j])
        else:
            m = mla_mixer(x, od_w_in[j], od_q_norm[j], od_kv_norm[j], od_w_uq[j], od_w_ukv[j],
                          od_w_out[j])
        x = layer_norm(DEEPNORM_ALPHA * x + m, ln1_g[i], ln1_b[i])
        f = hier_moe(x, moe_wg[i], moe_bg[i], moe_we[i], moe_be[i], moe_w1[i], moe_w3[i], moe_w2[i])
        x = layer_norm(DEEPNORM_ALPHA * x + f, ln2_g[i], ln2_b[i])
    return x


def setup_inputs(seed: int = 0) -> dict:
    key = jax.random.key(seed)
    keys = list(jax.random.split(key, 32))

    def nrm(shape, scale):
        return jax.random.normal(keys.pop(), shape, jnp.float32) * scale

    D = D_MODEL
    return {
        'x_prompt': nrm((BATCH, SEQ, D), 1.0),
        'x_sample': nrm((DEC_BATCH, DEC_SEQ, D), 1.0),
        'ev_w_in': nrm((N_EVEN, D, EVEN_IN), D ** -0.5),
        'ev_wa_f': nrm((N_EVEN, GLA_GATE_RANK, GLA_QK), GLA_GATE_RANK ** -0.5),
        'ev_ba_f': nrm((N_EVEN, GLA_QK), 0.1),
        'ev_wa_b': nrm((N_EVEN, GLA_GATE_RANK, GLA_QK), GLA_GATE_RANK ** -0.5),
        'ev_ba_b': nrm((N_EVEN, GLA_QK), 0.1),
        'ev_gla_norm': 1.0 + nrm((N_EVEN, GLA_VW), 0.02),
        'ev_w_out': nrm((N_EVEN, EVEN_MIX, D), EVEN_MIX ** -0.5 * DEEPNORM_BETA),
        'od_w_in': nrm((N_ODD, D, ODD_IN), D ** -0.5),
        'od_q_norm': 1.0 + nrm((N_ODD, MLA_Q_LORA), 0.02),
        'od_kv_norm': 1.0 + nrm((N_ODD, MLA_KV_LORA), 0.02),
        'od_w_uq': nrm((N_ODD, MLA_Q_LORA, MLA_HEADS * (MLA_NOPE + MLA_ROPE)), MLA_Q_LORA ** -0.5),
        'od_w_ukv': nrm((N_ODD, MLA_KV_LORA, MLA_HEADS * (MLA_NOPE + MLA_V)), MLA_KV_LORA ** -0.5),
        'od_w_out': nrm((N_ODD, ODD_MIX, D), ODD_MIX ** -0.5 * DEEPNORM_BETA),
        'ln1_g': 1.0 + nrm((DEPTH, D), 0.02),
        'ln1_b': nrm((DEPTH, D), 0.02),
        'ln2_g': 1.0 + nrm((DEPTH, D), 0.02),
        'ln2_b': nrm((DEPTH, D), 0.02),
        'moe_wg': nrm((DEPTH, D, MOE_GROUPS), D ** -0.5),
        'moe_bg': nrm((DEPTH, MOE_GROUPS), 0.01),
        'moe_we': nrm((DEPTH, D, MOE_EXPERTS), D ** -0.5),
        'moe_be': nrm((DEPTH, MOE_EXPERTS), 0.01),
        'moe_w1': nrm((DEPTH, MOE_EXPERTS, D, MOE_D_FF), D ** -0.5),
        'moe_w3': nrm((DEPTH, MOE_EXPERTS, D, MOE_D_FF), D ** -0.5),
        'moe_w2': nrm((DEPTH, MOE_EXPERTS, MOE_D_FF, D), MOE_D_FF ** -0.5 * DEEPNORM_BETA),
    }


def reference(x_prompt, x_sample, ev_w_in, ev_wa_f, ev_ba_f, ev_wa_b, ev_ba_b, ev_gla_norm,
              ev_w_out, od_w_in, od_q_norm, od_kv_norm, od_w_uq, od_w_ukv, od_w_out,
              ln1_g, ln1_b, ln2_g, ln2_b, moe_wg, moe_bg, moe_we, moe_be, moe_w1, moe_w3, moe_w2):
    y_prompt = trunk(x_prompt, ev_w_in, ev_wa_f, ev_ba_f, ev_wa_b, ev_ba_b, ev_gla_norm, ev_w_out,
                     od_w_in, od_q_norm, od_kv_norm, od_w_uq, od_w_ukv, od_w_out,
                     ln1_g, ln1_b, ln2_g, ln2_b,
                     moe_wg, moe_bg, moe_we, moe_be, moe_w1, moe_w3, moe_w2)
    y_sample = trunk(x_sample, ev_w_in, ev_wa_f, ev_ba_f, ev_wa_b, ev_ba_b, ev_gla_norm, ev_w_out,
                     od_w_in, od_q_norm, od_kv_norm, od_w_uq, od_w_ukv, od_w_out,
                     ln1_g, ln1_b, ln2_g, ln2_b,
                     moe_wg, moe_bg, moe_we, moe_be, moe_w1, moe_w3, moe_w2)
    return (y_prompt, y_sample)
```

```python
import functools

import numpy as np
import jax
import jax.numpy as jnp
from jax import lax
from jax.experimental import pallas as pl
from jax.experimental.pallas import tpu as pltpu

F32 = jnp.float32
BF16 = jnp.bfloat16
HIGHEST = lax.Precision.HIGHEST

D_MODEL = 1024
DEPTH = 2
GLA_HEADS, GLA_DK, GLA_DV = 4, 64, 128
GLA_QK, GLA_VW = GLA_HEADS * GLA_DK, GLA_HEADS * GLA_DV
GLA_RANK, GLA_TAU, GLA_CHUNK = 16, 16.0, 64
DIL_HEADS, DIL_DH = 8, 64
DIL_W = DIL_HEADS * DIL_DH
DIL_PATTERNS = ((128, 1), (512, 4), (2048, 16))
DIL_RADIUS = 64
MASK_VALUE = -1e30
MLA_HEADS, MLA_NOPE, MLA_ROPE, MLA_V = 16, 64, 32, 64
MLA_Q_LORA, MLA_KV_LORA = 384, 128
ROPE_BASE = 10000.0
MOE_GROUPS, MOE_EPG, MOE_EXPERTS, MOE_FF = 4, 4, 16, 512
ALPHA = (2 * DEPTH) ** 0.25
LN_EPS = 1e-5
RMS_EPS = 1e-6

LANES = 128
VMEM_LIMIT = 56 * 1024 * 1024
ROW_TILE = 512
MOE_TILE = 256
ATT_TQ = 512


def _params(*sem):
    return pltpu.CompilerParams(dimension_semantics=sem, vmem_limit_bytes=VMEM_LIMIT)


def _full(shape):
    n = len(shape)
    return pl.BlockSpec(shape, lambda *_: (0,) * n)


def _mm(a, b):
    return jnp.dot(a, b, preferred_element_type=F32)


def _mm_nt(a, b):
    return lax.dot_general(a, b, (((1,), (1,)), ((), ())), preferred_element_type=F32)


def _mm_tn(a, b):
    return lax.dot_general(a, b, (((0,), (0,)), ((), ())), preferred_element_type=F32)


def _layer_norm(y, g, b):
    mu = jnp.mean(y, axis=-1, keepdims=True)
    yc = y - mu
    var = jnp.mean(yc * yc, axis=-1, keepdims=True)
    return yc * lax.rsqrt(var + LN_EPS) * g + b


def _log_sigmoid(z):
    return jnp.minimum(z, 0.0) - jnp.log(1.0 + jnp.exp(-jnp.abs(z)))


def _even_in_kernel(x_ref, wa_ref, wb_ref, wg_ref, oa_ref, ob_ref, og_ref):
    x = x_ref[...].astype(BF16)
    oa_ref[...] = _mm(x, wa_ref[...]).astype(oa_ref.dtype)
    ob_ref[...] = _mm(x, wb_ref[...])
    og_ref[...] = _mm(x, wg_ref[...])


def _even_in(x2d, wa, wb, wg):
    T = x2d.shape[0]
    tm = ROW_TILE
    return pl.pallas_call(
        _even_in_kernel,
        grid=(T // tm,),
        in_specs=[pl.BlockSpec((tm, D_MODEL), lambda i: (i, 0)),
                  _full(wa.shape), _full(wb.shape), _full(wg.shape)],
        out_specs=[pl.BlockSpec((tm, wa.shape[1]), lambda i: (i, 0)),
                   pl.BlockSpec((tm, wb.shape[1]), lambda i: (i, 0)),
                   pl.BlockSpec((tm, wg.shape[1]), lambda i: (i, 0))],
        out_shape=[jax.ShapeDtypeStruct((T, wa.shape[1]), BF16),
                   jax.ShapeDtypeStruct((T, wb.shape[1]), F32),
                   jax.ShapeDtypeStruct((T, wg.shape[1]), F32)],
        compiler_params=_params("parallel"),
        name="even_in_proj",
    )(x2d, wa, wb, wg)


def _gla_kernel(q_ref, k_ref, v_ref, r_ref, gate_ref, wf_ref, bf_ref, wb_ref, bb_ref, ng_ref,
                o_ref, laf_ref, lab_ref, of_ref, ob_ref, *, seq):
    C = GLA_CHUNK
    n_chunks = seq // C
    gate = gate_ref[...]
    zf = jnp.dot(gate, wf_ref[...], precision=HIGHEST, preferred_element_type=F32) + bf_ref[...]
    zb = jnp.dot(gate, wb_ref[...], precision=HIGHEST, preferred_element_type=F32) + bb_ref[...]
    laf_ref[...] = _log_sigmoid(zf) * (1.0 / GLA_TAU)
    lab_ref[...] = _log_sigmoid(zb) * (1.0 / GLA_TAU)

    row = lax.broadcasted_iota(jnp.int32, (C, C), 0)
    col = lax.broadcasted_iota(jnp.int32, (C, C), 1)
    lower = row >= col
    upper = col >= row
    cum_f = lower.astype(F32)
    cum_b = upper.astype(F32)
    lane = lax.broadcasted_iota(jnp.int32, (1, LANES), 1)
    head_lane = (lane < GLA_DK, lane >= GLA_DK)
    srow = lax.broadcasted_iota(jnp.int32, (2 * GLA_DV, LANES), 0)
    scol = lax.broadcasted_iota(jnp.int32, (2 * GLA_DV, LANES), 1)
    diag = (srow < GLA_DV) == (scol < GLA_DK)
    scale = GLA_DK ** -0.5

    def chunk(start, la_ref, cum, keep, last_row, state):
        rows = pl.ds(start, C)
        la = la_ref[rows, :]
        q = q_ref[rows, :].astype(F32) * scale
        k = k_ref[rows, :].astype(F32)
        v = v_ref[rows, :]
        b = jnp.dot(cum, la, precision=HIGHEST, preferred_element_type=F32)
        bl = b[last_row:last_row + 1, :]
        qd = (q * jnp.exp(b)).astype(BF16)
        kd = (k * jnp.exp(-b)).astype(BF16)
        ke = (k * jnp.exp(bl - b)).astype(BF16)
        o_inter = _mm_nt(qd, state.astype(BF16))
        parts = []
        for h in range(2):
            qh = jnp.where(head_lane[h], qd, jnp.zeros_like(qd))
            s = jnp.where(keep, _mm_nt(qh, kd), 0.0)
            parts.append(_mm(s.astype(BF16), v[:, h * GLA_DV:(h + 1) * GLA_DV]))
        o = o_inter + jnp.concatenate(parts, axis=1)
        upd = jnp.where(diag, _mm_tn(v, ke), 0.0)
        state = jnp.exp(bl) * state + upd
        return o, state

    def body(i, carry):
        sf, sb = carry
        start_f = pl.multiple_of(i * C, C)
        start_b = pl.multiple_of((n_chunks - 1 - i) * C, C)
        o_f, sf = chunk(start_f, laf_ref, cum_f, lower, C - 1, sf)
        o_b, sb = chunk(start_b, lab_ref, cum_b, upper, 0, sb)
        of_ref[pl.ds(start_f, C), :] = o_f
        ob_ref[pl.ds(start_b, C), :] = o_b
        return sf, sb

    zero = jnp.zeros((2 * GLA_DV, LANES), F32)
    lax.fori_loop(0, n_chunks, body, (zero, zero))

    blk = 256

    def finish(j, _):
        rows = pl.ds(pl.multiple_of(j * blk, blk), blk)
        o = of_ref[rows, :] + ob_ref[rows, :]
        g = ng_ref[...]
        outs = []
        for h in range(2):
            oh = o[:, h * GLA_DV:(h + 1) * GLA_DV]
            ms = jnp.mean(oh * oh, axis=-1, keepdims=True)
            outs.append(oh * lax.rsqrt(ms + RMS_EPS) * g[:, h * GLA_DV:(h + 1) * GLA_DV])
        r = r_ref[rows, :].astype(F32)
        o_ref[rows, :] = (jnp.concatenate(outs, axis=1) * (r * jax.nn.sigmoid(r))).astype(o_ref.dtype)
        return 0

    lax.fori_loop(0, seq // blk, finish, 0)


def _gla(h_a, gate, wf, bf, wb, bb, norm_g, B, S):
    pairs = GLA_HEADS // 2
    kq, kv = 2 * GLA_DK, 2 * GLA_DV
    sq = pl.BlockSpec((None, S, kq), lambda b, g: (b, 0, g))
    sk = pl.BlockSpec((None, S, kq), lambda b, g: (b, 0, pairs + g))
    sv = pl.BlockSpec((None, S, kv), lambda b, g: (b, 0, (2 * GLA_QK) // kv + g))
    sr = pl.BlockSpec((None, S, kv), lambda b, g: (b, 0, (2 * GLA_QK + GLA_VW) // kv + g))
    sg = pl.BlockSpec((None, S, LANES), lambda b, g: (b, 0, 0))
    sw = pl.BlockSpec((LANES, kq), lambda b, g: (0, g))
    sb = pl.BlockSpec((1, kq), lambda b, g: (0, g))
    sn = pl.BlockSpec((1, kv), lambda b, g: (0, g))
    return pl.pallas_call(
        functools.partial(_gla_kernel, seq=S),
        grid=(B, pairs),
        in_specs=[sq, sk, sv, sr, sg, sw, sb, sw, sb, sn],
        out_specs=pl.BlockSpec((None, S, kv), lambda b, g: (b, 0, g)),
        out_shape=jax.ShapeDtypeStruct((B, S, GLA_VW), BF16),
        scratch_shapes=[pltpu.VMEM((S, kq), F32), pltpu.VMEM((S, kq), F32),
                        pltpu.VMEM((S, kv), F32), pltpu.VMEM((S, kv), F32)],
        compiler_params=_params("parallel", "parallel"),
        name="gla_mixer",
    )(h_a, h_a, h_a, h_a, gate, wf, bf, wb, bb, norm_g)


def _dil_kernel(q_ref, k_ref, v_ref, o_ref, m_ref, l_ref, acc_ref, *, seq):
    R = DIL_RADIUS
    n_blocks = seq // R
    pair = pl.program_id(1)
    lane = lax.broadcasted_iota(jnp.int32, (1, LANES), 1)
    first_head = lane < DIL_DH
    qi = lax.broadcasted_iota(jnp.int32, (R, 3 * R), 0)
    kj = lax.broadcasted_iota(jnp.int32, (R, 3 * R), 1)
    rel = kj - R - qi
    band = jnp.abs(rel) <= R
    dist = jnp.abs(rel).astype(F32)
    slopes = [1.0 / jnp.left_shift(jnp.ones((R, 3 * R), jnp.int32), 2 * pair + h + 1).astype(F32)
              for h in range(2)]
    scale = DIL_DH ** -0.5

    for p_idx, (window, dil) in enumerate(DIL_PATTERNS):
        assert window // (2 * dil) == R
        nb = n_blocks // dil
        bias = [-(dil * dist) * slopes[h] for h in range(2)]

        def rows_at(block, dil=dil):
            return pl.ds(block, R, stride=dil) if dil > 1 else pl.ds(block, R)

        def body(idx, _, dil=dil, nb=nb, bias=bias, first=(p_idx == 0), rows_at=rows_at):
            res = idx // nb
            n = idx - res * nb
            start = res + dil * R * n
            prev = res + dil * R * jnp.maximum(n - 1, 0)
            nxt = res + dil * R * jnp.minimum(n + 1, nb - 1)
            q = (q_ref[rows_at(start), :] * scale).astype(BF16)
            kcat = jnp.concatenate([k_ref[rows_at(prev), :], k_ref[rows_at(start), :],
                                    k_ref[rows_at(nxt), :]], axis=0).astype(BF16)
            vcat = jnp.concatenate([v_ref[rows_at(prev), :], v_ref[rows_at(start), :],
                                    v_ref[rows_at(nxt), :]], axis=0).astype(BF16)
            valid = band & ((kj >= R) | (n > 0)) & ((kj < 2 * R) | (n < nb - 1))
            s = []
            for h in range(2):
                qh = jnp.where(first_head if h == 0 else ~first_head, q, jnp.zeros_like(q))
                s.append(jnp.where(valid, _mm_nt(qh, kcat) + bias[h], MASK_VALUE))
            m_cur = jnp.where(first_head, jnp.max(s[0], axis=-1, keepdims=True),
                              jnp.max(s[1], axis=-1, keepdims=True))
            if first:
                m_new = m_cur
            else:
                m_old = m_ref[rows_at(start), :]
                m_new = jnp.maximum(m_old, m_cur)
            p = [jnp.exp(s[0] - m_new[:, 0:1]), jnp.exp(s[1] - m_new[:, DIL_DH:DIL_DH + 1])]
            l_cur = jnp.where(first_head, jnp.sum(p[0], axis=-1, keepdims=True),
                              jnp.sum(p[1], axis=-1, keepdims=True))
            pv = jnp.where(first_head, _mm(p[0].astype(BF16), vcat), _mm(p[1].astype(BF16), vcat))
            if first:
                l_new, acc_new = l_cur, pv
            else:
                alpha = jnp.exp(m_old - m_new)
                l_new = alpha * l_ref[rows_at(start), :] + l_cur
                acc_new = alpha * acc_ref[rows_at(start), :] + pv
            m_ref[rows_at(start), :] = m_new
            l_ref[rows_at(start), :] = l_new
            acc_ref[rows_at(start), :] = acc_new
            return 0

        lax.fori_loop(0, n_blocks, body, 0)

    blk = 512

    def finish(j, _):
        rows = pl.ds(pl.multiple_of(j * blk, blk), blk)
        o_ref[rows, :] = (acc_ref[rows, :] / l_ref[rows, :]).astype(o_ref.dtype)
        return 0

    lax.fori_loop(0, seq // blk, finish, 0)


def _dilated(h_b, B, S):
    pairs = DIL_HEADS // 2
    sq = pl.BlockSpec((None, S, LANES), lambda b, g: (b, 0, g))
    sk = pl.BlockSpec((None, S, LANES), lambda b, g: (b, 0, pairs + g))
    sv = pl.BlockSpec((None, S, LANES), lambda b, g: (b, 0, 2 * pairs + g))
    return pl.pallas_call(
        functools.partial(_dil_kernel, seq=S),
        grid=(B, pairs),
        in_specs=[sq, sk, sv],
        out_specs=pl.BlockSpec((None, S, LANES), lambda b, g: (b, 0, g)),
        out_shape=jax.ShapeDtypeStruct((B, S, DIL_W), BF16),
        scratch_shapes=[pltpu.VMEM((S, LANES), F32)] * 3,
        compiler_params=_params("parallel", "parallel"),
        name="dilated_mixer",
    )(h_b, h_b, h_b)


def _route(lt):
    g = [lt[i:i + 1, :] for i in range(MOE_GROUPS)]
    gmax = functools.reduce(jnp.maximum, g)
    gexp = [jnp.exp(x - gmax) for x in g]
    gsum = functools.reduce(lambda a, b: a + b, gexp)
    gprob = [x / gsum for x in gexp]
    g_val, g_idx = gprob[0], jnp.zeros_like(gprob[0], dtype=jnp.int32)
    for i in range(1, MOE_GROUPS):
        better = gprob[i] > g_val
        g_idx = jnp.where(better, i, g_idx)
        g_val = jnp.where(better, gprob[i], g_val)
    e = []
    for j in range(MOE_EPG):
        x = lt[MOE_GROUPS + j:MOE_GROUPS + j + 1, :]
        for gi in range(1, MOE_GROUPS):
            r0 = MOE_GROUPS + gi * MOE_EPG + j
            x = jnp.where(g_idx == gi, lt[r0:r0 + 1, :], x)
        e.append(x)
    emax = functools.reduce(jnp.maximum, e)
    eexp = [jnp.exp(x - emax) for x in e]
    esum = functools.reduce(lambda a, b: a + b, eexp)
    eprob = [x / esum for x in eexp]
    v1, i1 = eprob[0], jnp.zeros_like(g_idx)
    for j in range(1, MOE_EPG):
        better = eprob[j] > v1
        i1 = jnp.where(better, j, i1)
        v1 = jnp.where(better, eprob[j], v1)
    v2, i2 = jnp.full_like(v1, -1.0), jnp.zeros_like(g_idx)
    for j in range(MOE_EPG):
        cand = jnp.where(i1 == j, -1.0, eprob[j])
        better = cand > v2
        i2 = jnp.where(better, j, i2)
        v2 = jnp.where(better, cand, v2)
    den = v1 + v2
    ids = jnp.concatenate([g_idx * MOE_EPG + i1, g_idx * MOE_EPG + i2], axis=0)
    wts = jnp.concatenate([g_val * (v1 / den), g_val * (v2 / den)], axis=0)
    return ids, wts


def _proj_ln_route_kernel(*refs, n_in):
    x_ref = refs[0]
    o_refs = refs[1:1 + n_in]
    w_refs = refs[1 + n_in:1 + 2 * n_in]
    g_ref, b_ref, wr_ref, br_ref, x1_ref, x1b_ref, ids_ref, wts_ref = refs[1 + 2 * n_in:]
    m = _mm(o_refs[0][...], w_refs[0][...])
    for o_ref, w_ref in zip(o_refs[1:], w_refs[1:]):
        m = m + _mm(o_ref[...], w_ref[...])
    x1 = _layer_norm(ALPHA * x_ref[...] + m, g_ref[...], b_ref[...])
    x1_ref[...] = x1
    x1b_ref[...] = x1.astype(BF16)
    lt = lax.dot_general(wr_ref[...], x1, (((1,), (1,)), ((), ())), precision=HIGHEST,
                         preferred_element_type=F32) + br_ref[...]
    ids, wts = _route(lt)
    ids_ref[...] = ids
    wts_ref[...] = wts


def _proj_ln_route(x2d, outs, ws, ln_g, ln_b, wr, br):
    T = x2d.shape[0]
    tm = ROW_TILE
    n_in = len(outs)
    row = lambda w: pl.BlockSpec((tm, w), lambda i: (i, 0))
    tok = pl.BlockSpec((2, tm), lambda i: (0, i))
    return pl.pallas_call(
        functools.partial(_proj_ln_route_kernel, n_in=n_in),
        grid=(T // tm,),
        in_specs=[row(D_MODEL)] + [row(o.shape[1]) for o in outs] + [_full(w.shape) for w in ws]
                 + [_full(ln_g.shape), _full(ln_b.shape), _full(wr.shape), _full(br.shape)],
        out_specs=[row(D_MODEL), row(D_MODEL), tok, tok],
        out_shape=[jax.ShapeDtypeStruct((T, D_MODEL), F32), jax.ShapeDtypeStruct((T, D_MODEL), BF16),
                   jax.ShapeDtypeStruct((2, T), jnp.int32), jax.ShapeDtypeStruct((2, T), F32)],
        compiler_params=_params("parallel"),
        name="proj_ln_route",
    )(x2d, *outs, *ws, ln_g, ln_b, wr, br)


def _moe_kernel(te_ref, nu_ref, xs_ref, w13_ref, w2_ref, rw_ref, ys_ref):
    i = pl.program_id(0)

    @pl.when(i < nu_ref[0])
    def _():
        h = _mm(xs_ref[...], w13_ref[...])
        h1, h3 = h[:, :MOE_FF], h[:, MOE_FF:]
        hidden = (h1 * jax.nn.sigmoid(h1)) * h3
        y = _mm(hidden.astype(BF16), w2_ref[...])
        ys_ref[...] = (y * rw_ref[...]).astype(ys_ref.dtype)

    @pl.when(i >= nu_ref[0])
    def _():
        ys_ref[...] = jnp.zeros_like(ys_ref)


def _moe_experts(xs, w13, w2, row_w, tile_expert, n_used):
    P = xs.shape[0]
    tm = MOE_TILE
    grid_spec = pltpu.PrefetchScalarGridSpec(
        num_scalar_prefetch=2,
        grid=(P // tm,),
        in_specs=[pl.BlockSpec((tm, D_MODEL), lambda i, te, nu: (i, 0)),
                  pl.BlockSpec((None, D_MODEL, 2 * MOE_FF), lambda i, te, nu: (te[i], 0, 0)),
                  pl.BlockSpec((None, MOE_FF, D_MODEL), lambda i, te, nu: (te[i], 0, 0)),
                  pl.BlockSpec((tm, 1), lambda i, te, nu: (i, 0))],
        out_specs=pl.BlockSpec((tm, D_MODEL), lambda i, te, nu: (i, 0)),
    )
    return pl.pallas_call(
        _moe_kernel,
        grid_spec=grid_spec,
        out_shape=jax.ShapeDtypeStruct((P, D_MODEL), BF16),
        compiler_params=_params("arbitrary"),
        name="moe_experts",
    )(tile_expert, n_used, xs, w13, w2, row_w)


def _moe_plan(ids, wts, T):
    tm = MOE_TILE
    n_assign = 2 * T
    e_flat = ids.reshape(n_assign)
    onehot = (e_flat[:, None] == jnp.arange(MOE_EXPERTS, dtype=jnp.int32)[None, :]).astype(jnp.int32)
    csum = jnp.cumsum(onehot, axis=0)
    rank = jnp.sum((csum - onehot) * onehot, axis=1)
    counts = csum[-1]
    padded = ((counts + tm - 1) // tm) * tm
    ends = jnp.cumsum(padded)
    offs = ends - padded
    pos = offs[e_flat] + rank
    P = n_assign + MOE_EXPERTS * tm
    n_tiles = P // tm
    tok = jnp.arange(n_assign, dtype=jnp.int32) % T
    src_tok = jnp.zeros((P,), jnp.int32).at[pos].set(tok)
    row_w = jnp.zeros((P,), F32).at[pos].set(wts.reshape(n_assign))
    tile_start = jnp.arange(n_tiles, dtype=jnp.int32) * tm
    tile_expert = jnp.minimum(jnp.sum((tile_start[:, None] >= ends[None, :]).astype(jnp.int32), axis=1),
                              MOE_EXPERTS - 1).astype(jnp.int32)
    n_used = (ends[-1:] // tm).astype(jnp.int32)
    return src_tok, row_w.reshape(P, 1), tile_expert, n_used, pos


def _combine_ln_kernel(x_ref, y0_ref, y1_ref, g_ref, b_ref, o_ref):
    y = ALPHA * x_ref[...] + y0_ref[...].astype(F32) + y1_ref[...].astype(F32)
    o_ref[...] = _layer_norm(y, g_ref[...], b_ref[...])


def _combine_ln(x2d, y0, y1, ln_g, ln_b):
    T = x2d.shape[0]
    tm = ROW_TILE
    row = pl.BlockSpec((tm, D_MODEL), lambda i: (i, 0))
    return pl.pallas_call(
        _combine_ln_kernel,
        grid=(T // tm,),
        in_specs=[row, row, row, _full(ln_g.shape), _full(ln_b.shape)],
        out_specs=row,
        out_shape=jax.ShapeDtypeStruct((T, D_MODEL), F32),
        compiler_params=_params("parallel"),
        name="combine_ln",
    )(x2d, y0, y1, ln_g, ln_b)


def _hier_moe_ln(x1, x1b, ids, wts, w13, w2, ln_g, ln_b):
    T = x1.shape[0]
    src_tok, row_w, tile_expert, n_used, pos = _moe_plan(ids, wts, T)
    xs = jnp.take(x1b, src_tok, axis=0)
    ys = _moe_experts(xs, w13, w2, row_w, tile_expert, n_used)
    y0 = jnp.take(ys, pos[:T], axis=0)
    y1 = jnp.take(ys, pos[T:], axis=0)
    return _combine_ln(x1, y0, y1, ln_g, ln_b)


def _rope_block(x, c, sa, sb):
    return x * c + pltpu.roll(x, LANES - 32, 1) * sa + pltpu.roll(x, 32, 1) * sb


def _mla_in_kernel(x_ref, win_ref, qg_ref, kvg_ref, wq_ref, wk_ref, wv_ref, c_ref, sa_ref, sb_ref,
                   q_ref, k_ref, v_ref):
    x = x_ref[...].astype(BF16)
    h = _mm(x, win_ref[...])
    c_q = h[:, :MLA_Q_LORA]
    c_kv = h[:, MLA_Q_LORA:MLA_Q_LORA + MLA_KV_LORA]
    k_rope = h[:, MLA_Q_LORA + MLA_KV_LORA:]
    cqn = c_q * lax.rsqrt(jnp.mean(c_q * c_q, axis=-1, keepdims=True) + RMS_EPS) * qg_ref[...]
    ckn = c_kv * lax.rsqrt(jnp.mean(c_kv * c_kv, axis=-1, keepdims=True) + RMS_EPS) * kvg_ref[...]
    cqn, ckn = cqn.astype(BF16), ckn.astype(BF16)
    scale = (MLA_NOPE + MLA_ROPE) ** -0.5
    q = _mm(cqn, wq_ref[...]) * scale
    kn = _mm(ckn, wk_ref[...])
    v_ref[...] = _mm(ckn, wv_ref[...]).astype(v_ref.dtype)
    c, sa, sb = c_ref[...], sa_ref[...], sb_ref[...]
    kr = _rope_block(k_rope, c, sa, sb).astype(k_ref.dtype)
    for g in range(MLA_HEADS // 2):
        lo = 2 * LANES * g
        q_ref[:, lo:lo + LANES] = q[:, lo:lo + LANES].astype(q_ref.dtype)
        q_ref[:, lo + LANES:lo + 2 * LANES] = _rope_block(q[:, lo + LANES:lo + 2 * LANES], c, sa, sb
                                                          ).astype(q_ref.dtype)
        k_ref[:, lo:lo + LANES] = kn[:, LANES * g:LANES * (g + 1)].astype(k_ref.dtype)
        k_ref[:, lo + LANES:lo + 2 * LANES] = kr


def _mla_in(x2d, win, qg, kvg, wq, wk, wv, c, sa, sb, S):
    T = x2d.shape[0]
    tm = ROW_TILE
    row = lambda w: pl.BlockSpec((tm, w), lambda i: (i, 0))
    n_s = S // tm
    tab = pl.BlockSpec((tm, LANES), lambda i: (i % n_s, 0))
    pairs = MLA_HEADS // 2
    return pl.pallas_call(
        _mla_in_kernel,
        grid=(T // tm,),
        in_specs=[row(D_MODEL), _full(win.shape), _full(qg.shape), _full(kvg.shape), _full(wq.shape),
                  _full(wk.shape), _full(wv.shape), tab, tab, tab],
        out_specs=[row(pairs * 2 * LANES), row(pairs * 2 * LANES), row(pairs * LANES)],
        out_shape=[jax.ShapeDtypeStruct((T, pairs * 2 * LANES), BF16),
                   jax.ShapeDtypeStruct((T, pairs * 2 * LANES), BF16),
                   jax.ShapeDtypeStruct((T, pairs * LANES), BF16)],
        compiler_params=_params("parallel"),
        name="mla_in_proj",
    )(x2d, win, qg, kvg, wq, wk, wv, c, sa, sb)


def _mla_attn_kernel(q_ref, k_ref, v_ref, o_ref):
    lane2 = lax.broadcasted_iota(jnp.int32, (1, 2 * LANES), 1)
    rl = lane2 - LANES
    in_a = (lane2 < MLA_NOPE) | ((rl >= 0) & (rl < 16)) | ((rl >= 32) & (rl < 48))
    in_b = ((lane2 >= MLA_NOPE) & (lane2 < LANES)) | ((rl >= 16) & (rl < 32)) | ((rl >= 48) & (rl < 64))
    lane = lax.broadcasted_iota(jnp.int32, (1, LANES), 1)
    q = q_ref[...]
    k = k_ref[...]
    v = v_ref[...]
    outs = []
    for sel in (in_a, in_b):
        qh = jnp.where(sel, q, jnp.zeros_like(q))
        s = _mm_nt(qh, k)
        m = jnp.max(s, axis=-1, keepdims=True)
        p = jnp.exp(s - m)
        l = jnp.sum(p, axis=-1, keepdims=True)
        outs.append(_mm(p.astype(BF16), v) / l)
    o_ref[...] = jnp.where(lane < MLA_V, outs[0], outs[1]).astype(o_ref.dtype)


def _mla_attn(q, k, v, B, S):
    pairs = MLA_HEADS // 2
    tq = min(ATT_TQ, S)
    return pl.pallas_call(
        _mla_attn_kernel,
        grid=(B, pairs, S // tq),
        in_specs=[pl.BlockSpec((None, tq, 2 * LANES), lambda b, g, i: (b, i, g)),
                  pl.BlockSpec((None, S, 2 * LANES), lambda b, g, i: (b, 0, g)),
                  pl.BlockSpec((None, S, LANES), lambda b, g, i: (b, 0, g))],
        out_specs=pl.BlockSpec((None, tq, LANES), lambda b, g, i: (b, i, g)),
        out_shape=jax.ShapeDtypeStruct((B, S, pairs * LANES), BF16),
        compiler_params=_params("parallel", "parallel", "arbitrary"),
        name="mla_attention",
    )(q, k, v)


def _prep_even(w_in, wa_f, ba_f, wa_b, ba_b, norm_g, w_out):
    o_q, o_k, o_v, o_r = 0, GLA_QK, 2 * GLA_QK, 2 * GLA_QK + GLA_VW
    o_af = o_r + GLA_VW
    o_ab = o_af + GLA_RANK
    o_qb = o_ab + GLA_RANK
    wa = w_in[:, :o_af].astype(BF16)
    wb = w_in[:, o_qb:].astype(BF16)
    wg = jnp.zeros((D_MODEL, LANES), F32).at[:, :2 * GLA_RANK].set(w_in[:, o_af:o_qb]).astype(BF16)
    wf = jnp.zeros((LANES, GLA_QK), F32).at[:GLA_RANK].set(wa_f)
    wb_gate = jnp.zeros((LANES, GLA_QK), F32).at[GLA_RANK:2 * GLA_RANK].set(wa_b)
    return dict(wa=wa, wb=wb, wg=wg, wf=wf, bf=ba_f.reshape(1, -1), wb_gate=wb_gate,
                bb=ba_b.reshape(1, -1), norm_g=norm_g.reshape(1, -1),
                wo_a=w_out[:GLA_VW].astype(BF16), wo_b=w_out[GLA_VW:].astype(BF16))


def _prep_odd(w_in, q_norm, kv_norm, w_uq, w_ukv, w_out, S):
    half = MLA_ROPE // 2
    pairs = MLA_HEADS // 2
    kr = w_in[:, MLA_Q_LORA + MLA_KV_LORA:]
    kr_rep = jnp.concatenate([kr[:, :half], kr[:, :half], kr[:, half:], kr[:, half:],
                              jnp.zeros((D_MODEL, LANES - 2 * MLA_ROPE), F32)], axis=1)
    win = jnp.concatenate([w_in[:, :MLA_Q_LORA + MLA_KV_LORA], kr_rep], axis=1).astype(BF16)
    uq = w_uq.reshape(MLA_Q_LORA, MLA_HEADS, MLA_NOPE + MLA_ROPE)
    nope = uq[:, :, :MLA_NOPE].reshape(MLA_Q_LORA, pairs, 2 * MLA_NOPE)
    r1 = uq[:, :, MLA_NOPE:MLA_NOPE + half].reshape(MLA_Q_LORA, pairs, 2 * half)
    r2 = uq[:, :, MLA_NOPE + half:].reshape(MLA_Q_LORA, pairs, 2 * half)
    pad = jnp.zeros((MLA_Q_LORA, pairs, LANES - 2 * MLA_ROPE), F32)
    wq = jnp.concatenate([nope, r1, r2, pad], axis=2).reshape(MLA_Q_LORA, pairs * 2 * LANES).astype(BF16)
    ukv = w_ukv.reshape(MLA_KV_LORA, MLA_HEADS, MLA_NOPE + MLA_V)
    wk = ukv[:, :, :MLA_NOPE].reshape(MLA_KV_LORA, MLA_HEADS * MLA_NOPE).astype(BF16)
    wv = ukv[:, :, MLA_NOPE:].reshape(MLA_KV_LORA, MLA_HEADS * MLA_V).astype(BF16)
    inv = 1.0 / (ROPE_BASE ** (jnp.arange(0, MLA_ROPE, 2, dtype=F32) / MLA_ROPE))
    ang = jnp.arange(S, dtype=F32)[:, None] * inv[None, :]
    cos, sin = jnp.cos(ang), jnp.sin(ang)
    z16 = jnp.zeros((S, half), F32)
    z64 = jnp.zeros((S, LANES - 2 * MLA_ROPE), F32)
    c = jnp.concatenate([cos, cos, cos, cos, z64], axis=1)
    sa = jnp.concatenate([-sin, -sin, z16, z16, z64], axis=1)
    sb = jnp.concatenate([z16, z16, sin, sin, z64], axis=1)
    return dict(win=win, qg=q_norm.reshape(1, -1), kvg=kv_norm.reshape(1, -1), wq=wq, wk=wk, wv=wv,
                c=c, sa=sa, sb=sb, wo=w_out.astype(BF16))


def _prep_moe(wg, bg, we, be, w1, w3, w2):
    wr = jnp.zeros((32, D_MODEL), F32).at[:MOE_GROUPS].set(wg.T).at[MOE_GROUPS:MOE_GROUPS + MOE_EXPERTS].set(we.T)
    br = jnp.zeros((32, 1), F32).at[:MOE_GROUPS, 0].set(bg).at[MOE_GROUPS:MOE_GROUPS + MOE_EXPERTS, 0].set(be)
    w13 = jnp.concatenate([w1, w3], axis=2).astype(BF16)
    return dict(wr=wr, br=br, w13=w13, w2=w2.astype(BF16))


def _trunk(x, ev, od, moe, ln1_g, ln1_b, ln2_g, ln2_b):
    B, S, D = x.shape
    T = B * S
    x2d = x.reshape(T, D)
    for i in range(DEPTH):
        if i % 2 == 0:
            p = ev[i // 2]
            h_a, h_b, gate = _even_in(x2d, p["wa"], p["wb"], p["wg"])
            o_a = _gla(h_a.reshape(B, S, -1), gate.reshape(B, S, -1), p["wf"], p["bf"], p["wb_gate"],
                       p["bb"], p["norm_g"], B, S)
            o_b = _dilated(h_b.reshape(B, S, -1), B, S)
            outs = [o_a.reshape(T, -1), o_b.reshape(T, -1)]
            ws = [p["wo_a"], p["wo_b"]]
        else:
            p = od[i // 2]
            q, k, v = _mla_in(x2d, p["win"], p["qg"], p["kvg"], p["wq"], p["wk"], p["wv"],
                              p["c"], p["sa"], p["sb"], S)
            o = _mla_attn(q.reshape(B, S, -1), k.reshape(B, S, -1), v.reshape(B, S, -1), B, S)
            outs = [o.reshape(T, -1)]
            ws = [p["wo"]]
        m = moe[i]
        x1, x1b, ids, wts = _proj_ln_route(x2d, outs, ws, ln1_g[i:i + 1], ln1_b[i:i + 1], m["wr"], m["br"])
        x2d = _hier_moe_ln(x1, x1b, ids, wts, m["w13"], m["w2"], ln2_g[i:i + 1], ln2_b[i:i + 1])
    return x2d.reshape(B, S, D)


def kernel(x_prompt, x_sample, ev_w_in, ev_wa_f, ev_ba_f, ev_wa_b, ev_ba_b, ev_gla_norm, ev_w_out,
           od_w_in, od_q_norm, od_kv_norm, od_w_uq, od_w_ukv, od_w_out, ln1_g, ln1_b, ln2_g, ln2_b,
           moe_wg, moe_bg, moe_we, moe_be, moe_w1, moe_w3, moe_w2):
    S = x_prompt.shape[1]
    ev = [_prep_even(ev_w_in[j], ev_wa_f[j], ev_ba_f[j], ev_wa_b[j], ev_ba_b[j], ev_gla_norm[j], ev_w_out[j])
          for j in range(ev_w_in.shape[0])]
    od = [_prep_odd(od_w_in[j], od_q_norm[j], od_kv_norm[j], od_w_uq[j], od_w_ukv[j], od_w_out[j], S)
          for j in range(od_w_in.shape[0])]
    moe = [_prep_moe(moe_wg[i], moe_bg[i], moe_we[i], moe_be[i], moe_w1[i], moe_w3[i], moe_w2[i])
           for i in range(DEPTH)]
    nb = x_prompt.shape[0]
    x = jnp.concatenate([x_prompt, x_sample], axis=0)
    y = _trunk(x, ev, od, moe, ln1_g, ln1_b, ln2_g, ln2_b)
    return (y[:nb], y[nb:])
```

```python
import functools

import numpy as np
import jax
import jax.numpy as jnp
from jax import lax
from jax.experimental import pallas as pl
from jax.experimental.pallas import tpu as pltpu

F32 = jnp.float32
BF16 = jnp.bfloat16
HIGHEST = lax.Precision.HIGHEST

D_MODEL = 1024
DEPTH = 2
GLA_HEADS, GLA_DK, GLA_DV = 4, 64, 128
GLA_QK, GLA_VW = GLA_HEADS * GLA_DK, GLA_HEADS * GLA_DV
GLA_RANK, GLA_TAU, GLA_CHUNK = 16, 16.0, 64
DIL_HEADS, DIL_DH = 8, 64
DIL_W = DIL_HEADS * DIL_DH
DIL_PATTERNS = ((128, 1), (512, 4), (2048, 16))
DIL_RADIUS = 64
MASK_VALUE = -1e30
MLA_HEADS, MLA_NOPE, MLA_ROPE, MLA_V = 16, 64, 32, 64
MLA_Q_LORA, MLA_KV_LORA = 384, 128
ROPE_BASE = 10000.0
MOE_GROUPS, MOE_EPG, MOE_EXPERTS, MOE_FF = 4, 4, 16, 512
ALPHA = (2 * DEPTH) ** 0.25
LN_EPS = 1e-5
RMS_EPS = 1e-6
LOG2_E = 1.4426950408889634

LANES = 128
VMEM_LIMIT = 56 * 1024 * 1024
ROW_TILE = 512
MOE_TILE = 256
ATT_TQ = 512
ATT_TK = 512
DIL_UNROLL = 4
GLA_UNROLL = 2


def _params(*sem):
    return pltpu.CompilerParams(dimension_semantics=sem, vmem_limit_bytes=VMEM_LIMIT)


def _full(shape):
    n = len(shape)
    return pl.BlockSpec(shape, lambda *_: (0,) * n)


def _mm(a, b):
    return jnp.dot(a, b, preferred_element_type=F32)


def _mm_nt(a, b):
    return lax.dot_general(a, b, (((1,), (1,)), ((), ())), preferred_element_type=F32)


def _mm_tn(a, b):
    return lax.dot_general(a, b, (((0,), (0,)), ((), ())), preferred_element_type=F32)


def _layer_norm(y, g, b):
    mu = jnp.mean(y, axis=-1, keepdims=True)
    yc = y - mu
    var = jnp.mean(yc * yc, axis=-1, keepdims=True)
    return yc * lax.rsqrt(var + LN_EPS) * g + b


def _log_sigmoid(z):
    return jnp.minimum(z, 0.0) - jnp.log(1.0 + jnp.exp(-jnp.abs(z)))


def _even_in_kernel(x_ref, wa_ref, wb_ref, wg_ref, oa_ref, ob_ref, og_ref):
    x = x_ref[...].astype(BF16)
    oa_ref[...] = _mm(x, wa_ref[...]).astype(oa_ref.dtype)
    ob_ref[...] = _mm(x, wb_ref[...])
    og_ref[...] = _mm(x, wg_ref[...])


def _even_in(x2d, wa, wb, wg):
    T = x2d.shape[0]
    tm = ROW_TILE
    return pl.pallas_call(
        _even_in_kernel,
        grid=(T // tm,),
        in_specs=[pl.BlockSpec((tm, D_MODEL), lambda i: (i, 0)),
                  _full(wa.shape), _full(wb.shape), _full(wg.shape)],
        out_specs=[pl.BlockSpec((tm, wa.shape[1]), lambda i: (i, 0)),
                   pl.BlockSpec((tm, wb.shape[1]), lambda i: (i, 0)),
                   pl.BlockSpec((tm, wg.shape[1]), lambda i: (i, 0))],
        out_shape=[jax.ShapeDtypeStruct((T, wa.shape[1]), BF16),
                   jax.ShapeDtypeStruct((T, wb.shape[1]), F32),
                   jax.ShapeDtypeStruct((T, wg.shape[1]), F32)],
        compiler_params=_params("parallel"),
        name="even_in_proj",
    )(x2d, wa, wb, wg)


def _gla_kernel(q_ref, k_ref, v_ref, r_ref, gate_ref, wf_ref, bf_ref, wb_ref, bb_ref, ng_ref,
                o_ref, laf_ref, lab_ref, of_ref, ob_ref, *, seq):
    C = GLA_CHUNK
    n_chunks = seq // C
    gate = gate_ref[...]
    zf = jnp.dot(gate, wf_ref[...], precision=HIGHEST, preferred_element_type=F32) + bf_ref[...]
    zb = jnp.dot(gate, wb_ref[...], precision=HIGHEST, preferred_element_type=F32) + bb_ref[...]
    laf_ref[...] = _log_sigmoid(zf) * (1.0 / GLA_TAU)
    lab_ref[...] = _log_sigmoid(zb) * (1.0 / GLA_TAU)

    row = lax.broadcasted_iota(jnp.int32, (C, C), 0)
    col = lax.broadcasted_iota(jnp.int32, (C, C), 1)
    lower = row >= col
    upper = col >= row
    cum_f = lower.astype(F32)
    cum_b = upper.astype(F32)
    lane = lax.broadcasted_iota(jnp.int32, (1, LANES), 1)
    head_lane = (lane < GLA_DK, lane >= GLA_DK)
    srow = lax.broadcasted_iota(jnp.int32, (2 * GLA_DV, LANES), 0)
    scol = lax.broadcasted_iota(jnp.int32, (2 * GLA_DV, LANES), 1)
    diag = (srow < GLA_DV) == (scol < GLA_DK)
    scale = GLA_DK ** -0.5

    def chunk(start, la_ref, cum, keep, last_row, state):
        rows = pl.ds(start, C)
        la = la_ref[rows, :]
        q = q_ref[rows, :].astype(F32) * scale
        k = k_ref[rows, :].astype(F32)
        v = v_ref[rows, :]
        b = jnp.dot(cum, la, precision=HIGHEST, preferred_element_type=F32)
        bl = b[last_row:last_row + 1, :]
        qd = (q * jnp.exp(b)).astype(BF16)
        kd = (k * jnp.exp(-b)).astype(BF16)
        ke = (k * jnp.exp(bl - b)).astype(BF16)
        o_inter = _mm_nt(qd, state.astype(BF16))
        parts = []
        for h in range(2):
            qh = jnp.where(head_lane[h], qd, jnp.zeros_like(qd))
            s = jnp.where(keep, _mm_nt(qh, kd), 0.0)
            parts.append(_mm(s.astype(BF16), v[:, h * GLA_DV:(h + 1) * GLA_DV]))
        o = o_inter + jnp.concatenate(parts, axis=1)
        upd = jnp.where(diag, _mm_tn(v, ke), 0.0)
        state = jnp.exp(bl) * state + upd
        return o, state

    def body(i, carry):
        sf, sb = carry
        start_f = pl.multiple_of(i * C, C)
        start_b = pl.multiple_of((n_chunks - 1 - i) * C, C)
        o_f, sf = chunk(start_f, laf_ref, cum_f, lower, C - 1, sf)
        o_b, sb = chunk(start_b, lab_ref, cum_b, upper, 0, sb)
        of_ref[pl.ds(start_f, C), :] = o_f
        ob_ref[pl.ds(start_b, C), :] = o_b
        return sf, sb

    zero = jnp.zeros((2 * GLA_DV, LANES), F32)
    lax.fori_loop(0, n_chunks, body, (zero, zero), unroll=GLA_UNROLL)

    blk = 256

    def finish(j, _):
        rows = pl.ds(pl.multiple_of(j * blk, blk), blk)
        o = of_ref[rows, :] + ob_ref[rows, :]
        g = ng_ref[...]
        outs = []
        for h in range(2):
            oh = o[:, h * GLA_DV:(h + 1) * GLA_DV]
            ms = jnp.mean(oh * oh, axis=-1, keepdims=True)
            outs.append(oh * lax.rsqrt(ms + RMS_EPS) * g[:, h * GLA_DV:(h + 1) * GLA_DV])
        r = r_ref[rows, :].astype(F32)
        o_ref[rows, :] = (jnp.concatenate(outs, axis=1) * (r * jax.nn.sigmoid(r))).astype(o_ref.dtype)
        return 0

    lax.fori_loop(0, seq // blk, finish, 0)


def _gla(h_a, gate, wf, bf, wb, bb, norm_g, B, S):
    pairs = GLA_HEADS // 2
    kq, kv = 2 * GLA_DK, 2 * GLA_DV
    sq = pl.BlockSpec((None, S, kq), lambda b, g: (b, 0, g))
    sk = pl.BlockSpec((None, S, kq), lambda b, g: (b, 0, pairs + g))
    sv = pl.BlockSpec((None, S, kv), lambda b, g: (b, 0, (2 * GLA_QK) // kv + g))
    sr = pl.BlockSpec((None, S, kv), lambda b, g: (b, 0, (2 * GLA_QK + GLA_VW) // kv + g))
    sg = pl.BlockSpec((None, S, LANES), lambda b, g: (b, 0, 0))
    sw = pl.BlockSpec((LANES, kq), lambda b, g: (0, g))
    sb = pl.BlockSpec((1, kq), lambda b, g: (0, g))
    sn = pl.BlockSpec((1, kv), lambda b, g: (0, g))
    return pl.pallas_call(
        functools.partial(_gla_kernel, seq=S),
        grid=(B, pairs),
        in_specs=[sq, sk, sv, sr, sg, sw, sb, sw, sb, sn],
        out_specs=pl.BlockSpec((None, S, kv), lambda b, g: (b, 0, g)),
        out_shape=jax.ShapeDtypeStruct((B, S, GLA_VW), BF16),
        scratch_shapes=[pltpu.VMEM((S, kq), F32), pltpu.VMEM((S, kq), F32),
                        pltpu.VMEM((S, kv), F32), pltpu.VMEM((S, kv), F32)],
        compiler_params=_params("parallel", "parallel"),
        name="gla_mixer",
    )(h_a, h_a, h_a, h_a, gate, wf, bf, wb, bb, norm_g)


def _dil_kernel(q_ref, k_ref, v_ref, o_ref, m_ref, l_ref, acc_ref, *, seq):
    R = DIL_RADIUS
    n_blocks = seq // R
    pair = pl.program_id(1)
    lane = lax.broadcasted_iota(jnp.int32, (1, LANES), 1)
    first_head = lane < DIL_DH
    qi = lax.broadcasted_iota(jnp.int32, (R, 3 * R), 0)
    kj = lax.broadcasted_iota(jnp.int32, (R, 3 * R), 1)
    rel = kj - R - qi
    band = jnp.abs(rel) <= R
    dist = jnp.abs(rel).astype(F32)
    slopes = [1.0 / jnp.left_shift(jnp.ones((R, 3 * R), jnp.int32), 2 * pair + h + 1).astype(F32)
              for h in range(2)]
    scale = DIL_DH ** -0.5

    for p_idx, (window, dil) in enumerate(DIL_PATTERNS):
        assert window // (2 * dil) == R
        nb = n_blocks // dil
        shift = nb.bit_length() - 1
        assert nb == 1 << shift
        bias = [-(dil * dist) * slopes[h] for h in range(2)]

        def rows_at(block, dil=dil):
            return pl.ds(block, R, stride=dil) if dil > 1 else pl.ds(pl.multiple_of(block, R), R)

        def body(idx, _, dil=dil, nb=nb, shift=shift, bias=bias, p_idx=p_idx, rows_at=rows_at):
            res = lax.shift_right_logical(idx, shift)
            n = jnp.bitwise_and(idx, nb - 1)
            start = res + dil * R * n
            prev = res + dil * R * jnp.maximum(n - 1, 0)
            nxt = res + dil * R * jnp.minimum(n + 1, nb - 1)
            q = (q_ref[rows_at(start), :] * scale).astype(BF16)
            kcat = jnp.concatenate([k_ref[rows_at(prev), :], k_ref[rows_at(start), :],
                                    k_ref[rows_at(nxt), :]], axis=0).astype(BF16)
            vcat = jnp.concatenate([v_ref[rows_at(prev), :], v_ref[rows_at(start), :],
                                    v_ref[rows_at(nxt), :]], axis=0).astype(BF16)
            valid = band & ((kj >= R) | (n > 0)) & ((kj < 2 * R) | (n < nb - 1))
            s = []
            for h in range(2):
                qh = jnp.where(first_head if h == 0 else ~first_head, q, jnp.zeros_like(q))
                s.append(jnp.where(valid, _mm_nt(qh, kcat) + bias[h], MASK_VALUE))
            m0 = jnp.max(s[0], axis=-1, keepdims=True)
            m1 = jnp.max(s[1], axis=-1, keepdims=True)
            p = [jnp.exp(s[0] - m0), jnp.exp(s[1] - m1)]
            m_ref[p_idx, rows_at(start), :] = jnp.where(first_head, m0, m1)
            l_ref[p_idx, rows_at(start), :] = jnp.where(first_head, jnp.sum(p[0], axis=-1, keepdims=True),
                                                        jnp.sum(p[1], axis=-1, keepdims=True))
            acc_ref[p_idx, rows_at(start), :] = jnp.where(first_head, _mm(p[0].astype(BF16), vcat),
                                                          _mm(p[1].astype(BF16), vcat))
            return 0

        lax.fori_loop(0, n_blocks, body, 0, unroll=DIL_UNROLL)

    blk = 256
    n_pat = len(DIL_PATTERNS)

    def finish(j, _):
        rows = pl.ds(pl.multiple_of(j * blk, blk), blk)
        ms = [m_ref[p, rows, :] for p in range(n_pat)]
        m = functools.reduce(jnp.maximum, ms)
        ws = [jnp.exp(x - m) for x in ms]
        l = functools.reduce(lambda a, b: a + b, [w * l_ref[p, rows, :] for p, w in enumerate(ws)])
        acc = functools.reduce(lambda a, b: a + b, [w * acc_ref[p, rows, :] for p, w in enumerate(ws)])
        o_ref[rows, :] = (acc / l).astype(o_ref.dtype)
        return 0

    lax.fori_loop(0, seq // blk, finish, 0)


def _dilated(h_b, B, S):
    pairs = DIL_HEADS // 2
    sq = pl.BlockSpec((None, S, LANES), lambda b, g: (b, 0, g))
    sk = pl.BlockSpec((None, S, LANES), lambda b, g: (b, 0, pairs + g))
    sv = pl.BlockSpec((None, S, LANES), lambda b, g: (b, 0, 2 * pairs + g))
    return pl.pallas_call(
        functools.partial(_dil_kernel, seq=S),
        grid=(B, pairs),
        in_specs=[sq, sk, sv],
        out_specs=pl.BlockSpec((None, S, LANES), lambda b, g: (b, 0, g)),
        out_shape=jax.ShapeDtypeStruct((B, S, DIL_W), BF16),
        scratch_shapes=[pltpu.VMEM((len(DIL_PATTERNS), S, LANES), F32)] * 3,
        compiler_params=_params("parallel", "parallel"),
        name="dilated_mixer",
    )(h_b, h_b, h_b)


def _route(lt):
    g = [lt[i:i + 1, :] for i in range(MOE_GROUPS)]
    gmax = functools.reduce(jnp.maximum, g)
    gexp = [jnp.exp(x - gmax) for x in g]
    gsum = functools.reduce(lambda a, b: a + b, gexp)
    gprob = [x / gsum for x in gexp]
    g_val, g_idx = gprob[0], jnp.zeros_like(gprob[0], dtype=jnp.int32)
    for i in range(1, MOE_GROUPS):
        better = gprob[i] > g_val
        g_idx = jnp.where(better, i, g_idx)
        g_val = jnp.where(better, gprob[i], g_val)
    e = []
    for j in range(MOE_EPG):
        x = lt[MOE_GROUPS + j:MOE_GROUPS + j + 1, :]
        for gi in range(1, MOE_GROUPS):
            r0 = MOE_GROUPS + gi * MOE_EPG + j
            x = jnp.where(g_idx == gi, lt[r0:r0 + 1, :], x)
        e.append(x)
    emax = functools.reduce(jnp.maximum, e)
    eexp = [jnp.exp(x - emax) for x in e]
    esum = functools.reduce(lambda a, b: a + b, eexp)
    eprob = [x / esum for x in eexp]
    v1, i1 = eprob[0], jnp.zeros_like(g_idx)
    for j in range(1, MOE_EPG):
        better = eprob[j] > v1
        i1 = jnp.where(better, j, i1)
        v1 = jnp.where(better, eprob[j], v1)
    v2, i2 = jnp.full_like(v1, -1.0), jnp.zeros_like(g_idx)
    for j in range(MOE_EPG):
        cand = jnp.where(i1 == j, -1.0, eprob[j])
        better = cand > v2
        i2 = jnp.where(better, j, i2)
        v2 = jnp.where(better, cand, v2)
    den = v1 + v2
    ids = jnp.concatenate([g_idx * MOE_EPG + i1, g_idx * MOE_EPG + i2], axis=0)
    wts = jnp.concatenate([g_val * (v1 / den), g_val * (v2 / den)], axis=0)
    return ids, wts


def _proj_ln_route_kernel(*refs, n_in):
    x_ref = refs[0]
    o_refs = refs[1:1 + n_in]
    w_refs = refs[1 + n_in:1 + 2 * n_in]
    g_ref, b_ref, wr_ref, br_ref, x1_ref, x1b_ref, ids_ref, wts_ref = refs[1 + 2 * n_in:]
    m = _mm(o_refs[0][...], w_refs[0][...])
    for o_ref, w_ref in zip(o_refs[1:], w_refs[1:]):
        m = m + _mm(o_ref[...], w_ref[...])
    x1 = _layer_norm(ALPHA * x_ref[...] + m, g_ref[...], b_ref[...])
    x1_ref[...] = x1
    x1b_ref[...] = x1.astype(BF16)
    lt = lax.dot_general(wr_ref[...], x1, (((1,), (1,)), ((), ())), precision=HIGHEST,
                         preferred_element_type=F32) + br_ref[...]
    ids, wts = _route(lt)
    ids_ref[...] = ids
    wts_ref[...] = wts


def _proj_ln_route(x2d, outs, ws, ln_g, ln_b, wr, br):
    T = x2d.shape[0]
    tm = ROW_TILE
    n_in = len(outs)
    row = lambda w: pl.BlockSpec((tm, w), lambda i: (i, 0))
    tok = pl.BlockSpec((2, tm), lambda i: (0, i))
    return pl.pallas_call(
        functools.partial(_proj_ln_route_kernel, n_in=n_in),
        grid=(T // tm,),
        in_specs=[row(D_MODEL)] + [row(o.shape[1]) for o in outs] + [_full(w.shape) for w in ws]
                 + [_full(ln_g.shape), _full(ln_b.shape), _full(wr.shape), _full(br.shape)],
        out_specs=[row(D_MODEL), row(D_MODEL), tok, tok],
        out_shape=[jax.ShapeDtypeStruct((T, D_MODEL), F32), jax.ShapeDtypeStruct((T, D_MODEL), BF16),
                   jax.ShapeDtypeStruct((2, T), jnp.int32), jax.ShapeDtypeStruct((2, T), F32)],
        compiler_params=_params("parallel"),
        name="proj_ln_route",
    )(x2d, *outs, *ws, ln_g, ln_b, wr, br)


def _moe_kernel(te_ref, nu_ref, xs_ref, w13_ref, w2_ref, rw_ref, ys_ref):
    i = pl.program_id(0)

    @pl.when(i < nu_ref[0])
    def _():
        h = _mm(xs_ref[...], w13_ref[...])
        h1, h3 = h[:, :MOE_FF], h[:, MOE_FF:]
        hidden = (h1 * jax.nn.sigmoid(h1)) * h3
        y = _mm(hidden.astype(BF16), w2_ref[...])
        ys_ref[...] = (y * rw_ref[...]).astype(ys_ref.dtype)

    @pl.when(i >= nu_ref[0])
    def _():
        ys_ref[...] = jnp.zeros_like(ys_ref)


def _moe_experts(xs, w13, w2, row_w, tile_expert, n_used):
    P = xs.shape[0]
    tm = MOE_TILE
    grid_spec = pltpu.PrefetchScalarGridSpec(
        num_scalar_prefetch=2,
        grid=(P // tm,),
        in_specs=[pl.BlockSpec((tm, D_MODEL), lambda i, te, nu: (i, 0)),
                  pl.BlockSpec((None, D_MODEL, 2 * MOE_FF), lambda i, te, nu: (te[i], 0, 0)),
                  pl.BlockSpec((None, MOE_FF, D_MODEL), lambda i, te, nu: (te[i], 0, 0)),
                  pl.BlockSpec((tm, 1), lambda i, te, nu: (i, 0))],
        out_specs=pl.BlockSpec((tm, D_MODEL), lambda i, te, nu: (i, 0)),
    )
    return pl.pallas_call(
        _moe_kernel,
        grid_spec=grid_spec,
        out_shape=jax.ShapeDtypeStruct((P, D_MODEL), BF16),
        compiler_params=_params("arbitrary"),
        name="moe_experts",
    )(tile_expert, n_used, xs, w13, w2, row_w)


def _moe_plan(ids, wts, T):
    tm = MOE_TILE
    n_assign = 2 * T
    e_flat = ids.reshape(n_assign)
    onehot = (e_flat[:, None] == jnp.arange(MOE_EXPERTS, dtype=jnp.int32)[None, :]).astype(jnp.int32)
    csum = jnp.cumsum(onehot, axis=0)
    rank = jnp.sum((csum - onehot) * onehot, axis=1)
    counts = csum[-1]
    padded = ((counts + tm - 1) // tm) * tm
    ends = jnp.cumsum(padded)
    offs = ends - padded
    pos = offs[e_flat] + rank
    P = n_assign + MOE_EXPERTS * tm
    n_tiles = P // tm
    tile_start = jnp.arange(n_tiles, dtype=jnp.int32) * tm
    tile_expert = jnp.minimum(jnp.sum((tile_start[:, None] >= ends[None, :]).astype(jnp.int32), axis=1),
                              MOE_EXPERTS - 1).astype(jnp.int32)
    n_used = (ends[-1:] // tm).astype(jnp.int32)
    tok = jnp.arange(n_assign, dtype=jnp.int32) % T
    _, tok_sorted, w_sorted = lax.sort((e_flat, tok, wts.reshape(n_assign)), num_keys=1, is_stable=True)
    slot_e = jnp.repeat(tile_expert, tm)
    r = jnp.arange(P, dtype=jnp.int32) - offs[slot_e]
    valid = r < counts[slot_e]
    u = jnp.clip((jnp.cumsum(counts) - counts)[slot_e] + r, 0, n_assign - 1)
    src_tok = jnp.where(valid, tok_sorted.at[u].get(mode="promise_in_bounds"), 0)
    row_w = jnp.where(valid, w_sorted.at[u].get(mode="promise_in_bounds"), 0.0)
    return src_tok, row_w.reshape(P, 1), tile_expert, n_used, pos


def _combine_ln_kernel(x_ref, y0_ref, y1_ref, g_ref, b_ref, o_ref):
    y = ALPHA * x_ref[...] + y0_ref[...].astype(F32) + y1_ref[...].astype(F32)
    o_ref[...] = _layer_norm(y, g_ref[...], b_ref[...])


def _combine_ln(x2d, y0, y1, ln_g, ln_b):
    T = x2d.shape[0]
    tm = ROW_TILE
    row = pl.BlockSpec((tm, D_MODEL), lambda i: (i, 0))
    return pl.pallas_call(
        _combine_ln_kernel,
        grid=(T // tm,),
        in_specs=[row, row, row, _full(ln_g.shape), _full(ln_b.shape)],
        out_specs=row,
        out_shape=jax.ShapeDtypeStruct((T, D_MODEL), F32),
        compiler_params=_params("parallel"),
        name="combine_ln",
    )(x2d, y0, y1, ln_g, ln_b)


def _hier_moe_ln(x1, x1b, ids, wts, w13, w2, ln_g, ln_b):
    T = x1.shape[0]
    src_tok, row_w, tile_expert, n_used, pos = _moe_plan(ids, wts, T)
    xs = x1b.at[src_tok].get(mode="promise_in_bounds")
    ys = _moe_experts(xs, w13, w2, row_w, tile_expert, n_used)
    y0 = ys.at[pos[:T]].get(mode="promise_in_bounds")
    y1 = ys.at[pos[T:]].get(mode="promise_in_bounds")
    return _combine_ln(x1, y0, y1, ln_g, ln_b)


def _rope_block(x, c, sa, sb):
    return x * c + pltpu.roll(x, LANES - 32, 1) * sa + pltpu.roll(x, 32, 1) * sb


def _mla_in_kernel(x_ref, win_ref, qg_ref, kvg_ref, wq_ref, wk_ref, wv_ref, c_ref, sa_ref, sb_ref,
                   q_ref, k_ref, v_ref):
    x = x_ref[...].astype(BF16)
    h = _mm(x, win_ref[...])
    c_q = h[:, :MLA_Q_LORA]
    c_kv = h[:, MLA_Q_LORA:MLA_Q_LORA + MLA_KV_LORA]
    k_rope = h[:, MLA_Q_LORA + MLA_KV_LORA:]
    cqn = c_q * lax.rsqrt(jnp.mean(c_q * c_q, axis=-1, keepdims=True) + RMS_EPS) * qg_ref[...]
    ckn = c_kv * lax.rsqrt(jnp.mean(c_kv * c_kv, axis=-1, keepdims=True) + RMS_EPS) * kvg_ref[...]
    cqn, ckn = cqn.astype(BF16), ckn.astype(BF16)
    scale = (MLA_NOPE + MLA_ROPE) ** -0.5 * LOG2_E
    q = _mm(cqn, wq_ref[...]) * scale
    kn = _mm(ckn, wk_ref[...])
    v = _mm(ckn, wv_ref[...]).astype(v_ref.dtype)
    c, sa, sb = c_ref[...], sa_ref[...], sb_ref[...]
    kr = _rope_block(k_rope, c, sa, sb).astype(k_ref.dtype)
    ones = jnp.ones((x.shape[0], LANES), v_ref.dtype)
    for g in range(MLA_HEADS // 2):
        lo = 2 * LANES * g
        v_ref[:, lo:lo + LANES] = v[:, LANES * g:LANES * (g + 1)]
        v_ref[:, lo + LANES:lo + 2 * LANES] = ones
        q_ref[:, lo:lo + LANES] = q[:, lo:lo + LANES].astype(q_ref.dtype)
        q_ref[:, lo + LANES:lo + 2 * LANES] = _rope_block(q[:, lo + LANES:lo + 2 * LANES], c, sa, sb
                                                          ).astype(q_ref.dtype)
        k_ref[:, lo:lo + LANES] = kn[:, LANES * g:LANES * (g + 1)].astype(k_ref.dtype)
        k_ref[:, lo + LANES:lo + 2 * LANES] = kr


def _mla_in(x2d, win, qg, kvg, wq, wk, wv, c, sa, sb, S):
    T = x2d.shape[0]
    tm = ROW_TILE
    row = lambda w: pl.BlockSpec((tm, w), lambda i: (i, 0))
    n_s = S // tm
    tab = pl.BlockSpec((tm, LANES), lambda i: (i % n_s, 0))
    pairs = MLA_HEADS // 2
    return pl.pallas_call(
        _mla_in_kernel,
        grid=(T // tm,),
        in_specs=[row(D_MODEL), _full(win.shape), _full(qg.shape), _full(kvg.shape), _full(wq.shape),
                  _full(wk.shape), _full(wv.shape), tab, tab, tab],
        out_specs=[row(pairs * 2 * LANES)] * 3,
        out_shape=[jax.ShapeDtypeStruct((T, pairs * 2 * LANES), BF16)] * 3,
        compiler_params=_params("parallel"),
        name="mla_in_proj",
    )(x2d, win, qg, kvg, wq, wk, wv, c, sa, sb)


def _mla_attn_kernel(q_ref, k_ref, v_ref, o_ref, sa_ref, sb_ref, *, seq):
    lane2 = lax.broadcasted_iota(jnp.int32, (1, 2 * LANES), 1)
    rl = lane2 - LANES
    in_a = (lane2 < MLA_NOPE) | ((rl >= 0) & (rl < 16)) | ((rl >= 32) & (rl < 48))
    in_b = ((lane2 >= MLA_NOPE) & (lane2 < LANES)) | ((rl >= 16) & (rl < 32)) | ((rl >= 48) & (rl < 64))
    lane = lax.broadcasted_iota(jnp.int32, (1, LANES), 1)
    q = q_ref[...]
    tq = q.shape[0]
    tk = min(ATT_TK, seq)
    n_k = seq // tk
    q_heads = (jnp.where(in_a, q, jnp.zeros_like(q)), jnp.where(in_b, q, jnp.zeros_like(q)))
    s_refs = (sa_ref, sb_ref)

    def score_chunk(h, j, m):
        s = _mm_nt(q_heads[h], k_ref[j * tk:(j + 1) * tk, :])
        s_refs[h][:, j * tk:(j + 1) * tk] = s
        for c in range(tk // LANES):
            m = jnp.maximum(m, s[:, c * LANES:(c + 1) * LANES])
        return m

    def value_chunk(h, j, m_row, acc):
        p = jnp.exp2(s_refs[h][:, j * tk:(j + 1) * tk] - m_row).astype(BF16)
        return acc + _mm(p, v_ref[j * tk:(j + 1) * tk, :])

    neg = jnp.full((tq, LANES), -jnp.inf, F32)
    m = neg
    for j in range(n_k):
        m = score_chunk(0, j, m)
    m_a = jnp.max(m, axis=-1, keepdims=True)
    acc_a = jnp.zeros((tq, 2 * LANES), F32)
    m = neg
    for j in range(n_k):
        acc_a = value_chunk(0, j, m_a, acc_a)
        m = score_chunk(1, j, m)
    m_b = jnp.max(m, axis=-1, keepdims=True)
    acc_b = jnp.zeros((tq, 2 * LANES), F32)
    for j in range(n_k):
        acc_b = value_chunk(1, j, m_b, acc_b)
    out_a = acc_a[:, :LANES] / acc_a[:, LANES:]
    out_b = acc_b[:, :LANES] / acc_b[:, LANES:]
    o_ref[...] = jnp.where(lane < MLA_V, out_a, out_b).astype(o_ref.dtype)


def _mla_attn(q, k, v, B, S):
    pairs = MLA_HEADS // 2
    tq = min(ATT_TQ, S)
    return pl.pallas_call(
        functools.partial(_mla_attn_kernel, seq=S),
        grid=(B, pairs, S // tq),
        in_specs=[pl.BlockSpec((None, tq, 2 * LANES), lambda b, g, i: (b, i, g)),
                  pl.BlockSpec((None, S, 2 * LANES), lambda b, g, i: (b, 0, g)),
                  pl.BlockSpec((None, S, 2 * LANES), lambda b, g, i: (b, 0, g))],
        out_specs=pl.BlockSpec((None, tq, LANES), lambda b, g, i: (b, i, g)),
        out_shape=jax.ShapeDtypeStruct((B, S, pairs * LANES), BF16),
        scratch_shapes=[pltpu.VMEM((tq, S), F32), pltpu.VMEM((tq, S), F32)],
        compiler_params=_params("parallel", "parallel", "arbitrary"),
        name="mla_attention",
    )(q, k, v)


def _prep_even(w_in, wa_f, ba_f, wa_b, ba_b, norm_g, w_out):
    o_q, o_k, o_v, o_r = 0, GLA_QK, 2 * GLA_QK, 2 * GLA_QK + GLA_VW
    o_af = o_r + GLA_VW
    o_ab = o_af + GLA_RANK
    o_qb = o_ab + GLA_RANK
    wa = w_in[:, :o_af].astype(BF16)
    wb = w_in[:, o_qb:].astype(BF16)
    wg = jnp.zeros((D_MODEL, LANES), F32).at[:, :2 * GLA_RANK].set(w_in[:, o_af:o_qb]).astype(BF16)
    wf = jnp.zeros((LANES, GLA_QK), F32).at[:GLA_RANK].set(wa_f)
    wb_gate = jnp.zeros((LANES, GLA_QK), F32).at[GLA_RANK:2 * GLA_RANK].set(wa_b)
    return dict(wa=wa, wb=wb, wg=wg, wf=wf, bf=ba_f.reshape(1, -1), wb_gate=wb_gate,
                bb=ba_b.reshape(1, -1), norm_g=norm_g.reshape(1, -1),
                wo_a=w_out[:GLA_VW].astype(BF16), wo_b=w_out[GLA_VW:].astype(BF16))


def _prep_odd(w_in, q_norm, kv_norm, w_uq, w_ukv, w_out, S):
    half = MLA_ROPE // 2
    pairs = MLA_HEADS // 2
    kr = w_in[:, MLA_Q_LORA + MLA_KV_LORA:]
    kr_rep = jnp.concatenate([kr[:, :half], kr[:, :half], kr[:, half:], kr[:, half:],
                              jnp.zeros((D_MODEL, LANES - 2 * MLA_ROPE), F32)], axis=1)
    win = jnp.concatenate([w_in[:, :MLA_Q_LORA + MLA_KV_LORA], kr_rep], axis=1).astype(BF16)
    uq = w_uq.reshape(MLA_Q_LORA, MLA_HEADS, MLA_NOPE + MLA_ROPE)
    nope = uq[:, :, :MLA_NOPE].reshape(MLA_Q_LORA, pairs, 2 * MLA_NOPE)
    r1 = uq[:, :, MLA_NOPE:MLA_NOPE + half].reshape(MLA_Q_LORA, pairs, 2 * half)
    r2 = uq[:, :, MLA_NOPE + half:].reshape(MLA_Q_LORA, pairs, 2 * half)
    pad = jnp.zeros((MLA_Q_LORA, pairs, LANES - 2 * MLA_ROPE), F32)
    wq = jnp.concatenate([nope, r1, r2, pad], axis=2).reshape(MLA_Q_LORA, pairs * 2 * LANES).astype(BF16)
    ukv = w_ukv.reshape(MLA_KV_LORA, MLA_HEADS, MLA_NOPE + MLA_V)
    wk = ukv[:, :, :MLA_NOPE].reshape(MLA_KV_LORA, MLA_HEADS * MLA_NOPE).astype(BF16)
    wv = ukv[:, :, MLA_NOPE:].reshape(MLA_KV_LORA, MLA_HEADS * MLA_V).astype(BF16)
    inv = 1.0 / (ROPE_BASE ** (jnp.arange(0, MLA_ROPE, 2, dtype=F32) / MLA_ROPE))
    ang = jnp.arange(S, dtype=F32)[:, None] * inv[None, :]
    cos, sin = jnp.cos(ang), jnp.sin(ang)
    z16 = jnp.zeros((S, half), F32)
    z64 = jnp.zeros((S, LANES - 2 * MLA_ROPE), F32)
    c = jnp.concatenate([cos, cos, cos, cos, z64], axis=1)
    sa = jnp.concatenate([-sin, -sin, z16, z16, z64], axis=1)
    sb = jnp.concatenate([z16, z16, sin, sin, z64], axis=1)
    return dict(win=win, qg=q_norm.reshape(1, -1), kvg=kv_norm.reshape(1, -1), wq=wq, wk=wk, wv=wv,
                c=c, sa=sa, sb=sb, wo=w_out.astype(BF16))


def _prep_moe(wg, bg, we, be, w1, w3, w2):
    wr = jnp.zeros((32, D_MODEL), F32).at[:MOE_GROUPS].set(wg.T).at[MOE_GROUPS:MOE_GROUPS + MOE_EXPERTS].set(we.T)
    br = jnp.zeros((32, 1), F32).at[:MOE_GROUPS, 0].set(bg).at[MOE_GROUPS:MOE_GROUPS + MOE_EXPERTS, 0].set(be)
    w13 = jnp.concatenate([w1, w3], axis=2).astype(BF16)
    return dict(wr=wr, br=br, w13=w13, w2=w2.astype(BF16))


def _trunk(x, ev, od, moe, ln1_g, ln1_b, ln2_g, ln2_b):
    B, S, D = x.shape
    T = B * S
    x2d = x.reshape(T, D)
    for i in range(DEPTH):
        if i % 2 == 0:
            p = ev[i // 2]
            h_a, h_b, gate = _even_in(x2d, p["wa"], p["wb"], p["wg"])
            o_a = _gla(h_a.reshape(B, S, -1), gate.reshape(B, S, -1), p["wf"], p["bf"], p["wb_gate"],
                       p["bb"], p["norm_g"], B, S)
            o_b = _dilated(h_b.reshape(B, S, -1), B, S)
            outs = [o_a.reshape(T, -1), o_b.reshape(T, -1)]
            ws = [p["wo_a"], p["wo_b"]]
        else:
            p = od[i // 2]
            q, k, v = _mla_in(x2d, p["win"], p["qg"], p["kvg"], p["wq"], p["wk"], p["wv"],
                              p["c"], p["sa"], p["sb"], S)
            o = _mla_attn(q.reshape(B, S, -1), k.reshape(B, S, -1), v.reshape(B, S, -1), B, S)
            outs = [o.reshape(T, -1)]
            ws = [p["wo"]]
        m = moe[i]
        x1, x1b, ids, wts = _proj_ln_route(x2d, outs, ws, ln1_g[i:i + 1], ln1_b[i:i + 1], m["wr"], m["br"])
        x2d = _hier_moe_ln(x1, x1b, ids, wts, m["w13"], m["w2"], ln2_g[i:i + 1], ln2_b[i:i + 1])
    return x2d.reshape(B, S, D)


def kernel(x_prompt, x_sample, ev_w_in, ev_wa_f, ev_ba_f, ev_wa_b, ev_ba_b, ev_gla_norm, ev_w_out,
           od_w_in, od_q_norm, od_kv_norm, od_w_uq, od_w_ukv, od_w_out, ln1_g, ln1_b, ln2_g, ln2_b,
           moe_wg, moe_bg, moe_we, moe_be, moe_w1, moe_w3, moe_w2):
    S = x_prompt.shape[1]
    ev = [_prep_even(ev_w_in[j], ev_wa_f[j], ev_ba_f[j], ev_wa_b[j], ev_ba_b[j], ev_gla_norm[j], ev_w_out[j])
          for j in range(ev_w_in.shape[0])]
    od = [_prep_odd(od_w_in[j], od_q_norm[j], od_kv_norm[j], od_w_uq[j], od_w_ukv[j], od_w_out[j], S)
          for j in range(od_w_in.shape[0])]
    moe = [_prep_moe(moe_wg[i], moe_bg[i], moe_we[i], moe_be[i], moe_w1[i], moe_w3[i], moe_w2[i])
           for i in range(DEPTH)]
    return tuple(_trunk(x, ev, od, moe, ln1_g, ln1_b, ln2_g, ln2_b) for x in (x_prompt, x_sample))
```

```python
import functools

import numpy as np
import jax
import jax.numpy as jnp
from jax import lax
from jax.experimental import pallas as pl
from jax.experimental.pallas import tpu as pltpu

F32 = jnp.float32
BF16 = jnp.bfloat16
HIGHEST = lax.Precision.HIGHEST

D_MODEL = 1024
DEPTH = 2
GLA_HEADS, GLA_DK, GLA_DV = 4, 64, 128
GLA_QK, GLA_VW = GLA_HEADS * GLA_DK, GLA_HEADS * GLA_DV
GLA_RANK, GLA_TAU, GLA_CHUNK = 16, 16.0, 64
DIL_HEADS, DIL_DH = 8, 64
DIL_W = DIL_HEADS * DIL_DH
DIL_PATTERNS = ((128, 1), (512, 4), (2048, 16))
DIL_RADIUS = 64
MASK_VALUE = -1e30
MLA_HEADS, MLA_NOPE, MLA_ROPE, MLA_V = 16, 64, 32, 64
MLA_Q_LORA, MLA_KV_LORA = 384, 128
ROPE_BASE = 10000.0
MOE_GROUPS, MOE_EPG, MOE_EXPERTS, MOE_FF = 4, 4, 16, 512
ALPHA = (2 * DEPTH) ** 0.25
LN_EPS = 1e-5
RMS_EPS = 1e-6
LOG2_E = 1.4426950408889634

LANES = 128
VMEM_LIMIT = 56 * 1024 * 1024
ROW_TILE = 512
MOE_TILE = 256
ATT_TQ = 512
ATT_TK = 512
DIL_UNROLL = 2
GLA_UNROLL = 2
GLA_TILE = 256
DIL_TQ = 256


def _params(*sem):
    return pltpu.CompilerParams(dimension_semantics=sem, vmem_limit_bytes=VMEM_LIMIT)


def _full(shape):
    n = len(shape)
    return pl.BlockSpec(shape, lambda *_: (0,) * n)


def _mm(a, b):
    return jnp.dot(a, b, preferred_element_type=F32)


def _mm_nt(a, b):
    return lax.dot_general(a, b, (((1,), (1,)), ((), ())), preferred_element_type=F32)


def _mm_tn(a, b):
    return lax.dot_general(a, b, (((0,), (0,)), ((), ())), preferred_element_type=F32)


def _layer_norm(y, g, b):
    mu = jnp.mean(y, axis=-1, keepdims=True)
    yc = y - mu
    var = jnp.mean(yc * yc, axis=-1, keepdims=True)
    return yc * lax.rsqrt(var + LN_EPS) * g + b


def _log_sigmoid(z):
    return jnp.minimum(z, 0.0) - jnp.log(1.0 + jnp.exp(-jnp.abs(z)))


def _even_in_kernel(x_ref, wa_ref, wb_ref, wg_ref, oa_ref, ob_ref, og_ref):
    x = x_ref[...].astype(BF16)
    oa_ref[...] = _mm(x, wa_ref[...]).astype(oa_ref.dtype)
    ob_ref[...] = _mm(x, wb_ref[...])
    og_ref[...] = _mm(x, wg_ref[...])


def _even_in(x2d, wa, wb, wg):
    T = x2d.shape[0]
    tm = ROW_TILE
    return pl.pallas_call(
        _even_in_kernel,
        grid=(T // tm,),
        in_specs=[pl.BlockSpec((tm, D_MODEL), lambda i: (i, 0)),
                  _full(wa.shape), _full(wb.shape), _full(wg.shape)],
        out_specs=[pl.BlockSpec((tm, wa.shape[1]), lambda i: (i, 0)),
                   pl.BlockSpec((tm, wb.shape[1]), lambda i: (i, 0)),
                   pl.BlockSpec((tm, wg.shape[1]), lambda i: (i, 0))],
        out_shape=[jax.ShapeDtypeStruct((T, wa.shape[1]), BF16),
                   jax.ShapeDtypeStruct((T, wb.shape[1]), F32),
                   jax.ShapeDtypeStruct((T, wg.shape[1]), F32)],
        compiler_params=_params("parallel"),
        name="even_in_proj",
    )(x2d, wa, wb, wg)


def _gla_kernel(q_ref, k_ref, v_ref, r_ref, gate_ref, wf_ref, bf_ref, wb_ref, bb_ref, ng_ref,
                o_ref, la_ref, qd_ref, ke_ref, vt_ref, tot_ref, of_ref, ob_ref, *, seq):
    C = GLA_CHUNK
    n_chunks = seq // C
    tile = min(GLA_TILE, seq)
    cpt = tile // C
    gate = gate_ref[...]
    gate_hi = gate.astype(BF16)
    gate_lo = (gate - gate_hi.astype(F32)).astype(BF16)
    for d, (w_ref, b_ref) in enumerate(((wf_ref, bf_ref), (wb_ref, bb_ref))):
        w = w_ref[...]
        w_hi = w.astype(BF16)
        w_lo = (w - w_hi.astype(F32)).astype(BF16)
        z = _mm(gate_hi, w_hi) + _mm(gate_hi, w_lo) + _mm(gate_lo, w_hi) + b_ref[...]
        la_ref[d] = _log_sigmoid(z) * (1.0 / GLA_TAU)

    trow = lax.broadcasted_iota(jnp.int32, (tile, tile), 0)
    tcol = lax.broadcasted_iota(jnp.int32, (tile, tile), 1)
    same_chunk = (trow // C) == (tcol // C)
    keep = (same_chunk & (trow >= tcol), same_chunk & (tcol >= trow))
    rmod = lax.broadcasted_iota(jnp.int32, (tile, LANES), 0) % C
    lane = lax.broadcasted_iota(jnp.int32, (1, LANES), 1)
    head_lane = (lane < GLA_DK, lane >= GLA_DK)
    srow = lax.broadcasted_iota(jnp.int32, (2 * GLA_DV, LANES), 0)
    scol = lax.broadcasted_iota(jnp.int32, (2 * GLA_DV, LANES), 1)
    diag = (srow < GLA_DV) == (scol < GLA_DK)
    scale = GLA_DK ** -0.5

    def chunk_scan(x, d):
        step = 1
        while step < C:
            if d == 0:
                x = x + jnp.where(rmod >= step, pltpu.roll(x, step, 0), 0.0)
            else:
                x = x + jnp.where(rmod < C - step, pltpu.roll(x, tile - step, 0), 0.0)
            step *= 2
        return x

    def intra(j, _):
        r0 = pl.multiple_of(j * tile, tile)
        rows = pl.ds(r0, tile)
        q = q_ref[rows, :].astype(F32) * scale
        k = k_ref[rows, :].astype(F32)
        v = v_ref[rows, :]
        vf = v.astype(F32)
        for c in range(cpt):
            vt_ref[j * cpt + c] = vf[c * C:(c + 1) * C, :].T.astype(BF16)
        for d, out_ref in enumerate((of_ref, ob_ref)):
            la = la_ref[d, rows, :]
            b = chunk_scan(la, d)
            tots = [jnp.sum(la[c * C:(c + 1) * C, :], axis=0, keepdims=True) for c in range(cpt)]
            for c in range(cpt):
                tot_ref[d, pl.ds(j * cpt + c, 1), :] = tots[c]
            tot = jnp.concatenate([jnp.broadcast_to(t, (C, LANES)) for t in tots], axis=0)
            qd = (q * jnp.exp(b)).astype(BF16)
            kd = (k * jnp.exp(-b)).astype(BF16)
            qd_ref[d, rows, :] = qd
            ke_ref[d, rows, :] = (k * jnp.exp(tot - b)).astype(BF16)
            parts = []
            for h in range(2):
                qh = jnp.where(head_lane[h], qd, jnp.zeros_like(qd))
                s = jnp.where(keep[d], _mm_nt(qh, kd), 0.0)
                parts.append(_mm(s.astype(BF16), v[:, h * GLA_DV:(h + 1) * GLA_DV]))
            out_ref[rows, :] = jnp.concatenate(parts, axis=1)
        return 0

    lax.fori_loop(0, seq // tile, intra, 0)

    def inter(i, carry):
        new = []
        for d, (state, out_ref) in enumerate(zip(carry, (of_ref, ob_ref))):
            c = i if d == 0 else n_chunks - 1 - i
            rows = pl.ds(pl.multiple_of(c * C, C), C)
            out_ref[rows, :] += _mm_nt(qd_ref[d, rows, :], state.astype(BF16))
            upd = jnp.where(diag, _mm(vt_ref[c], ke_ref[d, rows, :]), 0.0)
            new.append(jnp.exp(tot_ref[d, pl.ds(c, 1), :]) * state + upd)
        return tuple(new)

    zero = jnp.zeros((2 * GLA_DV, LANES), F32)
    lax.fori_loop(0, n_chunks, inter, (zero, zero), unroll=GLA_UNROLL)

    blk = 256

    def finish(j, _):
        rows = pl.ds(pl.multiple_of(j * blk, blk), blk)
        o = of_ref[rows, :] + ob_ref[rows, :]
        g = ng_ref[...]
        outs = []
        for h in range(2):
            oh = o[:, h * GLA_DV:(h + 1) * GLA_DV]
            ms = jnp.mean(oh * oh, axis=-1, keepdims=True)
            outs.append(oh * lax.rsqrt(ms + RMS_EPS) * g[:, h * GLA_DV:(h + 1) * GLA_DV])
        r = r_ref[rows, :].astype(F32)
        o_ref[rows, :] = (jnp.concatenate(outs, axis=1) * (r * jax.nn.sigmoid(r))).astype(o_ref.dtype)
        return 0

    lax.fori_loop(0, seq // blk, finish, 0)


def _gla(h_a, gate, wf, bf, wb, bb, norm_g, B, S):
    pairs = GLA_HEADS // 2
    kq, kv = 2 * GLA_DK, 2 * GLA_DV
    sq = pl.BlockSpec((None, S, kq), lambda b, g: (b, 0, g))
    sk = pl.BlockSpec((None, S, kq), lambda b, g: (b, 0, pairs + g))
    sv = pl.BlockSpec((None, S, kv), lambda b, g: (b, 0, (2 * GLA_QK) // kv + g))
    sr = pl.BlockSpec((None, S, kv), lambda b, g: (b, 0, (2 * GLA_QK + GLA_VW) // kv + g))
    sg = pl.BlockSpec((None, S, LANES), lambda b, g: (b, 0, 0))
    sw = pl.BlockSpec((LANES, kq), lambda b, g: (0, g))
    sb = pl.BlockSpec((1, kq), lambda b, g: (0, g))
    sn = pl.BlockSpec((1, kv), lambda b, g: (0, g))
    return pl.pallas_call(
        functools.partial(_gla_kernel, seq=S),
        grid=(B, pairs),
        in_specs=[sq, sk, sv, sr, sg, sw, sb, sw, sb, sn],
        out_specs=pl.BlockSpec((None, S, kv), lambda b, g: (b, 0, g)),
        out_shape=jax.ShapeDtypeStruct((B, S, GLA_VW), BF16),
        scratch_shapes=[pltpu.VMEM((2, S, kq), F32),
                        pltpu.VMEM((2, S, kq), BF16),
                        pltpu.VMEM((2, S, kq), BF16),
                        pltpu.VMEM((S // GLA_CHUNK, kv, GLA_CHUNK), BF16),
                        pltpu.VMEM((2, S // GLA_CHUNK, kq), F32),
                        pltpu.VMEM((S, kv), F32), pltpu.VMEM((S, kv), F32)],
        compiler_params=_params("parallel", "parallel"),
        name="gla_mixer",
    )(h_a, h_a, h_a, h_a, gate, wf, bf, wb, bb, norm_g)


def _dil_kernel(q_ref, k_ref, v_ref, o_ref, m_ref, l_ref, acc_ref, bias_ref, *, seq):
    R = DIL_RADIUS
    pair = pl.program_id(1)
    lane = lax.broadcasted_iota(jnp.int32, (1, LANES), 1)
    first_head = lane < DIL_DH
    scale = DIL_DH ** -0.5

    for p_idx, (window, dil) in enumerate(DIL_PATTERNS):
        assert window // (2 * dil) == R
        length = seq // dil
        tq = min(DIL_TQ, length)
        nk = tq + 2 * R
        tpr = length // tq
        shift = tpr.bit_length() - 1
        assert tpr == 1 << shift and tq % R == 0
        qi = lax.broadcasted_iota(jnp.int32, (tq, nk), 0)
        kj = lax.broadcasted_iota(jnp.int32, (tq, nk), 1)
        dist = jnp.abs(kj - R - qi)
        for h in range(2):
            slope = 1.0 / jnp.left_shift(jnp.ones((tq, nk), jnp.int32), 2 * pair + h + 1).astype(F32)
            bias_ref[h, :tq, :nk] = jnp.where(dist <= R, -(dil * dist).astype(F32) * slope, MASK_VALUE)
        ones = jnp.ones((nk, LANES), BF16)

        def rows_at(start, n, dil=dil):
            return pl.ds(start, n, stride=dil) if dil > 1 else pl.ds(pl.multiple_of(start, R), n)

        def body(idx, _, dil=dil, tq=tq, nk=nk, tpr=tpr, shift=shift, p_idx=p_idx, rows_at=rows_at,
                 kj=kj, ones=ones):
            res = lax.shift_right_logical(idx, shift)
            t = jnp.bitwise_and(idx, tpr - 1)
            start = res + dil * tq * t
            has_prev = t > 0
            has_next = t < tpr - 1
            prev = jnp.where(has_prev, start - dil * R, start)
            nxt = jnp.where(has_next, start + dil * tq, start)
            q = (q_ref[rows_at(start, tq), :] * scale).astype(BF16)
            kcat = jnp.concatenate([k_ref[rows_at(prev, R), :], k_ref[rows_at(start, tq), :],
                                    k_ref[rows_at(nxt, R), :]], axis=0).astype(BF16)
            vcat = jnp.concatenate([v_ref[rows_at(prev, R), :], v_ref[rows_at(start, tq), :],
                                    v_ref[rows_at(nxt, R), :]], axis=0).astype(BF16)
            vones = jnp.concatenate([vcat, ones], axis=1)
            in_seq = ((kj >= R) | has_prev) & ((kj < R + tq) | has_next)
            ms, pvs = [], []
            for h in range(2):
                qh = jnp.where(first_head if h == 0 else ~first_head, q, jnp.zeros_like(q))
                s = jnp.where(in_seq, _mm_nt(qh, kcat) + bias_ref[h, :tq, :nk], MASK_VALUE)
                m = jnp.max(s, axis=-1, keepdims=True)
                ms.append(m)
                pvs.append(_mm(jnp.exp(s - m).astype(BF16), vones))
            out_rows = rows_at(start, tq)
            m_ref[p_idx, out_rows, :] = jnp.where(first_head, ms[0], ms[1])
            l_ref[p_idx, out_rows, :] = jnp.where(first_head, pvs[0][:, LANES:], pvs[1][:, LANES:])
            acc_ref[p_idx, out_rows, :] = jnp.where(first_head, pvs[0][:, :LANES], pvs[1][:, :LANES])
            return 0

        lax.fori_loop(0, seq // tq, body, 0, unroll=DIL_UNROLL)

    blk = 256
    n_pat = len(DIL_PATTERNS)

    def finish(j, _):
        rows = pl.ds(pl.multiple_of(j * blk, blk), blk)
        ms = [m_ref[p, rows, :] for p in range(n_pat)]
        m = functools.reduce(jnp.maximum, ms)
        ws = [jnp.exp(x - m) for x in ms]
        l = functools.reduce(lambda a, b: a + b, [w * l_ref[p, rows, :] for p, w in enumerate(ws)])
        acc = functools.reduce(lambda a, b: a + b, [w * acc_ref[p, rows, :] for p, w in enumerate(ws)])
        o_ref[rows, :] = (acc / l).astype(o_ref.dtype)
        return 0

    lax.fori_loop(0, seq // blk, finish, 0)


def _dilated(h_b, B, S):
    pairs = DIL_HEADS // 2
    sq = pl.BlockSpec((None, S, LANES), lambda b, g: (b, 0, g))
    sk = pl.BlockSpec((None, S, LANES), lambda b, g: (b, 0, pairs + g))
    sv = pl.BlockSpec((None, S, LANES), lambda b, g: (b, 0, 2 * pairs + g))
    return pl.pallas_call(
        functools.partial(_dil_kernel, seq=S),
        grid=(B, pairs),
        in_specs=[sq, sk, sv],
        out_specs=pl.BlockSpec((None, S, LANES), lambda b, g: (b, 0, g)),
        out_shape=jax.ShapeDtypeStruct((B, S, DIL_W), BF16),
        scratch_shapes=[pltpu.VMEM((len(DIL_PATTERNS), S, LANES), F32)] * 3
                       + [pltpu.VMEM((2, DIL_TQ, DIL_TQ + 2 * DIL_RADIUS), F32)],
        compiler_params=_params("parallel", "parallel"),
        name="dilated_mixer",
    )(h_b, h_b, h_b)


def _route(lt):
    g = [lt[i:i + 1, :] for i in range(MOE_GROUPS)]
    gmax = functools.reduce(jnp.maximum, g)
    gexp = [jnp.exp(x - gmax) for x in g]
    gsum = functools.reduce(lambda a, b: a + b, gexp)
    gprob = [x / gsum for x in gexp]
    g_val, g_idx = gprob[0], jnp.zeros_like(gprob[0], dtype=jnp.int32)
    for i in range(1, MOE_GROUPS):
        better = gprob[i] > g_val
        g_idx = jnp.where(better, i, g_idx)
        g_val = jnp.where(better, gprob[i], g_val)
    e = []
    for j in range(MOE_EPG):
        x = lt[MOE_GROUPS + j:MOE_GROUPS + j + 1, :]
        for gi in range(1, MOE_GROUPS):
            r0 = MOE_GROUPS + gi * MOE_EPG + j
            x = jnp.where(g_idx == gi, lt[r0:r0 + 1, :], x)
        e.append(x)
    emax = functools.reduce(jnp.maximum, e)
    eexp = [jnp.exp(x - emax) for x in e]
    esum = functools.reduce(lambda a, b: a + b, eexp)
    eprob = [x / esum for x in eexp]
    v1, i1 = eprob[0], jnp.zeros_like(g_idx)
    for j in range(1, MOE_EPG):
        better = eprob[j] > v1
        i1 = jnp.where(better, j, i1)
        v1 = jnp.where(better, eprob[j], v1)
    v2, i2 = jnp.full_like(v1, -1.0), jnp.zeros_like(g_idx)
    for j in range(MOE_EPG):
        cand = jnp.where(i1 == j, -1.0, eprob[j])
        better = cand > v2
        i2 = jnp.where(better, j, i2)
        v2 = jnp.where(better, cand, v2)
    den = v1 + v2
    ids = jnp.concatenate([g_idx * MOE_EPG + i1, g_idx * MOE_EPG + i2], axis=0)
    wts = jnp.concatenate([g_val * (v1 / den), g_val * (v2 / den)], axis=0)
    return ids, wts


def _proj_ln_route_kernel(*refs, n_in):
    x_ref = refs[0]
    o_refs = refs[1:1 + n_in]
    w_refs = refs[1 + n_in:1 + 2 * n_in]
    g_ref, b_ref, wr_ref, br_ref, x1_ref, x1b_ref, ids_ref, wts_ref = refs[1 + 2 * n_in:]
    m = _mm(o_refs[0][...], w_refs[0][...])
    for o_ref, w_ref in zip(o_refs[1:], w_refs[1:]):
        m = m + _mm(o_ref[...], w_ref[...])
    x1 = _layer_norm(ALPHA * x_ref[...] + m, g_ref[...], b_ref[...])
    x1_ref[...] = x1
    x1b_ref[...] = x1.astype(BF16)
    lt = lax.dot_general(wr_ref[...], x1, (((1,), (1,)), ((), ())), precision=HIGHEST,
                         preferred_element_type=F32) + br_ref[...]
    ids, wts = _route(lt)
    ids_ref[...] = ids
    wts_ref[...] = wts


def _proj_ln_route(x2d, outs, ws, ln_g, ln_b, wr, br):
    T = x2d.shape[0]
    tm = ROW_TILE
    n_in = len(outs)
    row = lambda w: pl.BlockSpec((tm, w), lambda i: (i, 0))
    tok = pl.BlockSpec((2, tm), lambda i: (0, i))
    return pl.pallas_call(
        functools.partial(_proj_ln_route_kernel, n_in=n_in),
        grid=(T // tm,),
        in_specs=[row(D_MODEL)] + [row(o.shape[1]) for o in outs] + [_full(w.shape) for w in ws]
                 + [_full(ln_g.shape), _full(ln_b.shape), _full(wr.shape), _full(br.shape)],
        out_specs=[row(D_MODEL), row(D_MODEL), tok, tok],
        out_shape=[jax.ShapeDtypeStruct((T, D_MODEL), F32), jax.ShapeDtypeStruct((T, D_MODEL), BF16),
                   jax.ShapeDtypeStruct((2, T), jnp.int32), jax.ShapeDtypeStruct((2, T), F32)],
        compiler_params=_params("parallel"),
        name="proj_ln_route",
    )(x2d, *outs, *ws, ln_g, ln_b, wr, br)


def _moe_kernel(te_ref, nu_ref, xs_ref, w13_ref, w2_ref, rw_ref, ys_ref):
    i = pl.program_id(0)

    @pl.when(i < nu_ref[0])
    def _():
        h = _mm(xs_ref[...], w13_ref[...])
        h1, h3 = h[:, :MOE_FF], h[:, MOE_FF:]
        hidden = (h1 * jax.nn.sigmoid(h1)) * h3
        y = _mm(hidden.astype(BF16), w2_ref[...])
        ys_ref[...] = (y * rw_ref[...]).astype(ys_ref.dtype)

    @pl.when(i >= nu_ref[0])
    def _():
        ys_ref[...] = jnp.zeros_like(ys_ref)


def _moe_experts(xs, w13, w2, row_w, tile_expert, n_used):
    P = xs.shape[0]
    tm = MOE_TILE
    grid_spec = pltpu.PrefetchScalarGridSpec(
        num_scalar_prefetch=2,
        grid=(P // tm,),
        in_specs=[pl.BlockSpec((tm, D_MODEL), lambda i, te, nu: (i, 0)),
                  pl.BlockSpec((None, D_MODEL, 2 * MOE_FF), lambda i, te, nu: (te[i], 0, 0)),
                  pl.BlockSpec((None, MOE_FF, D_MODEL), lambda i, te, nu: (te[i], 0, 0)),
                  pl.BlockSpec((tm, 1), lambda i, te, nu: (i, 0))],
        out_specs=pl.BlockSpec((tm, D_MODEL), lambda i, te, nu: (i, 0)),
    )
    return pl.pallas_call(
        _moe_kernel,
        grid_spec=grid_spec,
        out_shape=jax.ShapeDtypeStruct((P, D_MODEL), BF16),
        compiler_params=_params("arbitrary"),
        name="moe_experts",
    )(tile_expert, n_used, xs, w13, w2, row_w)


def _moe_plan(ids, wts, T):
    tm = MOE_TILE
    n_assign = 2 * T
    e_flat = ids.reshape(n_assign)
    onehot = (e_flat[:, None] == jnp.arange(MOE_EXPERTS, dtype=jnp.int32)[None, :]).astype(jnp.int32)
    csum = jnp.cumsum(onehot, axis=0)
    counts = csum[-1]
    padded = ((counts + tm - 1) // tm) * tm
    ends = jnp.cumsum(padded)
    offs = ends - padded
    pos = jnp.sum(onehot * (csum - onehot + offs[None, :]), axis=1)
    P = n_assign + MOE_EXPERTS * tm
    n_tiles = P // tm
    tile_start = jnp.arange(n_tiles, dtype=jnp.int32) * tm
    tile_expert = jnp.minimum(jnp.sum((tile_start[:, None] >= ends[None, :]).astype(jnp.int32), axis=1),
                              MOE_EXPERTS - 1).astype(jnp.int32)
    n_used = (ends[-1:] // tm).astype(jnp.int32)
    tok = jnp.arange(n_assign, dtype=jnp.int32) % T
    _, tok_sorted, w_sorted = lax.sort((e_flat, tok, wts.reshape(n_assign)), num_keys=1, is_stable=True)
    t_onehot = (tile_expert[:, None] == jnp.arange(MOE_EXPERTS, dtype=jnp.int32)[None, :]).astype(jnp.int32)
    t_rank0 = tile_start - jnp.sum(t_onehot * offs[None, :], axis=1)
    t_count = jnp.sum(t_onehot * counts[None, :], axis=1)
    t_first = jnp.sum(t_onehot * (jnp.cumsum(counts) - counts)[None, :], axis=1)
    r = t_rank0[:, None] + jnp.arange(tm, dtype=jnp.int32)[None, :]
    valid = (r < t_count[:, None]).reshape(P)
    u = jnp.clip(t_first[:, None] + r, 0, n_assign - 1).reshape(P)
    src_tok = jnp.where(valid, tok_sorted.at[u].get(mode="promise_in_bounds"), 0)
    row_w = jnp.where(valid, w_sorted.at[u].get(mode="promise_in_bounds"), 0.0)
    return src_tok, row_w.reshape(P, 1), tile_expert, n_used, pos


def _combine_ln_kernel(x_ref, y0_ref, y1_ref, g_ref, b_ref, o_ref):
    y = ALPHA * x_ref[...] + y0_ref[...].astype(F32) + y1_ref[...].astype(F32)
    o_ref[...] = _layer_norm(y, g_ref[...], b_ref[...])


def _combine_ln(x2d, y0, y1, ln_g, ln_b):
    T = x2d.shape[0]
    tm = ROW_TILE
    row = pl.BlockSpec((tm, D_MODEL), lambda i: (i, 0))
    return pl.pallas_call(
        _combine_ln_kernel,
        grid=(T // tm,),
        in_specs=[row, row, row, _full(ln_g.shape), _full(ln_b.shape)],
        out_specs=row,
        out_shape=jax.ShapeDtypeStruct((T, D_MODEL), F32),
        compiler_params=_params("parallel"),
        name="combine_ln",
    )(x2d, y0, y1, ln_g, ln_b)


def _hier_moe_ln(x1, x1b, ids, wts, w13, w2, ln_g, ln_b):
    T = x1.shape[0]
    src_tok, row_w, tile_expert, n_used, pos = _moe_plan(ids, wts, T)
    xs = x1b.at[src_tok].get(mode="promise_in_bounds")
    ys = _moe_experts(xs, w13, w2, row_w, tile_expert, n_used)
    y0 = ys.at[pos[:T]].get(mode="promise_in_bounds")
    y1 = ys.at[pos[T:]].get(mode="promise_in_bounds")
    return _combine_ln(x1, y0, y1, ln_g, ln_b)


def _rope_block(x, c, sa, sb):
    return x * c + pltpu.roll(x, LANES - 32, 1) * sa + pltpu.roll(x, 32, 1) * sb


def _mla_in_kernel(x_ref, win_ref, qg_ref, kvg_ref, wq_ref, wk_ref, wv_ref, c_ref, sa_ref, sb_ref,
                   q_ref, k_ref, v_ref):
    x = x_ref[...].astype(BF16)
    h = _mm(x, win_ref[...])
    c_q = h[:, :MLA_Q_LORA]
    c_kv = h[:, MLA_Q_LORA:MLA_Q_LORA + MLA_KV_LORA]
    k_rope = h[:, MLA_Q_LORA + MLA_KV_LORA:]
    cqn = c_q * lax.rsqrt(jnp.mean(c_q * c_q, axis=-1, keepdims=True) + RMS_EPS) * qg_ref[...]
    ckn = c_kv * lax.rsqrt(jnp.mean(c_kv * c_kv, axis=-1, keepdims=True) + RMS_EPS) * kvg_ref[...]
    cqn, ckn = cqn.astype(BF16), ckn.astype(BF16)
    scale = (MLA_NOPE + MLA_ROPE) ** -0.5 * LOG2_E
    q = _mm(cqn, wq_ref[...]) * scale
    kn = _mm(ckn, wk_ref[...])
    v = _mm(ckn, wv_ref[...]).astype(v_ref.dtype)
    c, sa, sb = c_ref[...], sa_ref[...], sb_ref[...]
    kr = _rope_block(k_rope, c, sa, sb).astype(k_ref.dtype)
    ones = jnp.ones((x.shape[0], LANES), v_ref.dtype)
    for g in range(MLA_HEADS // 2):
        lo = 2 * LANES * g
        v_ref[:, lo:lo + LANES] = v[:, LANES * g:LANES * (g + 1)]
        v_ref[:, lo + LANES:lo + 2 * LANES] = ones
        q_ref[:, lo:lo + LANES] = q[:, lo:lo + LANES].astype(q_ref.dtype)
        q_ref[:, lo + LANES:lo + 2 * LANES] = _rope_block(q[:, lo + LANES:lo + 2 * LANES], c, sa, sb
                                                          ).astype(q_ref.dtype)
        k_ref[:, lo:lo + LANES] = kn[:, LANES * g:LANES * (g + 1)].astype(k_ref.dtype)
        k_ref[:, lo + LANES:lo + 2 * LANES] = kr


def _mla_in(x2d, win, qg, kvg, wq, wk, wv, c, sa, sb, S):
    T = x2d.shape[0]
    tm = ROW_TILE
    row = lambda w: pl.BlockSpec((tm, w), lambda i: (i, 0))
    n_s = S // tm
    tab = pl.BlockSpec((tm, LANES), lambda i: (i % n_s, 0))
    pairs = MLA_HEADS // 2
    return pl.pallas_call(
        _mla_in_kernel,
        grid=(T // tm,),
        in_specs=[row(D_MODEL), _full(win.shape), _full(qg.shape), _full(kvg.shape), _full(wq.shape),
                  _full(wk.shape), _full(wv.shape), tab, tab, tab],
        out_specs=[row(pairs * 2 * LANES)] * 3,
        out_shape=[jax.ShapeDtypeStruct((T, pairs * 2 * LANES), BF16)] * 3,
        compiler_params=_params("parallel"),
        name="mla_in_proj",
    )(x2d, win, qg, kvg, wq, wk, wv, c, sa, sb)


def _mla_attn_kernel(q_ref, k_ref, v_ref, o_ref, sa_ref, sb_ref, *, seq):
    lane2 = lax.broadcasted_iota(jnp.int32, (1, 2 * LANES), 1)
    rl = lane2 - LANES
    in_a = (lane2 < MLA_NOPE) | ((rl >= 0) & (rl < 16)) | ((rl >= 32) & (rl < 48))
    in_b = ((lane2 >= MLA_NOPE) & (lane2 < LANES)) | ((rl >= 16) & (rl < 32)) | ((rl >= 48) & (rl < 64))
    lane = lax.broadcasted_iota(jnp.int32, (1, LANES), 1)
    q = q_ref[...]
    tq = q.shape[0]
    tk = min(ATT_TK, seq)
    n_k = seq // tk
    q_heads = (jnp.where(in_a, q, jnp.zeros_like(q)), jnp.where(in_b, q, jnp.zeros_like(q)))
    s_refs = (sa_ref, sb_ref)

    def score_chunk(h, j, m):
        s = _mm_nt(q_heads[h], k_ref[j * tk:(j + 1) * tk, :])
        s_refs[h][:, j * tk:(j + 1) * tk] = s
        for c in range(tk // LANES):
            m = jnp.maximum(m, s[:, c * LANES:(c + 1) * LANES])
        return m

    def value_chunk(h, j, m_row, acc):
        p = jnp.exp2(s_refs[h][:, j * tk:(j + 1) * tk] - m_row).astype(BF16)
        return acc + _mm(p, v_ref[j * tk:(j + 1) * tk, :])

    neg = jnp.full((tq, LANES), -jnp.inf, F32)
    m = neg
    for j in range(n_k):
        m = score_chunk(0, j, m)
    m_a = jnp.max(m, axis=-1, keepdims=True)
    acc_a = jnp.zeros((tq, 2 * LANES), F32)
    m = neg
    for j in range(n_k):
        acc_a = value_chunk(0, j, m_a, acc_a)
        m = score_chunk(1, j, m)
    m_b = jnp.max(m, axis=-1, keepdims=True)
    acc_b = jnp.zeros((tq, 2 * LANES), F32)
    for j in range(n_k):
        acc_b = value_chunk(1, j, m_b, acc_b)
    out_a = acc_a[:, :LANES] / acc_a[:, LANES:]
    out_b = acc_b[:, :LANES] / acc_b[:, LANES:]
    o_ref[...] = jnp.where(lane < MLA_V, out_a, out_b).astype(o_ref.dtype)


def _mla_attn(q, k, v, B, S):
    pairs = MLA_HEADS // 2
    tq = min(ATT_TQ, S)
    return pl.pallas_call(
        functools.partial(_mla_attn_kernel, seq=S),
        grid=(B, pairs, S // tq),
        in_specs=[pl.BlockSpec((None, tq, 2 * LANES), lambda b, g, i: (b, i, g)),
                  pl.BlockSpec((None, S, 2 * LANES), lambda b, g, i: (b, 0, g)),
                  pl.BlockSpec((None, S, 2 * LANES), lambda b, g, i: (b, 0, g))],
        out_specs=pl.BlockSpec((None, tq, LANES), lambda b, g, i: (b, i, g)),
        out_shape=jax.ShapeDtypeStruct((B, S, pairs * LANES), BF16),
        scratch_shapes=[pltpu.VMEM((tq, S), F32), pltpu.VMEM((tq, S), F32)],
        compiler_params=_params("parallel", "parallel", "arbitrary"),
        name="mla_attention",
    )(q, k, v)


def _prep_even(w_in, wa_f, ba_f, wa_b, ba_b, norm_g, w_out):
    o_q, o_k, o_v, o_r = 0, GLA_QK, 2 * GLA_QK, 2 * GLA_QK + GLA_VW
    o_af = o_r + GLA_VW
    o_ab = o_af + GLA_RANK
    o_qb = o_ab + GLA_RANK
    wa = w_in[:, :o_af].astype(BF16)
    wb = w_in[:, o_qb:].astype(BF16)
    wg = jnp.zeros((D_MODEL, LANES), F32).at[:, :2 * GLA_RANK].set(w_in[:, o_af:o_qb]).astype(BF16)
    wf = jnp.zeros((LANES, GLA_QK), F32).at[:GLA_RANK].set(wa_f)
    wb_gate = jnp.zeros((LANES, GLA_QK), F32).at[GLA_RANK:2 * GLA_RANK].set(wa_b)
    return dict(wa=wa, wb=wb, wg=wg, wf=wf, bf=ba_f.reshape(1, -1), wb_gate=wb_gate,
                bb=ba_b.reshape(1, -1), norm_g=norm_g.reshape(1, -1),
                wo_a=w_out[:GLA_VW].astype(BF16), wo_b=w_out[GLA_VW:].astype(BF16))


def _prep_odd(w_in, q_norm, kv_norm, w_uq, w_ukv, w_out, S):
    half = MLA_ROPE // 2
    pairs = MLA_HEADS // 2
    kr = w_in[:, MLA_Q_LORA + MLA_KV_LORA:]
    kr_rep = jnp.concatenate([kr[:, :half], kr[:, :half], kr[:, half:], kr[:, half:],
                              jnp.zeros((D_MODEL, LANES - 2 * MLA_ROPE), F32)], axis=1)
    win = jnp.concatenate([w_in[:, :MLA_Q_LORA + MLA_KV_LORA], kr_rep], axis=1).astype(BF16)
    uq = w_uq.reshape(MLA_Q_LORA, MLA_HEADS, MLA_NOPE + MLA_ROPE)
    nope = uq[:, :, :MLA_NOPE].reshape(MLA_Q_LORA, pairs, 2 * MLA_NOPE)
    r1 = uq[:, :, MLA_NOPE:MLA_NOPE + half].reshape(MLA_Q_LORA, pairs, 2 * half)
    r2 = uq[:, :, MLA_NOPE + half:].reshape(MLA_Q_LORA, pairs, 2 * half)
    pad = jnp.zeros((MLA_Q_LORA, pairs, LANES - 2 * MLA_ROPE), F32)
    wq = jnp.concatenate([nope, r1, r2, pad], axis=2).reshape(MLA_Q_LORA, pairs * 2 * LANES).astype(BF16)
    ukv = w_ukv.reshape(MLA_KV_LORA, MLA_HEADS, MLA_NOPE + MLA_V)
    wk = ukv[:, :, :MLA_NOPE].reshape(MLA_KV_LORA, MLA_HEADS * MLA_NOPE).astype(BF16)
    wv = ukv[:, :, MLA_NOPE:].reshape(MLA_KV_LORA, MLA_HEADS * MLA_V).astype(BF16)
    inv = 1.0 / (ROPE_BASE ** (jnp.arange(0, MLA_ROPE, 2, dtype=F32) / MLA_ROPE))
    ang = jnp.arange(S, dtype=F32)[:, None] * inv[None, :]
    cos, sin = jnp.cos(ang), jnp.sin(ang)
    z16 = jnp.zeros((S, half), F32)
    z64 = jnp.zeros((S, LANES - 2 * MLA_ROPE), F32)
    c = jnp.concatenate([cos, cos, cos, cos, z64], axis=1)
    sa = jnp.concatenate([-sin, -sin, z16, z16, z64], axis=1)
    sb = jnp.concatenate([z16, z16, sin, sin, z64], axis=1)
    return dict(win=win, qg=q_norm.reshape(1, -1), kvg=kv_norm.reshape(1, -1), wq=wq, wk=wk, wv=wv,
                c=c, sa=sa, sb=sb, wo=w_out.astype(BF16))


def _prep_moe(wg, bg, we, be, w1, w3, w2):
    wr = jnp.zeros((32, D_MODEL), F32).at[:MOE_GROUPS].set(wg.T).at[MOE_GROUPS:MOE_GROUPS + MOE_EXPERTS].set(we.T)
    br = jnp.zeros((32, 1), F32).at[:MOE_GROUPS, 0].set(bg).at[MOE_GROUPS:MOE_GROUPS + MOE_EXPERTS, 0].set(be)
    w13 = jnp.concatenate([w1, w3], axis=2).astype(BF16)
    return dict(wr=wr, br=br, w13=w13, w2=w2.astype(BF16))


def _trunk(x, ev, od, moe, ln1_g, ln1_b, ln2_g, ln2_b):
    B, S, D = x.shape
    T = B * S
    x2d = x.reshape(T, D)
    for i in range(DEPTH):
        if i % 2 == 0:
            p = ev[i // 2]
            h_a, h_b, gate = _even_in(x2d, p["wa"], p["wb"], p["wg"])
            o_a = _gla(h_a.reshape(B, S, -1), gate.reshape(B, S, -1), p["wf"], p["bf"], p["wb_gate"],
                       p["bb"], p["norm_g"], B, S)
            o_b = _dilated(h_b.reshape(B, S, -1), B, S)
            outs = [o_a.reshape(T, -1), o_b.reshape(T, -1)]
            ws = [p["wo_a"], p["wo_b"]]
        else:
            p = od[i // 2]
            q, k, v = _mla_in(x2d, p["win"], p["qg"], p["kvg"], p["wq"], p["wk"], p["wv"],
                              p["c"], p["sa"], p["sb"], S)
            o = _mla_attn(q.reshape(B, S, -1), k.reshape(B, S, -1), v.reshape(B, S, -1), B, S)
            outs = [o.reshape(T, -1)]
            ws = [p["wo"]]
        m = moe[i]
        x1, x1b, ids, wts = _proj_ln_route(x2d, outs, ws, ln1_g[i:i + 1], ln1_b[i:i + 1], m["wr"], m["br"])
        x2d = _hier_moe_ln(x1, x1b, ids, wts, m["w13"], m["w2"], ln2_g[i:i + 1], ln2_b[i:i + 1])
    return x2d.reshape(B, S, D)


def kernel(x_prompt, x_sample, ev_w_in, ev_wa_f, ev_ba_f, ev_wa_b, ev_ba_b, ev_gla_norm, ev_w_out,
           od_w_in, od_q_norm, od_kv_norm, od_w_uq, od_w_ukv, od_w_out, ln1_g, ln1_b, ln2_g, ln2_b,
           moe_wg, moe_bg, moe_we, moe_be, moe_w1, moe_w3, moe_w2):
    S = x_prompt.shape[1]
    ev = [_prep_even(ev_w_in[j], ev_wa_f[j], ev_ba_f[j], ev_wa_b[j], ev_ba_b[j], ev_gla_norm[j], ev_w_out[j])
          for j in range(ev_w_in.shape[0])]
    od = [_prep_odd(od_w_in[j], od_q_norm[j], od_kv_norm[j], od_w_uq[j], od_w_ukv[j], od_w_out[j], S)
          for j in range(od_w_in.shape[0])]
    moe = [_prep_moe(moe_wg[i], moe_bg[i], moe_we[i], moe_be[i], moe_w1[i], moe_w3[i], moe_w2[i])
           for i in range(DEPTH)]
    return tuple(_trunk(x, ev, od, moe, ln1_g, ln1_b, ln2_g, ln2_b) for x in (x_prompt, x_sample))
```

```python
import functools

import numpy as np
import jax
import jax.numpy as jnp
from jax import lax
from jax.experimental import pallas as pl
from jax.experimental.pallas import tpu as pltpu

F32 = jnp.float32
BF16 = jnp.bfloat16
HIGHEST = lax.Precision.HIGHEST

D_MODEL = 1024
DEPTH = 2
GLA_HEADS, GLA_DK, GLA_DV = 4, 64, 128
GLA_QK, GLA_VW = GLA_HEADS * GLA_DK, GLA_HEADS * GLA_DV
GLA_RANK, GLA_TAU, GLA_CHUNK = 16, 16.0, 64
DIL_HEADS, DIL_DH = 8, 64
DIL_W = DIL_HEADS * DIL_DH
DIL_PATTERNS = ((128, 1), (512, 4), (2048, 16))
DIL_RADIUS = 64
MASK_VALUE = -1e30
MLA_HEADS, MLA_NOPE, MLA_ROPE, MLA_V = 16, 64, 32, 64
MLA_Q_LORA, MLA_KV_LORA = 384, 128
ROPE_BASE = 10000.0
MOE_GROUPS, MOE_EPG, MOE_EXPERTS, MOE_FF = 4, 4, 16, 512
ALPHA = (2 * DEPTH) ** 0.25
LN_EPS = 1e-5
RMS_EPS = 1e-6
LOG2_E = 1.4426950408889634

LANES = 128
VMEM_LIMIT = 56 * 1024 * 1024
ROW_TILE = 512
MOE_TILE = 512
ATT_TQ = 512
ATT_TK = 512
DIL_UNROLL = 4
GLA_UNROLL = 2
GLA_TILE = 256
DIL_TQ = 256


def _params(*sem):
    return pltpu.CompilerParams(dimension_semantics=sem, vmem_limit_bytes=VMEM_LIMIT)


def _full(shape):
    n = len(shape)
    return pl.BlockSpec(shape, lambda *_: (0,) * n)


def _mm(a, b):
    return jnp.dot(a, b, preferred_element_type=F32)


def _mm_nt(a, b):
    return lax.dot_general(a, b, (((1,), (1,)), ((), ())), preferred_element_type=F32)


def _mm_tn(a, b):
    return lax.dot_general(a, b, (((0,), (0,)), ((), ())), preferred_element_type=F32)


def _layer_norm(y, g, b):
    mu = jnp.mean(y, axis=-1, keepdims=True)
    yc = y - mu
    var = jnp.mean(yc * yc, axis=-1, keepdims=True)
    return yc * lax.rsqrt(var + LN_EPS) * g + b


def _log_sigmoid(z):
    return jnp.minimum(z, 0.0) - jnp.log(1.0 + jnp.exp(-jnp.abs(z)))


def _even_in_kernel(x_ref, wa_ref, wb_ref, wg_ref, oa_ref, ob_ref, og_ref):
    x = x_ref[...].astype(BF16)
    oa_ref[...] = _mm(x, wa_ref[...]).astype(oa_ref.dtype)
    ob_ref[...] = _mm(x, wb_ref[...])
    og_ref[...] = _mm(x, wg_ref[...])


def _even_in(x2d, wa, wb, wg):
    T = x2d.shape[0]
    tm = ROW_TILE
    return pl.pallas_call(
        _even_in_kernel,
        grid=(T // tm,),
        in_specs=[pl.BlockSpec((tm, D_MODEL), lambda i: (i, 0)),
                  _full(wa.shape), _full(wb.shape), _full(wg.shape)],
        out_specs=[pl.BlockSpec((tm, wa.shape[1]), lambda i: (i, 0)),
                   pl.BlockSpec((tm, wb.shape[1]), lambda i: (i, 0)),
                   pl.BlockSpec((tm, wg.shape[1]), lambda i: (i, 0))],
        out_shape=[jax.ShapeDtypeStruct((T, wa.shape[1]), BF16),
                   jax.ShapeDtypeStruct((T, wb.shape[1]), F32),
                   jax.ShapeDtypeStruct((T, wg.shape[1]), F32)],
        compiler_params=_params("parallel"),
        name="even_in_proj",
    )(x2d, wa, wb, wg)


def _gla_kernel(q_ref, k_ref, v_ref, r_ref, gate_ref, wf_ref, bf_ref, wb_ref, bb_ref, ng_ref,
                o_ref, la_ref, qd_ref, ke_ref, vt_ref, tot_ref, of_ref, ob_ref, *, seq):
    C = GLA_CHUNK
    n_chunks = seq // C
    tile = min(GLA_TILE, seq)
    cpt = tile // C
    gate = gate_ref[...]
    gate_hi = gate.astype(BF16)
    gate_lo = (gate - gate_hi.astype(F32)).astype(BF16)
    for d, (w_ref, b_ref) in enumerate(((wf_ref, bf_ref), (wb_ref, bb_ref))):
        w = w_ref[...]
        w_hi = w.astype(BF16)
        w_lo = (w - w_hi.astype(F32)).astype(BF16)
        z = _mm(gate_hi, w_hi) + _mm(gate_hi, w_lo) + _mm(gate_lo, w_hi) + b_ref[...]
        la_ref[d] = _log_sigmoid(z) * (1.0 / GLA_TAU)

    trow = lax.broadcasted_iota(jnp.int32, (tile, tile), 0)
    tcol = lax.broadcasted_iota(jnp.int32, (tile, tile), 1)
    same_chunk = (trow // C) == (tcol // C)
    keep = (same_chunk & (trow >= tcol), same_chunk & (tcol >= trow))
    rmod = lax.broadcasted_iota(jnp.int32, (tile, LANES), 0) % C
    lane = lax.broadcasted_iota(jnp.int32, (1, LANES), 1)
    head_lane = (lane < GLA_DK, lane >= GLA_DK)
    srow = lax.broadcasted_iota(jnp.int32, (2 * GLA_DV, LANES), 0)
    scol = lax.broadcasted_iota(jnp.int32, (2 * GLA_DV, LANES), 1)
    diag = (srow < GLA_DV) == (scol < GLA_DK)
    scale = GLA_DK ** -0.5

    def chunk_scan(x, d):
        step = 1
        while step < C:
            if d == 0:
                x = x + jnp.where(rmod >= step, pltpu.roll(x, step, 0), 0.0)
            else:
                x = x + jnp.where(rmod < C - step, pltpu.roll(x, tile - step, 0), 0.0)
            step *= 2
        return x

    def intra(j, _):
        r0 = pl.multiple_of(j * tile, tile)
        rows = pl.ds(r0, tile)
        q = q_ref[rows, :].astype(F32) * scale
        k = k_ref[rows, :].astype(F32)
        v = v_ref[rows, :]
        vf = v.astype(F32)
        for c in range(cpt):
            vt_ref[j * cpt + c] = vf[c * C:(c + 1) * C, :].T.astype(BF16)
        for d, out_ref in enumerate((of_ref, ob_ref)):
            la = la_ref[d, rows, :]
            b = chunk_scan(la, d)
            tots = [jnp.sum(la[c * C:(c + 1) * C, :], axis=0, keepdims=True) for c in range(cpt)]
            for c in range(cpt):
                tot_ref[d, pl.ds(j * cpt + c, 1), :] = tots[c]
            tot = jnp.concatenate([jnp.broadcast_to(t, (C, LANES)) for t in tots], axis=0)
            qd = (q * jnp.exp(b)).astype(BF16)
            kd = (k * jnp.exp(-b)).astype(BF16)
            qd_ref[d, rows, :] = qd
            ke_ref[d, rows, :] = (k * jnp.exp(tot - b)).astype(BF16)
            parts = []
            for h in range(2):
                qh = jnp.where(head_lane[h], qd, jnp.zeros_like(qd))
                s = jnp.where(keep[d], _mm_nt(qh, kd), 0.0)
                parts.append(_mm(s.astype(BF16), v[:, h * GLA_DV:(h + 1) * GLA_DV]))
            out_ref[rows, :] = jnp.concatenate(parts, axis=1)
        return 0

    lax.fori_loop(0, seq // tile, intra, 0)

    def inter(i, carry):
        new = []
        for d, (state, out_ref) in enumerate(zip(carry, (of_ref, ob_ref))):
            c = i if d == 0 else n_chunks - 1 - i
            rows = pl.ds(pl.multiple_of(c * C, C), C)
            out_ref[rows, :] += _mm_nt(qd_ref[d, rows, :], state.astype(BF16))
            upd = jnp.where(diag, _mm(vt_ref[c], ke_ref[d, rows, :]), 0.0)
            new.append(jnp.exp(tot_ref[d, pl.ds(c, 1), :]) * state + upd)
        return tuple(new)

    zero = jnp.zeros((2 * GLA_DV, LANES), F32)
    lax.fori_loop(0, n_chunks, inter, (zero, zero), unroll=GLA_UNROLL)

    blk = 256

    def finish(j, _):
        rows = pl.ds(pl.multiple_of(j * blk, blk), blk)
        o = of_ref[rows, :] + ob_ref[rows, :]
        g = ng_ref[...]
        outs = []
        for h in range(2):
            oh = o[:, h * GLA_DV:(h + 1) * GLA_DV]
            ms = jnp.mean(oh * oh, axis=-1, keepdims=True)
            outs.append(oh * lax.rsqrt(ms + RMS_EPS) * g[:, h * GLA_DV:(h + 1) * GLA_DV])
        r = r_ref[rows, :].astype(F32)
        o_ref[rows, :] = (jnp.concatenate(outs, axis=1) * (r * jax.nn.sigmoid(r))).astype(o_ref.dtype)
        return 0

    lax.fori_loop(0, seq // blk, finish, 0)


def _gla(h_a, gate, wf, bf, wb, bb, norm_g, B, S):
    pairs = GLA_HEADS // 2
    kq, kv = 2 * GLA_DK, 2 * GLA_DV
    sq = pl.BlockSpec((None, S, kq), lambda b, g: (b, 0, g))
    sk = pl.BlockSpec((None, S, kq), lambda b, g: (b, 0, pairs + g))
    sv = pl.BlockSpec((None, S, kv), lambda b, g: (b, 0, (2 * GLA_QK) // kv + g))
    sr = pl.BlockSpec((None, S, kv), lambda b, g: (b, 0, (2 * GLA_QK + GLA_VW) // kv + g))
    sg = pl.BlockSpec((None, S, LANES), lambda b, g: (b, 0, 0))
    sw = pl.BlockSpec((LANES, kq), lambda b, g: (0, g))
    sb = pl.BlockSpec((1, kq), lambda b, g: (0, g))
    sn = pl.BlockSpec((1, kv), lambda b, g: (0, g))
    return pl.pallas_call(
        functools.partial(_gla_kernel, seq=S),
        grid=(B, pairs),
        in_specs=[sq, sk, sv, sr, sg, sw, sb, sw, sb, sn],
        out_specs=pl.BlockSpec((None, S, kv), lambda b, g: (b, 0, g)),
        out_shape=jax.ShapeDtypeStruct((B, S, GLA_VW), BF16),
        scratch_shapes=[pltpu.VMEM((2, S, kq), F32),
                        pltpu.VMEM((2, S, kq), BF16),
                        pltpu.VMEM((2, S, kq), BF16),
                        pltpu.VMEM((S // GLA_CHUNK, kv, GLA_CHUNK), BF16),
                        pltpu.VMEM((2, S // GLA_CHUNK, kq), F32),
                        pltpu.VMEM((S, kv), F32), pltpu.VMEM((S, kv), F32)],
        compiler_params=_params("parallel", "parallel"),
        name="gla_mixer",
    )(h_a, h_a, h_a, h_a, gate, wf, bf, wb, bb, norm_g)


def _dil_kernel(q_ref, k_ref, v_ref, o_ref, m_ref, l_ref, acc_ref, bias_ref, *, seq):
    R = DIL_RADIUS
    pair = pl.program_id(1)
    lane = lax.broadcasted_iota(jnp.int32, (1, LANES), 1)
    first_head = lane < DIL_DH
    scale = DIL_DH ** -0.5

    for p_idx, (window, dil) in enumerate(DIL_PATTERNS):
        assert window // (2 * dil) == R
        length = seq // dil
        tq = min(DIL_TQ, length)
        nk = tq + 2 * R
        tpr = length // tq
        shift = tpr.bit_length() - 1
        assert tpr == 1 << shift and tq % R == 0
        qi = lax.broadcasted_iota(jnp.int32, (tq, nk), 0)
        kj = lax.broadcasted_iota(jnp.int32, (tq, nk), 1)
        dist = jnp.abs(kj - R - qi)
        for h in range(2):
            slope = 1.0 / jnp.left_shift(jnp.ones((tq, nk), jnp.int32), 2 * pair + h + 1).astype(F32)
            bias_ref[h, :tq, :nk] = jnp.where(dist <= R, -(dil * dist).astype(F32) * slope, MASK_VALUE)
        ones = jnp.ones((nk, LANES), BF16)

        def rows_at(start, n, dil=dil):
            return pl.ds(start, n, stride=dil) if dil > 1 else pl.ds(pl.multiple_of(start, R), n)

        def body(idx, _, dil=dil, tq=tq, nk=nk, tpr=tpr, shift=shift, p_idx=p_idx, rows_at=rows_at,
                 kj=kj, ones=ones):
            res = lax.shift_right_logical(idx, shift)
            t = jnp.bitwise_and(idx, tpr - 1)
            start = res + dil * tq * t
            has_prev = t > 0
            has_next = t < tpr - 1
            prev = jnp.where(has_prev, start - dil * R, start)
            nxt = jnp.where(has_next, start + dil * tq, start)
            q = (q_ref[rows_at(start, tq), :] * scale).astype(BF16)
            kcat = jnp.concatenate([k_ref[rows_at(prev, R), :], k_ref[rows_at(start, tq), :],
                                    k_ref[rows_at(nxt, R), :]], axis=0).astype(BF16)
            vcat = jnp.concatenate([v_ref[rows_at(prev, R), :], v_ref[rows_at(start, tq), :],
                                    v_ref[rows_at(nxt, R), :]], axis=0).astype(BF16)
            vones = jnp.concatenate([vcat, ones], axis=1)
            in_seq = ((kj >= R) | has_prev) & ((kj < R + tq) | has_next)
            ms, pvs = [], []
            for h in range(2):
                qh = jnp.where(first_head if h == 0 else ~first_head, q, jnp.zeros_like(q))
                s = jnp.where(in_seq, _mm_nt(qh, kcat) + bias_ref[h, :tq, :nk], MASK_VALUE)
                m = jnp.max(s, axis=-1, keepdims=True)
                ms.append(m)
                pvs.append(_mm(jnp.exp(s - m).astype(BF16), vones))
            out_rows = rows_at(start, tq)
            m_ref[p_idx, out_rows, :] = jnp.where(first_head, ms[0], ms[1])
            l_ref[p_idx, out_rows, :] = jnp.where(first_head, pvs[0][:, LANES:], pvs[1][:, LANES:])
            acc_ref[p_idx, out_rows, :] = jnp.where(first_head, pvs[0][:, :LANES], pvs[1][:, :LANES])
            return 0

        lax.fori_loop(0, seq // tq, body, 0, unroll=DIL_UNROLL)

    blk = 256
    n_pat = len(DIL_PATTERNS)

    def finish(j, _):
        rows = pl.ds(pl.multiple_of(j * blk, blk), blk)
        ms = [m_ref[p, rows, :] for p in range(n_pat)]
        m = functools.reduce(jnp.maximum, ms)
        ws = [jnp.exp(x - m) for x in ms]
        l = functools.reduce(lambda a, b: a + b, [w * l_ref[p, rows, :] for p, w in enumerate(ws)])
        acc = functools.reduce(lambda a, b: a + b, [w * acc_ref[p, rows, :] for p, w in enumerate(ws)])
        o_ref[rows, :] = (acc / l).astype(o_ref.dtype)
        return 0

    lax.fori_loop(0, seq // blk, finish, 0)


def _dilated(h_b, B, S):
    pairs = DIL_HEADS // 2
    sq = pl.BlockSpec((None, S, LANES), lambda b, g: (b, 0, g))
    sk = pl.BlockSpec((None, S, LANES), lambda b, g: (b, 0, pairs + g))
    sv = pl.BlockSpec((None, S, LANES), lambda b, g: (b, 0, 2 * pairs + g))
    return pl.pallas_call(
        functools.partial(_dil_kernel, seq=S),
        grid=(B, pairs),
        in_specs=[sq, sk, sv],
        out_specs=pl.BlockSpec((None, S, LANES), lambda b, g: (b, 0, g)),
        out_shape=jax.ShapeDtypeStruct((B, S, DIL_W), BF16),
        scratch_shapes=[pltpu.VMEM((len(DIL_PATTERNS), S, LANES), F32)] * 3
                       + [pltpu.VMEM((2, DIL_TQ, DIL_TQ + 2 * DIL_RADIUS), F32)],
        compiler_params=_params("parallel", "parallel"),
        name="dilated_mixer",
    )(h_b, h_b, h_b)


def _route(lt):
    g = [lt[i:i + 1, :] for i in range(MOE_GROUPS)]
    gmax = functools.reduce(jnp.maximum, g)
    gexp = [jnp.exp(x - gmax) for x in g]
    gsum = functools.reduce(lambda a, b: a + b, gexp)
    gprob = [x / gsum for x in gexp]
    g_val, g_idx = gprob[0], jnp.zeros_like(gprob[0], dtype=jnp.int32)
    for i in range(1, MOE_GROUPS):
        better = gprob[i] > g_val
        g_idx = jnp.where(better, i, g_idx)
        g_val = jnp.where(better, gprob[i], g_val)
    e = []
    for j in range(MOE_EPG):
        x = lt[MOE_GROUPS + j:MOE_GROUPS + j + 1, :]
        for gi in range(1, MOE_GROUPS):
            r0 = MOE_GROUPS + gi * MOE_EPG + j
            x = jnp.where(g_idx == gi, lt[r0:r0 + 1, :], x)
        e.append(x)
    emax = functools.reduce(jnp.maximum, e)
    eexp = [jnp.exp(x - emax) for x in e]
    esum = functools.reduce(lambda a, b: a + b, eexp)
    eprob = [x / esum for x in eexp]
    v1, i1 = eprob[0], jnp.zeros_like(g_idx)
    for j in range(1, MOE_EPG):
        better = eprob[j] > v1
        i1 = jnp.where(better, j, i1)
        v1 = jnp.where(better, eprob[j], v1)
    v2, i2 = jnp.full_like(v1, -1.0), jnp.zeros_like(g_idx)
    for j in range(MOE_EPG):
        cand = jnp.where(i1 == j, -1.0, eprob[j])
        better = cand > v2
        i2 = jnp.where(better, j, i2)
        v2 = jnp.where(better, cand, v2)
    den = v1 + v2
    ids = jnp.concatenate([g_idx * MOE_EPG + i1, g_idx * MOE_EPG + i2], axis=0)
    wts = jnp.concatenate([g_val * (v1 / den), g_val * (v2 / den)], axis=0)
    return ids, wts


def _proj_ln_route_kernel(*refs, n_in):
    x_ref = refs[0]
    o_refs = refs[1:1 + n_in]
    w_refs = refs[1 + n_in:1 + 2 * n_in]
    g_ref, b_ref, wr_ref, br_ref, x1_ref, x1b_ref, ids_ref, wts_ref = refs[1 + 2 * n_in:]
    m = _mm(o_refs[0][...], w_refs[0][...])
    for o_ref, w_ref in zip(o_refs[1:], w_refs[1:]):
        m = m + _mm(o_ref[...], w_ref[...])
    x1 = _layer_norm(ALPHA * x_ref[...] + m, g_ref[...], b_ref[...])
    x1_ref[...] = x1
    x_hi = x1.astype(BF16)
    x1b_ref[...] = x_hi
    x_lo = (x1 - x_hi.astype(F32)).astype(BF16)
    wr = wr_ref[...]
    w_hi = wr.astype(BF16)
    w_lo = (wr - w_hi.astype(F32)).astype(BF16)
    lt = _mm_nt(w_hi, x_hi) + _mm_nt(w_hi, x_lo) + _mm_nt(w_lo, x_hi) + br_ref[...]
    ids, wts = _route(lt)
    ids_ref[...] = ids
    wts_ref[...] = wts


def _proj_ln_route(x2d, outs, ws, ln_g, ln_b, wr, br):
    T = x2d.shape[0]
    tm = ROW_TILE
    n_in = len(outs)
    row = lambda w: pl.BlockSpec((tm, w), lambda i: (i, 0))
    tok = pl.BlockSpec((2, tm), lambda i: (0, i))
    return pl.pallas_call(
        functools.partial(_proj_ln_route_kernel, n_in=n_in),
        grid=(T // tm,),
        in_specs=[row(D_MODEL)] + [row(o.shape[1]) for o in outs] + [_full(w.shape) for w in ws]
                 + [_full(ln_g.shape), _full(ln_b.shape), _full(wr.shape), _full(br.shape)],
        out_specs=[row(D_MODEL), row(D_MODEL), tok, tok],
        out_shape=[jax.ShapeDtypeStruct((T, D_MODEL), F32), jax.ShapeDtypeStruct((T, D_MODEL), BF16),
                   jax.ShapeDtypeStruct((2, T), jnp.int32), jax.ShapeDtypeStruct((2, T), F32)],
        compiler_params=_params("parallel"),
        name="proj_ln_route",
    )(x2d, *outs, *ws, ln_g, ln_b, wr, br)


def _moe_kernel(te_ref, nu_ref, xs_ref, w1_ref, w3_ref, w2_ref, rw_ref, ys_ref, w13_s, w2_s):
    i = pl.program_id(0)

    @pl.when((i == 0) | (te_ref[i] != te_ref[jnp.maximum(i - 1, 0)]))
    def _():
        w13_s[:, :MOE_FF] = w1_ref[...].astype(BF16)
        w13_s[:, MOE_FF:] = w3_ref[...].astype(BF16)
        w2_s[...] = w2_ref[...].astype(BF16)

    @pl.when(i < nu_ref[0])
    def _():
        h = _mm(xs_ref[...], w13_s[...])
        h1, h3 = h[:, :MOE_FF], h[:, MOE_FF:]
        hidden = (h1 * jax.nn.sigmoid(h1)) * h3
        y = _mm(hidden.astype(BF16), w2_s[...])
        ys_ref[...] = (y * rw_ref[...]).astype(ys_ref.dtype)

    @pl.when(i >= nu_ref[0])
    def _():
        ys_ref[...] = jnp.zeros_like(ys_ref)


def _moe_experts(xs, w1, w3, w2, row_w, tile_expert, n_used):
    P = xs.shape[0]
    tm = MOE_TILE
    grid_spec = pltpu.PrefetchScalarGridSpec(
        num_scalar_prefetch=2,
        grid=(P // tm,),
        in_specs=[pl.BlockSpec((tm, D_MODEL), lambda i, te, nu: (i, 0)),
                  pl.BlockSpec((None, D_MODEL, MOE_FF), lambda i, te, nu: (te[i], 0, 0)),
                  pl.BlockSpec((None, D_MODEL, MOE_FF), lambda i, te, nu: (te[i], 0, 0)),
                  pl.BlockSpec((None, MOE_FF, D_MODEL), lambda i, te, nu: (te[i], 0, 0)),
                  pl.BlockSpec((tm, 1), lambda i, te, nu: (i, 0))],
        out_specs=pl.BlockSpec((tm, D_MODEL), lambda i, te, nu: (i, 0)),
        scratch_shapes=[pltpu.VMEM((D_MODEL, 2 * MOE_FF), BF16), pltpu.VMEM((MOE_FF, D_MODEL), BF16)],
    )
    return pl.pallas_call(
        _moe_kernel,
        grid_spec=grid_spec,
        out_shape=jax.ShapeDtypeStruct((P, D_MODEL), BF16),
        compiler_params=_params("arbitrary"),
        name="moe_experts",
    )(tile_expert, n_used, xs, w1, w3, w2, row_w)


def _moe_plan(ids, wts, T):
    tm = MOE_TILE
    n_assign = 2 * T
    e_flat = ids.reshape(n_assign)
    onehot = (e_flat[:, None] == jnp.arange(MOE_EXPERTS, dtype=jnp.int32)[None, :]).astype(jnp.int32)
    csum = jnp.cumsum(onehot, axis=0)
    counts = csum[-1]
    padded = ((counts + tm - 1) // tm) * tm
    ends = jnp.cumsum(padded)
    offs = ends - padded
    pos = jnp.sum(onehot * (csum - onehot + offs[None, :]), axis=1)
    P = n_assign + MOE_EXPERTS * tm
    n_tiles = P // tm
    tile_start = jnp.arange(n_tiles, dtype=jnp.int32) * tm
    tile_expert = jnp.minimum(jnp.sum((tile_start[:, None] >= ends[None, :]).astype(jnp.int32), axis=1),
                              MOE_EXPERTS - 1).astype(jnp.int32)
    n_used = (ends[-1:] // tm).astype(jnp.int32)
    tok = jnp.arange(n_assign, dtype=jnp.int32) % T
    _, tok_sorted, w_sorted = lax.sort((e_flat, tok, wts.reshape(n_assign)), num_keys=1, is_stable=True)
    t_onehot = (tile_expert[:, None] == jnp.arange(MOE_EXPERTS, dtype=jnp.int32)[None, :]).astype(jnp.int32)
    t_rank0 = tile_start - jnp.sum(t_onehot * offs[None, :], axis=1)
    t_count = jnp.sum(t_onehot * counts[None, :], axis=1)
    t_first = jnp.sum(t_onehot * (jnp.cumsum(counts) - counts)[None, :], axis=1)
    r = t_rank0[:, None] + jnp.arange(tm, dtype=jnp.int32)[None, :]
    valid = (r < t_count[:, None]).reshape(P)
    u = jnp.clip(t_first[:, None] + r, 0, n_assign - 1).reshape(P)
    src_tok = jnp.where(valid, tok_sorted.at[u].get(mode="promise_in_bounds"), 0)
    row_w = jnp.where(valid, w_sorted.at[u].get(mode="promise_in_bounds"), 0.0)
    return src_tok, row_w.reshape(P, 1), tile_expert, n_used, pos


def _combine_ln_kernel(x_ref, y0_ref, y1_ref, g_ref, b_ref, o_ref):
    y = ALPHA * x_ref[...] + y0_ref[...].astype(F32) + y1_ref[...].astype(F32)
    o_ref[...] = _layer_norm(y, g_ref[...], b_ref[...])


def _combine_ln(x2d, y0, y1, ln_g, ln_b):
    T = x2d.shape[0]
    tm = ROW_TILE
    row = pl.BlockSpec((tm, D_MODEL), lambda i: (i, 0))
    return pl.pallas_call(
        _combine_ln_kernel,
        grid=(T // tm,),
        in_specs=[row, row, row, _full(ln_g.shape), _full(ln_b.shape)],
        out_specs=row,
        out_shape=jax.ShapeDtypeStruct((T, D_MODEL), F32),
        compiler_params=_params("parallel"),
        name="combine_ln",
    )(x2d, y0, y1, ln_g, ln_b)


def _hier_moe_ln(x1, x1b, ids, wts, w1, w3, w2, ln_g, ln_b):
    T = x1.shape[0]
    src_tok, row_w, tile_expert, n_used, pos = _moe_plan(ids, wts, T)
    xs = x1b.at[src_tok].get(mode="promise_in_bounds")
    ys = _moe_experts(xs, w1, w3, w2, row_w, tile_expert, n_used)
    y0 = ys.at[pos[:T]].get(mode="promise_in_bounds")
    y1 = ys.at[pos[T:]].get(mode="promise_in_bounds")
    return _combine_ln(x1, y0, y1, ln_g, ln_b)


def _rope_block(x, c, sa, sb):
    return x * c + pltpu.roll(x, LANES - 32, 1) * sa + pltpu.roll(x, 32, 1) * sb


def _mla_in_kernel(x_ref, win_ref, qg_ref, kvg_ref, wq_ref, wk_ref, wv_ref, c_ref, sa_ref, sb_ref,
                   q_ref, k_ref, v_ref):
    x = x_ref[...].astype(BF16)
    h = _mm(x, win_ref[...])
    c_q = h[:, :MLA_Q_LORA]
    c_kv = h[:, MLA_Q_LORA:MLA_Q_LORA + MLA_KV_LORA]
    k_rope = h[:, MLA_Q_LORA + MLA_KV_LORA:]
    cqn = c_q * lax.rsqrt(jnp.mean(c_q * c_q, axis=-1, keepdims=True) + RMS_EPS) * qg_ref[...]
    ckn = c_kv * lax.rsqrt(jnp.mean(c_kv * c_kv, axis=-1, keepdims=True) + RMS_EPS) * kvg_ref[...]
    cqn, ckn = cqn.astype(BF16), ckn.astype(BF16)
    scale = (MLA_NOPE + MLA_ROPE) ** -0.5 * LOG2_E
    q = _mm(cqn, wq_ref[...]) * scale
    kn = _mm(ckn, wk_ref[...])
    v = _mm(ckn, wv_ref[...]).astype(v_ref.dtype)
    c, sa, sb = c_ref[...], sa_ref[...], sb_ref[...]
    kr = _rope_block(k_rope, c, sa, sb).astype(k_ref.dtype)
    ones = jnp.ones((x.shape[0], LANES), v_ref.dtype)
    for g in range(MLA_HEADS // 2):
        lo = 2 * LANES * g
        v_ref[:, lo:lo + LANES] = v[:, LANES * g:LANES * (g + 1)]
        v_ref[:, lo + LANES:lo + 2 * LANES] = ones
        q_ref[:, lo:lo + LANES] = q[:, lo:lo + LANES].astype(q_ref.dtype)
        q_ref[:, lo + LANES:lo + 2 * LANES] = _rope_block(q[:, lo + LANES:lo + 2 * LANES], c, sa, sb
                                                          ).astype(q_ref.dtype)
        k_ref[:, lo:lo + LANES] = kn[:, LANES * g:LANES * (g + 1)].astype(k_ref.dtype)
        k_ref[:, lo + LANES:lo + 2 * LANES] = kr


def _mla_in(x2d, win, qg, kvg, wq, wk, wv, c, sa, sb, S):
    T = x2d.shape[0]
    tm = ROW_TILE
    row = lambda w: pl.BlockSpec((tm, w), lambda i: (i, 0))
    n_s = S // tm
    tab = pl.BlockSpec((tm, LANES), lambda i: (i % n_s, 0))
    pairs = MLA_HEADS // 2
    return pl.pallas_call(
        _mla_in_kernel,
        grid=(T // tm,),
        in_specs=[row(D_MODEL), _full(win.shape), _full(qg.shape), _full(kvg.shape), _full(wq.shape),
                  _full(wk.shape), _full(wv.shape), tab, tab, tab],
        out_specs=[row(pairs * 2 * LANES)] * 3,
        out_shape=[jax.ShapeDtypeStruct((T, pairs * 2 * LANES), BF16)] * 3,
        compiler_params=_params("parallel"),
        name="mla_in_proj",
    )(x2d, win, qg, kvg, wq, wk, wv, c, sa, sb)


def _mla_attn_kernel(q_ref, k_ref, v_ref, o_ref, sa_ref, sb_ref, *, seq):
    lane2 = lax.broadcasted_iota(jnp.int32, (1, 2 * LANES), 1)
    rl = lane2 - LANES
    in_a = (lane2 < MLA_NOPE) | ((rl >= 0) & (rl < 16)) | ((rl >= 32) & (rl < 48))
    in_b = ((lane2 >= MLA_NOPE) & (lane2 < LANES)) | ((rl >= 16) & (rl < 32)) | ((rl >= 48) & (rl < 64))
    lane = lax.broadcasted_iota(jnp.int32, (1, LANES), 1)
    q = q_ref[...]
    tq = q.shape[0]
    tk = min(ATT_TK, seq)
    n_k = seq // tk
    q_heads = (jnp.where(in_a, q, jnp.zeros_like(q)), jnp.where(in_b, q, jnp.zeros_like(q)))
    s_refs = (sa_ref, sb_ref)

    def score_chunk(h, j, m):
        s = _mm_nt(q_heads[h], k_ref[j * tk:(j + 1) * tk, :])
        s_refs[h][:, j * tk:(j + 1) * tk] = s
        for c in range(tk // LANES):
            m = jnp.maximum(m, s[:, c * LANES:(c + 1) * LANES])
        return m

    def value_chunk(h, j, m_row, acc):
        p = jnp.exp2(s_refs[h][:, j * tk:(j + 1) * tk] - m_row).astype(BF16)
        return acc + _mm(p, v_ref[j * tk:(j + 1) * tk, :])

    neg = jnp.full((tq, LANES), -jnp.inf, F32)
    m = neg
    for j in range(n_k):
        m = score_chunk(0, j, m)
    m_a = jnp.max(m, axis=-1, keepdims=True)
    acc_a = jnp.zeros((tq, 2 * LANES), F32)
    m = neg
    for j in range(n_k):
        acc_a = value_chunk(0, j, m_a, acc_a)
        m = score_chunk(1, j, m)
    m_b = jnp.max(m, axis=-1, keepdims=True)
    acc_b = jnp.zeros((tq, 2 * LANES), F32)
    for j in range(n_k):
        acc_b = value_chunk(1, j, m_b, acc_b)
    out_a = acc_a[:, :LANES] / acc_a[:, LANES:]
    out_b = acc_b[:, :LANES] / acc_b[:, LANES:]
    o_ref[...] = jnp.where(lane < MLA_V, out_a, out_b).astype(o_ref.dtype)


def _mla_attn(q, k, v, B, S):
    pairs = MLA_HEADS // 2
    tq = min(ATT_TQ, S)
    return pl.pallas_call(
        functools.partial(_mla_attn_kernel, seq=S),
        grid=(B, pairs, S // tq),
        in_specs=[pl.BlockSpec((None, tq, 2 * LANES), lambda b, g, i: (b, i, g)),
                  pl.BlockSpec((None, S, 2 * LANES), lambda b, g, i: (b, 0, g)),
                  pl.BlockSpec((None, S, 2 * LANES), lambda b, g, i: (b, 0, g))],
        out_specs=pl.BlockSpec((None, tq, LANES), lambda b, g, i: (b, i, g)),
        out_shape=jax.ShapeDtypeStruct((B, S, pairs * LANES), BF16),
        scratch_shapes=[pltpu.VMEM((tq, S), F32), pltpu.VMEM((tq, S), F32)],
        compiler_params=_params("parallel", "parallel", "arbitrary"),
        name="mla_attention",
    )(q, k, v)


def _prep_even(w_in, wa_f, ba_f, wa_b, ba_b, norm_g, w_out):
    o_q, o_k, o_v, o_r = 0, GLA_QK, 2 * GLA_QK, 2 * GLA_QK + GLA_VW
    o_af = o_r + GLA_VW
    o_ab = o_af + GLA_RANK
    o_qb = o_ab + GLA_RANK
    wa = w_in[:, :o_af].astype(BF16)
    wb = w_in[:, o_qb:].astype(BF16)
    wg = jnp.zeros((D_MODEL, LANES), F32).at[:, :2 * GLA_RANK].set(w_in[:, o_af:o_qb]).astype(BF16)
    wf = jnp.zeros((LANES, GLA_QK), F32).at[:GLA_RANK].set(wa_f)
    wb_gate = jnp.zeros((LANES, GLA_QK), F32).at[GLA_RANK:2 * GLA_RANK].set(wa_b)
    return dict(wa=wa, wb=wb, wg=wg, wf=wf, bf=ba_f.reshape(1, -1), wb_gate=wb_gate,
                bb=ba_b.reshape(1, -1), norm_g=norm_g.reshape(1, -1),
                wo_a=w_out[:GLA_VW].astype(BF16), wo_b=w_out[GLA_VW:].astype(BF16))


def _prep_odd(w_in, q_norm, kv_norm, w_uq, w_ukv, w_out, S):
    half = MLA_ROPE // 2
    pairs = MLA_HEADS // 2
    kr = w_in[:, MLA_Q_LORA + MLA_KV_LORA:]
    kr_rep = jnp.concatenate([kr[:, :half], kr[:, :half], kr[:, half:], kr[:, half:],
                              jnp.zeros((D_MODEL, LANES - 2 * MLA_ROPE), F32)], axis=1)
    win = jnp.concatenate([w_in[:, :MLA_Q_LORA + MLA_KV_LORA], kr_rep], axis=1).astype(BF16)
    uq = w_uq.reshape(MLA_Q_LORA, MLA_HEADS, MLA_NOPE + MLA_ROPE)
    nope = uq[:, :, :MLA_NOPE].reshape(MLA_Q_LORA, pairs, 2 * MLA_NOPE)
    r1 = uq[:, :, MLA_NOPE:MLA_NOPE + half].reshape(MLA_Q_LORA, pairs, 2 * half)
    r2 = uq[:, :, MLA_NOPE + half:].reshape(MLA_Q_LORA, pairs, 2 * half)
    pad = jnp.zeros((MLA_Q_LORA, pairs, LANES - 2 * MLA_ROPE), F32)
    wq = jnp.concatenate([nope, r1, r2, pad], axis=2).reshape(MLA_Q_LORA, pairs * 2 * LANES).astype(BF16)
    ukv = w_ukv.reshape(MLA_KV_LORA, MLA_HEADS, MLA_NOPE + MLA_V)
    wk = ukv[:, :, :MLA_NOPE].reshape(MLA_KV_LORA, MLA_HEADS * MLA_NOPE).astype(BF16)
    wv = ukv[:, :, MLA_NOPE:].reshape(MLA_KV_LORA, MLA_HEADS * MLA_V).astype(BF16)
    inv = 1.0 / (ROPE_BASE ** (jnp.arange(0, MLA_ROPE, 2, dtype=F32) / MLA_ROPE))
    ang = jnp.arange(S, dtype=F32)[:, None] * inv[None, :]
    cos, sin = jnp.cos(ang), jnp.sin(ang)
    z16 = jnp.zeros((S, half), F32)
    z64 = jnp.zeros((S, LANES - 2 * MLA_ROPE), F32)
    c = jnp.concatenate([cos, cos, cos, cos, z64], axis=1)
    sa = jnp.concatenate([-sin, -sin, z16, z16, z64], axis=1)
    sb = jnp.concatenate([z16, z16, sin, sin, z64], axis=1)
    return dict(win=win, qg=q_norm.reshape(1, -1), kvg=kv_norm.reshape(1, -1), wq=wq, wk=wk, wv=wv,
                c=c, sa=sa, sb=sb, wo=w_out.astype(BF16))


def _prep_moe(wg, bg, we, be, w1, w3, w2):
    wr = jnp.zeros((32, D_MODEL), F32).at[:MOE_GROUPS].set(wg.T).at[MOE_GROUPS:MOE_GROUPS + MOE_EXPERTS].set(we.T)
    br = jnp.zeros((32, 1), F32).at[:MOE_GROUPS, 0].set(bg).at[MOE_GROUPS:MOE_GROUPS + MOE_EXPERTS, 0].set(be)
    return dict(wr=wr, br=br, w1=w1, w3=w3, w2=w2)


def _trunk(x, ev, od, moe, ln1_g, ln1_b, ln2_g, ln2_b):
    B, S, D = x.shape
    T = B * S
    x2d = x.reshape(T, D)
    for i in range(DEPTH):
        if i % 2 == 0:
            p = ev[i // 2]
            h_a, h_b, gate = _even_in(x2d, p["wa"], p["wb"], p["wg"])
            o_a = _gla(h_a.reshape(B, S, -1), gate.reshape(B, S, -1), p["wf"], p["bf"], p["wb_gate"],
                       p["bb"], p["norm_g"], B, S)
            o_b = _dilated(h_b.reshape(B, S, -1), B, S)
            outs = [o_a.reshape(T, -1), o_b.reshape(T, -1)]
            ws = [p["wo_a"], p["wo_b"]]
        else:
            p = od[i // 2]
            q, k, v = _mla_in(x2d, p["win"], p["qg"], p["kvg"], p["wq"], p["wk"], p["wv"],
                              p["c"], p["sa"], p["sb"], S)
            o = _mla_attn(q.reshape(B, S, -1), k.reshape(B, S, -1), v.reshape(B, S, -1), B, S)
            outs = [o.reshape(T, -1)]
            ws = [p["wo"]]
        m = moe[i]
        x1, x1b, ids, wts = _proj_ln_route(x2d, outs, ws, ln1_g[i:i + 1], ln1_b[i:i + 1], m["wr"], m["br"])
        x2d = _hier_moe_ln(x1, x1b, ids, wts, m["w1"], m["w3"], m["w2"], ln2_g[i:i + 1], ln2_b[i:i + 1])
    return x2d.reshape(B, S, D)


def kernel(x_prompt, x_sample, ev_w_in, ev_wa_f, ev_ba_f, ev_wa_b, ev_ba_b, ev_gla_norm, ev_w_out,
           od_w_in, od_q_norm, od_kv_norm, od_w_uq, od_w_ukv, od_w_out, ln1_g, ln1_b, ln2_g, ln2_b,
           moe_wg, moe_bg, moe_we, moe_be, moe_w1, moe_w3, moe_w2):
    S = x_prompt.shape[1]
    ev = [_prep_even(ev_w_in[j], ev_wa_f[j], ev_ba_f[j], ev_wa_b[j], ev_ba_b[j], ev_gla_norm[j], ev_w_out[j])
          for j in range(ev_w_in.shape[0])]
    od = [_prep_odd(od_w_in[j], od_q_norm[j], od_kv_norm[j], od_w_uq[j], od_w_ukv[j], od_w_out[j], S)
          for j in range(od_w_in.shape[0])]
    moe = [_prep_moe(moe_wg[i], moe_bg[i], moe_we[i], moe_be[i], moe_w1[i], moe_w3[i], moe_w2[i])
           for i in range(DEPTH)]
    return tuple(_trunk(x, ev, od, moe, ln1_g, ln1_b, ln2_g, ln2_b) for x in (x_prompt, x_sample))
```

```python
import functools

import numpy as np
import jax
import jax.numpy as jnp
from jax import lax
from jax.experimental import pallas as pl
from jax.experimental.pallas import tpu as pltpu

F32 = jnp.float32
BF16 = jnp.bfloat16
HIGHEST = lax.Precision.HIGHEST

D_MODEL = 1024
DEPTH = 2
GLA_HEADS, GLA_DK, GLA_DV = 4, 64, 128
GLA_QK, GLA_VW = GLA_HEADS * GLA_DK, GLA_HEADS * GLA_DV
GLA_RANK, GLA_TAU, GLA_CHUNK = 16, 16.0, 64
DIL_HEADS, DIL_DH = 8, 64
DIL_W = DIL_HEADS * DIL_DH
DIL_PATTERNS = ((128, 1), (512, 4), (2048, 16))
DIL_RADIUS = 64
MASK_VALUE = -1e30
MLA_HEADS, MLA_NOPE, MLA_ROPE, MLA_V = 16, 64, 32, 64
MLA_Q_LORA, MLA_KV_LORA = 384, 128
ROPE_BASE = 10000.0
MOE_GROUPS, MOE_EPG, MOE_EXPERTS, MOE_FF = 4, 4, 16, 512
ALPHA = (2 * DEPTH) ** 0.25
LN_EPS = 1e-5
RMS_EPS = 1e-6
LOG2_E = 1.4426950408889634

LANES = 128
VMEM_LIMIT = 56 * 1024 * 1024
ROW_TILE = 512
MOE_TILE = 512
ATT_TQ = 512
ATT_TK = 512
DIL_UNROLL = 4
GLA_UNROLL = 2
GLA_TILE = 256
DIL_TQ = 256


def _params(*sem):
    return pltpu.CompilerParams(dimension_semantics=sem, vmem_limit_bytes=VMEM_LIMIT)


def _full(shape):
    n = len(shape)
    return pl.BlockSpec(shape, lambda *_: (0,) * n)


def _mm(a, b):
    return jnp.dot(a, b, preferred_element_type=F32)


def _mm_nt(a, b):
    return lax.dot_general(a, b, (((1,), (1,)), ((), ())), preferred_element_type=F32)


def _mm_tn(a, b):
    return lax.dot_general(a, b, (((0,), (0,)), ((), ())), preferred_element_type=F32)


def _layer_norm(y, g, b):
    mu = jnp.mean(y, axis=-1, keepdims=True)
    yc = y - mu
    var = jnp.mean(yc * yc, axis=-1, keepdims=True)
    return yc * lax.rsqrt(var + LN_EPS) * g + b


def _log_sigmoid(z):
    return jnp.minimum(z, 0.0) - jnp.log(1.0 + jnp.exp(-jnp.abs(z)))


def _even_in_kernel(x_ref, wa_ref, wb_ref, wg_ref, oa_ref, ob_ref, og_ref):
    x = x_ref[...].astype(BF16)
    oa_ref[...] = _mm(x, wa_ref[...]).astype(oa_ref.dtype)
    ob_ref[...] = _mm(x, wb_ref[...])
    og_ref[...] = _mm(x, wg_ref[...])


def _even_in(x2d, wa, wb, wg):
    T = x2d.shape[0]
    tm = ROW_TILE
    return pl.pallas_call(
        _even_in_kernel,
        grid=(T // tm,),
        in_specs=[pl.BlockSpec((tm, D_MODEL), lambda i: (i, 0)),
                  _full(wa.shape), _full(wb.shape), _full(wg.shape)],
        out_specs=[pl.BlockSpec((tm, wa.shape[1]), lambda i: (i, 0)),
                   pl.BlockSpec((tm, wb.shape[1]), lambda i: (i, 0)),
                   pl.BlockSpec((tm, wg.shape[1]), lambda i: (i, 0))],
        out_shape=[jax.ShapeDtypeStruct((T, wa.shape[1]), BF16),
                   jax.ShapeDtypeStruct((T, wb.shape[1]), F32),
                   jax.ShapeDtypeStruct((T, wg.shape[1]), F32)],
        compiler_params=_params("parallel"),
        name="even_in_proj",
    )(x2d, wa, wb, wg)


def _gla_kernel(q_ref, k_ref, v_ref, r_ref, gate_ref, wf_ref, bf_ref, wb_ref, bb_ref, ng_ref,
                o_ref, la_ref, qd_ref, ke_ref, vt_ref, tot_ref, of_ref, ob_ref, *, seq):
    C = GLA_CHUNK
    n_chunks = seq // C
    tile = min(GLA_TILE, seq)
    cpt = tile // C
    gate = gate_ref[...]
    gate_hi = gate.astype(BF16)
    gate_lo = (gate - gate_hi.astype(F32)).astype(BF16)
    for d, (w_ref, b_ref) in enumerate(((wf_ref, bf_ref), (wb_ref, bb_ref))):
        w = w_ref[...]
        w_hi = w.astype(BF16)
        w_lo = (w - w_hi.astype(F32)).astype(BF16)
        z = _mm(gate_hi, w_hi) + _mm(gate_hi, w_lo) + _mm(gate_lo, w_hi) + b_ref[...]
        la_ref[d] = _log_sigmoid(z) * (1.0 / GLA_TAU)

    trow = lax.broadcasted_iota(jnp.int32, (tile, tile), 0)
    tcol = lax.broadcasted_iota(jnp.int32, (tile, tile), 1)
    same_chunk = (trow // C) == (tcol // C)
    keep = (same_chunk & (trow >= tcol), same_chunk & (tcol >= trow))
    rmod = lax.broadcasted_iota(jnp.int32, (tile, LANES), 0) % C
    lane = lax.broadcasted_iota(jnp.int32, (1, LANES), 1)
    head_lane = (lane < GLA_DK, lane >= GLA_DK)
    srow = lax.broadcasted_iota(jnp.int32, (2 * GLA_DV, LANES), 0)
    scol = lax.broadcasted_iota(jnp.int32, (2 * GLA_DV, LANES), 1)
    diag = (srow < GLA_DV) == (scol < GLA_DK)
    scale = GLA_DK ** -0.5

    def chunk_scan(x, d):
        step = 1
        while step < C:
            if d == 0:
                x = x + jnp.where(rmod >= step, pltpu.roll(x, step, 0), 0.0)
            else:
                x = x + jnp.where(rmod < C - step, pltpu.roll(x, tile - step, 0), 0.0)
            step *= 2
        return x

    def intra(j, _):
        r0 = pl.multiple_of(j * tile, tile)
        rows = pl.ds(r0, tile)
        q = q_ref[rows, :].astype(F32) * scale
        k = k_ref[rows, :].astype(F32)
        v = v_ref[rows, :]
        vf = v.astype(F32)
        for c in range(cpt):
            vt_ref[j * cpt + c] = vf[c * C:(c + 1) * C, :].T.astype(BF16)
        for d, out_ref in enumerate((of_ref, ob_ref)):
            la = la_ref[d, rows, :]
            b = chunk_scan(la, d)
            tots = [jnp.sum(la[c * C:(c + 1) * C, :], axis=0, keepdims=True) for c in range(cpt)]
            for c in range(cpt):
                tot_ref[d, pl.ds(j * cpt + c, 1), :] = tots[c]
            tot = jnp.concatenate([jnp.broadcast_to(t, (C, LANES)) for t in tots], axis=0)
            qd = (q * jnp.exp(b)).astype(BF16)
            kd = (k * jnp.exp(-b)).astype(BF16)
            qd_ref[d, rows, :] = qd
            ke_ref[d, rows, :] = (k * jnp.exp(tot - b)).astype(BF16)
            parts = []
            for h in range(2):
                qh = jnp.where(head_lane[h], qd, jnp.zeros_like(qd))
                s = jnp.where(keep[d], _mm_nt(qh, kd), 0.0)
                parts.append(_mm(s.astype(BF16), v[:, h * GLA_DV:(h + 1) * GLA_DV]))
            out_ref[rows, :] = jnp.concatenate(parts, axis=1)
        return 0

    lax.fori_loop(0, seq // tile, intra, 0)

    def inter(i, carry):
        new = []
        for d, (state, out_ref) in enumerate(zip(carry, (of_ref, ob_ref))):
            c = i if d == 0 else n_chunks - 1 - i
            rows = pl.ds(pl.multiple_of(c * C, C), C)
            out_ref[rows, :] += _mm_nt(qd_ref[d, rows, :], state.astype(BF16))
            upd = jnp.where(diag, _mm(vt_ref[c], ke_ref[d, rows, :]), 0.0)
            new.append(jnp.exp(tot_ref[d, pl.ds(c, 1), :]) * state + upd)
        return tuple(new)

    zero = jnp.zeros((2 * GLA_DV, LANES), F32)
    lax.fori_loop(0, n_chunks, inter, (zero, zero), unroll=GLA_UNROLL)

    blk = 256

    def finish(j, _):
        rows = pl.ds(pl.multiple_of(j * blk, blk), blk)
        o = of_ref[rows, :] + ob_ref[rows, :]
        g = ng_ref[...]
        outs = []
        for h in range(2):
            oh = o[:, h * GLA_DV:(h + 1) * GLA_DV]
            ms = jnp.mean(oh * oh, axis=-1, keepdims=True)
            outs.append(oh * lax.rsqrt(ms + RMS_EPS) * g[:, h * GLA_DV:(h + 1) * GLA_DV])
        r = r_ref[rows, :].astype(F32)
        o_ref[rows, :] = (jnp.concatenate(outs, axis=1) * (r * jax.nn.sigmoid(r))).astype(o_ref.dtype)
        return 0

    lax.fori_loop(0, seq // blk, finish, 0)


def _gla(h_a, gate, wf, bf, wb, bb, norm_g, B, S):
    pairs = GLA_HEADS // 2
    kq, kv = 2 * GLA_DK, 2 * GLA_DV
    sq = pl.BlockSpec((None, S, kq), lambda b, g: (b, 0, g))
    sk = pl.BlockSpec((None, S, kq), lambda b, g: (b, 0, pairs + g))
    sv = pl.BlockSpec((None, S, kv), lambda b, g: (b, 0, (2 * GLA_QK) // kv + g))
    sr = pl.BlockSpec((None, S, kv), lambda b, g: (b, 0, (2 * GLA_QK + GLA_VW) // kv + g))
    sg = pl.BlockSpec((None, S, LANES), lambda b, g: (b, 0, 0))
    sw = pl.BlockSpec((LANES, kq), lambda b, g: (0, g))
    sb = pl.BlockSpec((1, kq), lambda b, g: (0, g))
    sn = pl.BlockSpec((1, kv), lambda b, g: (0, g))
    return pl.pallas_call(
        functools.partial(_gla_kernel, seq=S),
        grid=(B, pairs),
        in_specs=[sq, sk, sv, sr, sg, sw, sb, sw, sb, sn],
        out_specs=pl.BlockSpec((None, S, kv), lambda b, g: (b, 0, g)),
        out_shape=jax.ShapeDtypeStruct((B, S, GLA_VW), BF16),
        scratch_shapes=[pltpu.VMEM((2, S, kq), F32),
                        pltpu.VMEM((2, S, kq), BF16),
                        pltpu.VMEM((2, S, kq), BF16),
                        pltpu.VMEM((S // GLA_CHUNK, kv, GLA_CHUNK), BF16),
                        pltpu.VMEM((2, S // GLA_CHUNK, kq), F32),
                        pltpu.VMEM((S, kv), F32), pltpu.VMEM((S, kv), F32)],
        compiler_params=_params("parallel", "parallel"),
        name="gla_mixer",
    )(h_a, h_a, h_a, h_a, gate, wf, bf, wb, bb, norm_g)


def _dil_kernel(q_ref, k_ref, v_ref, o_ref, m_ref, l_ref, acc_ref, bias_ref, *, seq):
    R = DIL_RADIUS
    pair = pl.program_id(1)
    lane = lax.broadcasted_iota(jnp.int32, (1, LANES), 1)
    first_head = lane < DIL_DH
    scale = DIL_DH ** -0.5

    for p_idx, (window, dil) in enumerate(DIL_PATTERNS):
        assert window // (2 * dil) == R
        length = seq // dil
        tq = min(DIL_TQ, length)
        nk = tq + 2 * R
        tpr = length // tq
        shift = tpr.bit_length() - 1
        assert tpr == 1 << shift and tq % R == 0
        qi = lax.broadcasted_iota(jnp.int32, (tq, nk), 0)
        kj = lax.broadcasted_iota(jnp.int32, (tq, nk), 1)
        dist = jnp.abs(kj - R - qi)
        for h in range(2):
            slope = 1.0 / jnp.left_shift(jnp.ones((tq, nk), jnp.int32), 2 * pair + h + 1).astype(F32)
            bias_ref[h, :tq, :nk] = jnp.where(dist <= R, -(dil * dist).astype(F32) * slope, MASK_VALUE)
        ones = jnp.ones((nk, LANES), BF16)

        def rows_at(start, n, dil=dil):
            return pl.ds(start, n, stride=dil) if dil > 1 else pl.ds(pl.multiple_of(start, R), n)

        def body(idx, _, dil=dil, tq=tq, nk=nk, tpr=tpr, shift=shift, p_idx=p_idx, rows_at=rows_at,
                 kj=kj, ones=ones):
            res = lax.shift_right_logical(idx, shift)
            t = jnp.bitwise_and(idx, tpr - 1)
            start = res + dil * tq * t
            has_prev = t > 0
            has_next = t < tpr - 1
            prev = jnp.where(has_prev, start - dil * R, start)
            nxt = jnp.where(has_next, start + dil * tq, start)
            q = (q_ref[rows_at(start, tq), :] * scale).astype(BF16)
            kcat = jnp.concatenate([k_ref[rows_at(prev, R), :], k_ref[rows_at(start, tq), :],
                                    k_ref[rows_at(nxt, R), :]], axis=0).astype(BF16)
            vcat = jnp.concatenate([v_ref[rows_at(prev, R), :], v_ref[rows_at(start, tq), :],
                                    v_ref[rows_at(nxt, R), :]], axis=0).astype(BF16)
            vones = jnp.concatenate([vcat, ones], axis=1)
            in_seq = ((kj >= R) | has_prev) & ((kj < R + tq) | has_next)
            ms, pvs = [], []
            for h in range(2):
                qh = jnp.where(first_head if h == 0 else ~first_head, q, jnp.zeros_like(q))
                s = jnp.where(in_seq, _mm_nt(qh, kcat) + bias_ref[h, :tq, :nk], MASK_VALUE)
                m = jnp.max(s, axis=-1, keepdims=True)
                ms.append(m)
                pvs.append(_mm(jnp.exp(s - m).astype(BF16), vones))
            out_rows = rows_at(start, tq)
            m_ref[p_idx, out_rows, :] = jnp.where(first_head, ms[0], ms[1])
            l_ref[p_idx, out_rows, :] = jnp.where(first_head, pvs[0][:, LANES:], pvs[1][:, LANES:])
            acc_ref[p_idx, out_rows, :] = jnp.where(first_head, pvs[0][:, :LANES], pvs[1][:, :LANES])
            return 0

        lax.fori_loop(0, seq // tq, body, 0, unroll=DIL_UNROLL)

    blk = 256
    n_pat = len(DIL_PATTERNS)

    def finish(j, _):
        rows = pl.ds(pl.multiple_of(j * blk, blk), blk)
        ms = [m_ref[p, rows, :] for p in range(n_pat)]
        m = functools.reduce(jnp.maximum, ms)
        ws = [jnp.exp(x - m) for x in ms]
        l = functools.reduce(lambda a, b: a + b, [w * l_ref[p, rows, :] for p, w in enumerate(ws)])
        acc = functools.reduce(lambda a, b: a + b, [w * acc_ref[p, rows, :] for p, w in enumerate(ws)])
        o_ref[rows, :] = (acc / l).astype(o_ref.dtype)
        return 0

    lax.fori_loop(0, seq // blk, finish, 0)


def _dilated(h_b, B, S):
    pairs = DIL_HEADS // 2
    sq = pl.BlockSpec((None, S, LANES), lambda b, g: (b, 0, g))
    sk = pl.BlockSpec((None, S, LANES), lambda b, g: (b, 0, pairs + g))
    sv = pl.BlockSpec((None, S, LANES), lambda b, g: (b, 0, 2 * pairs + g))
    return pl.pallas_call(
        functools.partial(_dil_kernel, seq=S),
        grid=(B, pairs),
        in_specs=[sq, sk, sv],
        out_specs=pl.BlockSpec((None, S, LANES), lambda b, g: (b, 0, g)),
        out_shape=jax.ShapeDtypeStruct((B, S, DIL_W), BF16),
        scratch_shapes=[pltpu.VMEM((len(DIL_PATTERNS), S, LANES), F32)] * 3
                       + [pltpu.VMEM((2, DIL_TQ, DIL_TQ + 2 * DIL_RADIUS), F32)],
        compiler_params=_params("parallel", "parallel"),
        name="dilated_mixer",
    )(h_b, h_b, h_b)


def _route(lt):
    g = [lt[i:i + 1, :] for i in range(MOE_GROUPS)]
    gmax = functools.reduce(jnp.maximum, g)
    gexp = [jnp.exp(x - gmax) for x in g]
    gsum = functools.reduce(lambda a, b: a + b, gexp)
    gprob = [x / gsum for x in gexp]
    g_val, g_idx = gprob[0], jnp.zeros_like(gprob[0], dtype=jnp.int32)
    for i in range(1, MOE_GROUPS):
        better = gprob[i] > g_val
        g_idx = jnp.where(better, i, g_idx)
        g_val = jnp.where(better, gprob[i], g_val)
    e = []
    for j in range(MOE_EPG):
        x = lt[MOE_GROUPS + j:MOE_GROUPS + j + 1, :]
        for gi in range(1, MOE_GROUPS):
            r0 = MOE_GROUPS + gi * MOE_EPG + j
            x = jnp.where(g_idx == gi, lt[r0:r0 + 1, :], x)
        e.append(x)
    emax = functools.reduce(jnp.maximum, e)
    eexp = [jnp.exp(x - emax) for x in e]
    esum = functools.reduce(lambda a, b: a + b, eexp)
    eprob = [x / esum for x in eexp]
    v1, i1 = eprob[0], jnp.zeros_like(g_idx)
    for j in range(1, MOE_EPG):
        better = eprob[j] > v1
        i1 = jnp.where(better, j, i1)
        v1 = jnp.where(better, eprob[j], v1)
    v2, i2 = jnp.full_like(v1, -1.0), jnp.zeros_like(g_idx)
    for j in range(MOE_EPG):
        cand = jnp.where(i1 == j, -1.0, eprob[j])
        better = cand > v2
        i2 = jnp.where(better, j, i2)
        v2 = jnp.where(better, cand, v2)
    den = v1 + v2
    ids = jnp.concatenate([g_idx * MOE_EPG + i1, g_idx * MOE_EPG + i2], axis=0)
    wts = jnp.concatenate([g_val * (v1 / den), g_val * (v2 / den)], axis=0)
    return ids, wts


def _proj_ln_route_kernel(*refs, n_in):
    x_ref = refs[0]
    o_refs = refs[1:1 + n_in]
    w_refs = refs[1 + n_in:1 + 2 * n_in]
    g_ref, b_ref, wr_ref, br_ref, x1_ref, x1b_ref, ids_ref, wts_ref = refs[1 + 2 * n_in:]
    m = _mm(o_refs[0][...], w_refs[0][...])
    for o_ref, w_ref in zip(o_refs[1:], w_refs[1:]):
        m = m + _mm(o_ref[...], w_ref[...])
    x1 = _layer_norm(ALPHA * x_ref[...] + m, g_ref[...], b_ref[...])
    x1_ref[...] = x1
    x_hi = x1.astype(BF16)
    x1b_ref[...] = x_hi
    x_lo = (x1 - x_hi.astype(F32)).astype(BF16)
    wr = wr_ref[...]
    w_hi = wr.astype(BF16)
    w_lo = (wr - w_hi.astype(F32)).astype(BF16)
    lt = _mm_nt(w_hi, x_hi) + _mm_nt(w_hi, x_lo) + _mm_nt(w_lo, x_hi) + br_ref[...]
    ids, wts = _route(lt)
    ids_ref[...] = ids
    wts_ref[...] = wts


def _proj_ln_route(x2d, outs, ws, ln_g, ln_b, wr, br):
    T = x2d.shape[0]
    tm = ROW_TILE
    n_in = len(outs)
    row = lambda w: pl.BlockSpec((tm, w), lambda i: (i, 0))
    tok = pl.BlockSpec((2, tm), lambda i: (0, i))
    return pl.pallas_call(
        functools.partial(_proj_ln_route_kernel, n_in=n_in),
        grid=(T // tm,),
        in_specs=[row(D_MODEL)] + [row(o.shape[1]) for o in outs] + [_full(w.shape) for w in ws]
                 + [_full(ln_g.shape), _full(ln_b.shape), _full(wr.shape), _full(br.shape)],
        out_specs=[row(D_MODEL), row(D_MODEL), tok, tok],
        out_shape=[jax.ShapeDtypeStruct((T, D_MODEL), F32), jax.ShapeDtypeStruct((T, D_MODEL), BF16),
                   jax.ShapeDtypeStruct((2, T), jnp.int32), jax.ShapeDtypeStruct((2, T), F32)],
        compiler_params=_params("parallel"),
        name="proj_ln_route",
    )(x2d, *outs, *ws, ln_g, ln_b, wr, br)


def _moe_kernel(te_ref, nu_ref, xs_ref, w1_ref, w3_ref, w2_ref, rw_ref, ys_ref, w13_s, w2_s):
    i = pl.program_id(0)

    @pl.when((i == 0) | (te_ref[i] != te_ref[jnp.maximum(i - 1, 0)]))
    def _():
        w13_s[:, :MOE_FF] = w1_ref[...].astype(BF16)
        w13_s[:, MOE_FF:] = w3_ref[...].astype(BF16)
        w2_s[...] = w2_ref[...].astype(BF16)

    @pl.when(i < nu_ref[0])
    def _():
        h = _mm(xs_ref[...], w13_s[...])
        h1, h3 = h[:, :MOE_FF], h[:, MOE_FF:]
        hidden = (h1 * jax.nn.sigmoid(h1)) * h3
        y = _mm(hidden.astype(BF16), w2_s[...])
        ys_ref[...] = (y * rw_ref[...]).astype(ys_ref.dtype)

    @pl.when(i >= nu_ref[0])
    def _():
        ys_ref[...] = jnp.zeros_like(ys_ref)


def _moe_experts(xs, w1, w3, w2, row_w, tile_expert, n_used):
    P = xs.shape[0]
    tm = MOE_TILE
    grid_spec = pltpu.PrefetchScalarGridSpec(
        num_scalar_prefetch=2,
        grid=(P // tm,),
        in_specs=[pl.BlockSpec((tm, D_MODEL), lambda i, te, nu: (i, 0)),
                  pl.BlockSpec((None, D_MODEL, MOE_FF), lambda i, te, nu: (te[i], 0, 0)),
                  pl.BlockSpec((None, D_MODEL, MOE_FF), lambda i, te, nu: (te[i], 0, 0)),
                  pl.BlockSpec((None, MOE_FF, D_MODEL), lambda i, te, nu: (te[i], 0, 0)),
                  pl.BlockSpec((tm, 1), lambda i, te, nu: (i, 0))],
        out_specs=pl.BlockSpec((tm, D_MODEL), lambda i, te, nu: (i, 0)),
        scratch_shapes=[pltpu.VMEM((D_MODEL, 2 * MOE_FF), BF16), pltpu.VMEM((MOE_FF, D_MODEL), BF16)],
    )
    return pl.pallas_call(
        _moe_kernel,
        grid_spec=grid_spec,
        out_shape=jax.ShapeDtypeStruct((P, D_MODEL), BF16),
        compiler_params=_params("arbitrary"),
        name="moe_experts",
    )(tile_expert, n_used, xs, w1, w3, w2, row_w)


def _moe_plan(ids, wts, T):
    tm = MOE_TILE
    n_assign = 2 * T
    e_flat = ids.reshape(n_assign)
    onehot = (e_flat[:, None] == jnp.arange(MOE_EXPERTS, dtype=jnp.int32)[None, :]).astype(jnp.int32)
    csum = jnp.cumsum(onehot, axis=0)
    counts = csum[-1]
    padded = ((counts + tm - 1) // tm) * tm
    ends = jnp.cumsum(padded)
    offs = ends - padded
    pos = jnp.sum(onehot * (csum - onehot + offs[None, :]), axis=1)
    P = n_assign + MOE_EXPERTS * tm
    n_tiles = P // tm
    tile_start = jnp.arange(n_tiles, dtype=jnp.int32) * tm
    tile_expert = jnp.minimum(jnp.sum((tile_start[:, None] >= ends[None, :]).astype(jnp.int32), axis=1),
                              MOE_EXPERTS - 1).astype(jnp.int32)
    n_used = (ends[-1:] // tm).astype(jnp.int32)
    tok = jnp.arange(n_assign, dtype=jnp.int32) % T
    _, tok_sorted, w_sorted = lax.sort((e_flat, tok, wts.reshape(n_assign)), num_keys=1, is_stable=True)
    t_onehot = (tile_expert[:, None] == jnp.arange(MOE_EXPERTS, dtype=jnp.int32)[None, :]).astype(jnp.int32)
    t_rank0 = tile_start - jnp.sum(t_onehot * offs[None, :], axis=1)
    t_count = jnp.sum(t_onehot * counts[None, :], axis=1)
    t_first = jnp.sum(t_onehot * (jnp.cumsum(counts) - counts)[None, :], axis=1)
    r = t_rank0[:, None] + jnp.arange(tm, dtype=jnp.int32)[None, :]
    valid = (r < t_count[:, None]).reshape(P)
    u = jnp.clip(t_first[:, None] + r, 0, n_assign - 1).reshape(P)
    src_tok = jnp.where(valid, tok_sorted.at[u].get(mode="promise_in_bounds"), 0)
    row_w = jnp.where(valid, w_sorted.at[u].get(mode="promise_in_bounds"), 0.0)
    return src_tok, row_w.reshape(P, 1), tile_expert, n_used, pos


def _combine_ln_kernel(x_ref, y0_ref, y1_ref, g_ref, b_ref, o_ref):
    y = ALPHA * x_ref[...] + y0_ref[...].astype(F32) + y1_ref[...].astype(F32)
    o_ref[...] = _layer_norm(y, g_ref[...], b_ref[...])


def _combine_ln(x2d, y0, y1, ln_g, ln_b):
    T = x2d.shape[0]
    tm = ROW_TILE
    row = pl.BlockSpec((tm, D_MODEL), lambda i: (i, 0))
    return pl.pallas_call(
        _combine_ln_kernel,
        grid=(T // tm,),
        in_specs=[row, row, row, _full(ln_g.shape), _full(ln_b.shape)],
        out_specs=row,
        out_shape=jax.ShapeDtypeStruct((T, D_MODEL), F32),
        compiler_params=_params("parallel"),
        name="combine_ln",
    )(x2d, y0, y1, ln_g, ln_b)


def _hier_moe_ln(x1, x1b, ids, wts, w1, w3, w2, ln_g, ln_b):
    T = x1.shape[0]
    src_tok, row_w, tile_expert, n_used, pos = _moe_plan(ids, wts, T)
    xs = x1b.at[src_tok].get(mode="promise_in_bounds")
    ys = _moe_experts(xs, w1, w3, w2, row_w, tile_expert, n_used)
    y0 = ys.at[pos[:T]].get(mode="promise_in_bounds")
    y1 = ys.at[pos[T:]].get(mode="promise_in_bounds")
    return _combine_ln(x1, y0, y1, ln_g, ln_b)


def _rope_block(x, c, sa, sb):
    return x * c + pltpu.roll(x, LANES - 32, 1) * sa + pltpu.roll(x, 32, 1) * sb


def _mla_in_kernel(x_ref, win_ref, qg_ref, kvg_ref, wq_ref, wk_ref, wv_ref, c_ref, sa_ref, sb_ref,
                   q_ref, k_ref, v_ref):
    x = x_ref[...].astype(BF16)
    h = _mm(x, win_ref[...])
    c_q = h[:, :MLA_Q_LORA]
    c_kv = h[:, MLA_Q_LORA:MLA_Q_LORA + MLA_KV_LORA]
    k_rope = h[:, MLA_Q_LORA + MLA_KV_LORA:]
    cqn = c_q * lax.rsqrt(jnp.mean(c_q * c_q, axis=-1, keepdims=True) + RMS_EPS) * qg_ref[...]
    ckn = c_kv * lax.rsqrt(jnp.mean(c_kv * c_kv, axis=-1, keepdims=True) + RMS_EPS) * kvg_ref[...]
    cqn, ckn = cqn.astype(BF16), ckn.astype(BF16)
    scale = (MLA_NOPE + MLA_ROPE) ** -0.5 * LOG2_E
    q = _mm(cqn, wq_ref[...]) * scale
    kn = _mm(ckn, wk_ref[...])
    v = _mm(ckn, wv_ref[...]).astype(v_ref.dtype)
    c, sa, sb = c_ref[...], sa_ref[...], sb_ref[...]
    kr = _rope_block(k_rope, c, sa, sb).astype(k_ref.dtype)
    ones = jnp.ones((x.shape[0], LANES), v_ref.dtype)
    for g in range(MLA_HEADS // 2):
        lo = 2 * LANES * g
        v_ref[:, lo:lo + LANES] = v[:, LANES * g:LANES * (g + 1)]
        v_ref[:, lo + LANES:lo + 2 * LANES] = ones
        q_ref[:, lo:lo + LANES] = q[:, lo:lo + LANES].astype(q_ref.dtype)
        q_ref[:, lo + LANES:lo + 2 * LANES] = _rope_block(q[:, lo + LANES:lo + 2 * LANES], c, sa, sb
                                                          ).astype(q_ref.dtype)
        k_ref[:, lo:lo + LANES] = kn[:, LANES * g:LANES * (g + 1)].astype(k_ref.dtype)
        k_ref[:, lo + LANES:lo + 2 * LANES] = kr


def _mla_in(x2d, win, qg, kvg, wq, wk, wv, c, sa, sb, S):
    T = x2d.shape[0]
    tm = ROW_TILE
    row = lambda w: pl.BlockSpec((tm, w), lambda i: (i, 0))
    n_s = S // tm
    tab = pl.BlockSpec((tm, LANES), lambda i: (i % n_s, 0))
    pairs = MLA_HEADS // 2
    return pl.pallas_call(
        _mla_in_kernel,
        grid=(T // tm,),
        in_specs=[row(D_MODEL), _full(win.shape), _full(qg.shape), _full(kvg.shape), _full(wq.shape),
                  _full(wk.shape), _full(wv.shape), tab, tab, tab],
        out_specs=[row(pairs * 2 * LANES)] * 3,
        out_shape=[jax.ShapeDtypeStruct((T, pairs * 2 * LANES), BF16)] * 3,
        compiler_params=_params("parallel"),
        name="mla_in_proj",
    )(x2d, win, qg, kvg, wq, wk, wv, c, sa, sb)


def _mla_attn_kernel(q_ref, k_ref, v_ref, o_ref, sa_ref, sb_ref, *, seq):
    lane2 = lax.broadcasted_iota(jnp.int32, (1, 2 * LANES), 1)
    rl = lane2 - LANES
    in_a = (lane2 < MLA_NOPE) | ((rl >= 0) & (rl < 16)) | ((rl >= 32) & (rl < 48))
    in_b = ((lane2 >= MLA_NOPE) & (lane2 < LANES)) | ((rl >= 16) & (rl < 32)) | ((rl >= 48) & (rl < 64))
    lane = lax.broadcasted_iota(jnp.int32, (1, LANES), 1)
    q = q_ref[...]
    tq = q.shape[0]
    tk = min(ATT_TK, seq)
    n_k = seq // tk
    q_heads = (jnp.where(in_a, q, jnp.zeros_like(q)), jnp.where(in_b, q, jnp.zeros_like(q)))
    s_refs = (sa_ref, sb_ref)

    def score_chunk(h, j, m):
        s = _mm_nt(q_heads[h], k_ref[j * tk:(j + 1) * tk, :])
        s_refs[h][:, j * tk:(j + 1) * tk] = s
        for c in range(tk // LANES):
            m = jnp.maximum(m, s[:, c * LANES:(c + 1) * LANES])
        return m

    def value_chunk(h, j, m_row, acc):
        p = jnp.exp2(s_refs[h][:, j * tk:(j + 1) * tk] - m_row).astype(BF16)
        return acc + _mm(p, v_ref[j * tk:(j + 1) * tk, :])

    neg = jnp.full((tq, LANES), -jnp.inf, F32)
    m = neg
    for j in range(n_k):
        m = score_chunk(0, j, m)
    m_a = jnp.max(m, axis=-1, keepdims=True)
    acc_a = jnp.zeros((tq, 2 * LANES), F32)
    m = neg
    for j in range(n_k):
        acc_a = value_chunk(0, j, m_a, acc_a)
        m = score_chunk(1, j, m)
    m_b = jnp.max(m, axis=-1, keepdims=True)
    acc_b = jnp.zeros((tq, 2 * LANES), F32)
    for j in range(n_k):
        acc_b = value_chunk(1, j, m_b, acc_b)
    out_a = acc_a[:, :LANES] / acc_a[:, LANES:]
    out_b = acc_b[:, :LANES] / acc_b[:, LANES:]
    o_ref[...] = jnp.where(lane < MLA_V, out_a, out_b).astype(o_ref.dtype)


def _mla_attn(q, k, v, B, S):
    pairs = MLA_HEADS // 2
    tq = min(ATT_TQ, S)
    return pl.pallas_call(
        functools.partial(_mla_attn_kernel, seq=S),
        grid=(B, pairs, S // tq),
        in_specs=[pl.BlockSpec((None, tq, 2 * LANES), lambda b, g, i: (b, i, g)),
                  pl.BlockSpec((None, S, 2 * LANES), lambda b, g, i: (b, 0, g)),
                  pl.BlockSpec((None, S, 2 * LANES), lambda b, g, i: (b, 0, g))],
        out_specs=pl.BlockSpec((None, tq, LANES), lambda b, g, i: (b, i, g)),
        out_shape=jax.ShapeDtypeStruct((B, S, pairs * LANES), BF16),
        scratch_shapes=[pltpu.VMEM((tq, S), F32), pltpu.VMEM((tq, S), F32)],
        compiler_params=_params("parallel", "parallel", "arbitrary"),
        name="mla_attention",
    )(q, k, v)


def _prep_even(w_in, wa_f, ba_f, wa_b, ba_b, norm_g, w_out):
    o_q, o_k, o_v, o_r = 0, GLA_QK, 2 * GLA_QK, 2 * GLA_QK + GLA_VW
    o_af = o_r + GLA_VW
    o_ab = o_af + GLA_RANK
    o_qb = o_ab + GLA_RANK
    wa = w_in[:, :o_af].astype(BF16)
    wb = w_in[:, o_qb:].astype(BF16)
    wg = jnp.zeros((D_MODEL, LANES), F32).at[:, :2 * GLA_RANK].set(w_in[:, o_af:o_qb]).astype(BF16)
    wf = jnp.zeros((LANES, GLA_QK), F32).at[:GLA_RANK].set(wa_f)
    wb_gate = jnp.zeros((LANES, GLA_QK), F32).at[GLA_RANK:2 * GLA_RANK].set(wa_b)
    return dict(wa=wa, wb=wb, wg=wg, wf=wf, bf=ba_f.reshape(1, -1), wb_gate=wb_gate,
                bb=ba_b.reshape(1, -1), norm_g=norm_g.reshape(1, -1),
                wo_a=w_out[:GLA_VW].astype(BF16), wo_b=w_out[GLA_VW:].astype(BF16))


def _prep_odd(w_in, q_norm, kv_norm, w_uq, w_ukv, w_out, S):
    half = MLA_ROPE // 2
    pairs = MLA_HEADS // 2
    kr = w_in[:, MLA_Q_LORA + MLA_KV_LORA:]
    kr_rep = jnp.concatenate([kr[:, :half], kr[:, :half], kr[:, half:], kr[:, half:],
                              jnp.zeros((D_MODEL, LANES - 2 * MLA_ROPE), F32)], axis=1)
    win = jnp.concatenate([w_in[:, :MLA_Q_LORA + MLA_KV_LORA], kr_rep], axis=1).astype(BF16)
    uq = w_uq.reshape(MLA_Q_LORA, MLA_HEADS, MLA_NOPE + MLA_ROPE)
    nope = uq[:, :, :MLA_NOPE].reshape(MLA_Q_LORA, pairs, 2 * MLA_NOPE)
    r1 = uq[:, :, MLA_NOPE:MLA_NOPE + half].reshape(MLA_Q_LORA, pairs, 2 * half)
    r2 = uq[:, :, MLA_NOPE + half:].reshape(MLA_Q_LORA, pairs, 2 * half)
    pad = jnp.zeros((MLA_Q_LORA, pairs, LANES - 2 * MLA_ROPE), F32)
    wq = jnp.concatenate([nope, r1, r2, pad], axis=2).reshape(MLA_Q_LORA, pairs * 2 * LANES).astype(BF16)
    ukv = w_ukv.reshape(MLA_KV_LORA, MLA_HEADS, MLA_NOPE + MLA_V)
    wk = ukv[:, :, :MLA_NOPE].reshape(MLA_KV_LORA, MLA_HEADS * MLA_NOPE).astype(BF16)
    wv = ukv[:, :, MLA_NOPE:].reshape(MLA_KV_LORA, MLA_HEADS * MLA_V).astype(BF16)
    inv = 1.0 / (ROPE_BASE ** (jnp.arange(0, MLA_ROPE, 2, dtype=F32) / MLA_ROPE))
    ang = jnp.arange(S, dtype=F32)[:, None] * inv[None, :]
    cos, sin = jnp.cos(ang), jnp.sin(ang)
    z16 = jnp.zeros((S, half), F32)
    z64 = jnp.zeros((S, LANES - 2 * MLA_ROPE), F32)
    c = jnp.concatenate([cos, cos, cos, cos, z64], axis=1)
    sa = jnp.concatenate([-sin, -sin, z16, z16, z64], axis=1)
    sb = jnp.concatenate([z16, z16, sin, sin, z64], axis=1)
    return dict(win=win, qg=q_norm.reshape(1, -1), kvg=kv_norm.reshape(1, -1), wq=wq, wk=wk, wv=wv,
                c=c, sa=sa, sb=sb, wo=w_out.astype(BF16))


def _prep_moe(wg, bg, we, be, w1, w3, w2):
    wr = jnp.zeros((32, D_MODEL), F32).at[:MOE_GROUPS].set(wg.T).at[MOE_GROUPS:MOE_GROUPS + MOE_EXPERTS].set(we.T)
    br = jnp.zeros((32, 1), F32).at[:MOE_GROUPS, 0].set(bg).at[MOE_GROUPS:MOE_GROUPS + MOE_EXPERTS, 0].set(be)
    return dict(wr=wr, br=br, w1=w1, w3=w3, w2=w2)


def _mix_route(i, x2d, B, S, prm):
    ev, od, moe, ln1_g, ln1_b = prm["ev"], prm["od"], prm["moe"], prm["ln1_g"], prm["ln1_b"]
    T = B * S
    if i % 2 == 0:
        p = ev[i // 2]
        h_a, h_b, gate = _even_in(x2d, p["wa"], p["wb"], p["wg"])
        o_a = _gla(h_a.reshape(B, S, -1), gate.reshape(B, S, -1), p["wf"], p["bf"], p["wb_gate"],
                   p["bb"], p["norm_g"], B, S)
        o_b = _dilated(h_b.reshape(B, S, -1), B, S)
        outs = [o_a.reshape(T, -1), o_b.reshape(T, -1)]
        ws = [p["wo_a"], p["wo_b"]]
    else:
        p = od[i // 2]
        q, k, v = _mla_in(x2d, p["win"], p["qg"], p["kvg"], p["wq"], p["wk"], p["wv"],
                          p["c"], p["sa"], p["sb"], S)
        o = _mla_attn(q.reshape(B, S, -1), k.reshape(B, S, -1), v.reshape(B, S, -1), B, S)
        outs = [o.reshape(T, -1)]
        ws = [p["wo"]]
    m = moe[i]
    return _proj_ln_route(x2d, outs, ws, ln1_g[i:i + 1], ln1_b[i:i + 1], m["wr"], m["br"])


def _channel_mix(i, routed, prm):
    m = prm["moe"][i]
    x1, x1b, ids, wts = routed
    return _hier_moe_ln(x1, x1b, ids, wts, m["w1"], m["w3"], m["w2"],
                        prm["ln2_g"][i:i + 1], prm["ln2_b"][i:i + 1])


def _after(x, token):
    x, _ = lax.optimization_barrier((x, token))
    return x


def _trunk_pair(xa, xb, prm):
    (Ba, S, D), Bb = xa.shape, xb.shape[0]
    a = xa.reshape(Ba * S, D)
    b = xb.reshape(Bb * S, D)
    routed_a = _mix_route(0, a, Ba, S, prm)
    for i in range(DEPTH):
        routed_b = _mix_route(i, _after(b, routed_a[2]), Bb, S, prm)
        a = _channel_mix(i, routed_a, prm)
        if i + 1 < DEPTH:
            routed_a = _mix_route(i + 1, _after(a, routed_b[2]), Ba, S, prm)
        b = _channel_mix(i, routed_b, prm)
    return a.reshape(Ba, S, D), b.reshape(Bb, S, D)


def kernel(x_prompt, x_sample, ev_w_in, ev_wa_f, ev_ba_f, ev_wa_b, ev_ba_b, ev_gla_norm, ev_w_out,
           od_w_in, od_q_norm, od_kv_norm, od_w_uq, od_w_ukv, od_w_out, ln1_g, ln1_b, ln2_g, ln2_b,
           moe_wg, moe_bg, moe_we, moe_be, moe_w1, moe_w3, moe_w2):
    S = x_prompt.shape[1]
    ev = [_prep_even(ev_w_in[j], ev_wa_f[j], ev_ba_f[j], ev_wa_b[j], ev_ba_b[j], ev_gla_norm[j], ev_w_out[j])
          for j in range(ev_w_in.shape[0])]
    od = [_prep_odd(od_w_in[j], od_q_norm[j], od_kv_norm[j], od_w_uq[j], od_w_ukv[j], od_w_out[j], S)
          for j in range(od_w_in.shape[0])]
    moe = [_prep_moe(moe_wg[i], moe_bg[i], moe_we[i], moe_be[i], moe_w1[i], moe_w3[i], moe_w2[i])
           for i in range(DEPTH)]
    prm = dict(ev=ev, od=od, moe=moe, ln1_g=ln1_g, ln1_b=ln1_b, ln2_g=ln2_g, ln2_b=ln2_b)
    return _trunk_pair(x_prompt, x_sample, prm)
```

```python
import functools

import numpy as np
import jax
import jax.numpy as jnp
from jax import lax
from jax.experimental import pallas as pl
from jax.experimental.pallas import tpu as pltpu

F32 = jnp.float32
BF16 = jnp.bfloat16
HIGHEST = lax.Precision.HIGHEST

D_MODEL = 1024
DEPTH = 2
GLA_HEADS, GLA_DK, GLA_DV = 4, 64, 128
GLA_QK, GLA_VW = GLA_HEADS * GLA_DK, GLA_HEADS * GLA_DV
GLA_RANK, GLA_TAU, GLA_CHUNK = 16, 16.0, 64
DIL_HEADS, DIL_DH = 8, 64
DIL_W = DIL_HEADS * DIL_DH
DIL_PATTERNS = ((128, 1), (512, 4), (2048, 16))
DIL_RADIUS = 64
MASK_VALUE = -1e30
MLA_HEADS, MLA_NOPE, MLA_ROPE, MLA_V = 16, 64, 32, 64
MLA_Q_LORA, MLA_KV_LORA = 384, 128
ROPE_BASE = 10000.0
MOE_GROUPS, MOE_EPG, MOE_EXPERTS, MOE_FF = 4, 4, 16, 512
ALPHA = (2 * DEPTH) ** 0.25
LN_EPS = 1e-5
RMS_EPS = 1e-6
LOG2_E = 1.4426950408889634

LANES = 128
VMEM_LIMIT = 56 * 1024 * 1024
ROW_TILE = 512
MOE_TILE = 512
ATT_TQ = 512
ATT_TK = 512
DIL_UNROLL = 2
GLA_UNROLL = 2
GLA_TILE = 256
DIL_TQ = 256


def _params(*sem):
    return pltpu.CompilerParams(dimension_semantics=sem, vmem_limit_bytes=VMEM_LIMIT)


def _full(shape):
    n = len(shape)
    return pl.BlockSpec(shape, lambda *_: (0,) * n)


def _mm(a, b):
    return jnp.dot(a, b, preferred_element_type=F32)


def _mm_nt(a, b):
    return lax.dot_general(a, b, (((1,), (1,)), ((), ())), preferred_element_type=F32)


def _mm_tn(a, b):
    return lax.dot_general(a, b, (((0,), (0,)), ((), ())), preferred_element_type=F32)


def _layer_norm(y, g, b):
    mu = jnp.mean(y, axis=-1, keepdims=True)
    yc = y - mu
    var = jnp.mean(yc * yc, axis=-1, keepdims=True)
    return yc * lax.rsqrt(var + LN_EPS) * g + b


def _log_sigmoid(z):
    return jnp.minimum(z, 0.0) - jnp.log(1.0 + jnp.exp(-jnp.abs(z)))


def _even_in_kernel(x_ref, wa_ref, wb_ref, wg_ref, oa_ref, ob_ref, og_ref):
    x = x_ref[...].astype(BF16)
    oa_ref[...] = _mm(x, wa_ref[...]).astype(oa_ref.dtype)
    ob_ref[...] = _mm(x, wb_ref[...])
    og_ref[...] = _mm(x, wg_ref[...])


def _even_in(x2d, wa, wb, wg):
    T = x2d.shape[0]
    tm = ROW_TILE
    return pl.pallas_call(
        _even_in_kernel,
        grid=(T // tm,),
        in_specs=[pl.BlockSpec((tm, D_MODEL), lambda i: (i, 0)),
                  _full(wa.shape), _full(wb.shape), _full(wg.shape)],
        out_specs=[pl.BlockSpec((tm, wa.shape[1]), lambda i: (i, 0)),
                   pl.BlockSpec((tm, wb.shape[1]), lambda i: (i, 0)),
                   pl.BlockSpec((tm, wg.shape[1]), lambda i: (i, 0))],
        out_shape=[jax.ShapeDtypeStruct((T, wa.shape[1]), BF16),
                   jax.ShapeDtypeStruct((T, wb.shape[1]), F32),
                   jax.ShapeDtypeStruct((T, wg.shape[1]), F32)],
        compiler_params=_params("parallel"),
        name="even_in_proj",
    )(x2d, wa, wb, wg)


def _gla_kernel(q_ref, k_ref, v_ref, r_ref, gate_ref, wf_ref, bf_ref, wb_ref, bb_ref, ng_ref,
                o_ref, la_ref, qd_ref, ke_ref, vt_ref, tot_ref, of_ref, ob_ref, *, seq):
    C = GLA_CHUNK
    n_chunks = seq // C
    tile = min(GLA_TILE, seq)
    cpt = tile // C
    gate = gate_ref[...]
    gate_hi = gate.astype(BF16)
    gate_lo = (gate - gate_hi.astype(F32)).astype(BF16)
    for d, (w_ref, b_ref) in enumerate(((wf_ref, bf_ref), (wb_ref, bb_ref))):
        w = w_ref[...]
        w_hi = w.astype(BF16)
        w_lo = (w - w_hi.astype(F32)).astype(BF16)
        z = _mm(gate_hi, w_hi) + _mm(gate_hi, w_lo) + _mm(gate_lo, w_hi) + b_ref[...]
        la_ref[d] = _log_sigmoid(z) * (1.0 / GLA_TAU)

    trow = lax.broadcasted_iota(jnp.int32, (tile, tile), 0)
    tcol = lax.broadcasted_iota(jnp.int32, (tile, tile), 1)
    same_chunk = (trow // C) == (tcol // C)
    keep = (same_chunk & (trow >= tcol), same_chunk & (tcol >= trow))
    rmod = lax.broadcasted_iota(jnp.int32, (tile, LANES), 0) % C
    lane = lax.broadcasted_iota(jnp.int32, (1, LANES), 1)
    head_lane = (lane < GLA_DK, lane >= GLA_DK)
    srow = lax.broadcasted_iota(jnp.int32, (2 * GLA_DV, LANES), 0)
    scol = lax.broadcasted_iota(jnp.int32, (2 * GLA_DV, LANES), 1)
    diag = (srow < GLA_DV) == (scol < GLA_DK)
    scale = GLA_DK ** -0.5

    def chunk_scan(x, d):
        step = 1
        while step < C:
            if d == 0:
                x = x + jnp.where(rmod >= step, pltpu.roll(x, step, 0), 0.0)
            else:
                x = x + jnp.where(rmod < C - step, pltpu.roll(x, tile - step, 0), 0.0)
            step *= 2
        return x

    def intra(j, _):
        r0 = pl.multiple_of(j * tile, tile)
        rows = pl.ds(r0, tile)
        q = q_ref[rows, :].astype(F32) * scale
        k = k_ref[rows, :].astype(F32)
        v = v_ref[rows, :]
        vf = v.astype(F32)
        for c in range(cpt):
            vt_ref[j * cpt + c] = vf[c * C:(c + 1) * C, :].T.astype(BF16)
        for d, out_ref in enumerate((of_ref, ob_ref)):
            la = la_ref[d, rows, :]
            b = chunk_scan(la, d)
            tots = [jnp.sum(la[c * C:(c + 1) * C, :], axis=0, keepdims=True) for c in range(cpt)]
            for c in range(cpt):
                tot_ref[d, pl.ds(j * cpt + c, 1), :] = tots[c]
            tot = jnp.concatenate([jnp.broadcast_to(t, (C, LANES)) for t in tots], axis=0)
            qd = (q * jnp.exp(b)).astype(BF16)
            kd = (k * jnp.exp(-b)).astype(BF16)
            qd_ref[d, rows, :] = qd
            ke_ref[d, rows, :] = (k * jnp.exp(tot - b)).astype(BF16)
            parts = []
            for h in range(2):
                qh = jnp.where(head_lane[h], qd, jnp.zeros_like(qd))
                s = jnp.where(keep[d], _mm_nt(qh, kd), 0.0)
                parts.append(_mm(s.astype(BF16), v[:, h * GLA_DV:(h + 1) * GLA_DV]))
            out_ref[rows, :] = jnp.concatenate(parts, axis=1)
        return 0

    lax.fori_loop(0, seq // tile, intra, 0)

    def inter(i, carry):
        new = []
        for d, (state, out_ref) in enumerate(zip(carry, (of_ref, ob_ref))):
            c = i if d == 0 else n_chunks - 1 - i
            rows = pl.ds(pl.multiple_of(c * C, C), C)
            out_ref[rows, :] += _mm_nt(qd_ref[d, rows, :], state.astype(BF16))
            upd = jnp.where(diag, _mm(vt_ref[c], ke_ref[d, rows, :]), 0.0)
            new.append(jnp.exp(tot_ref[d, pl.ds(c, 1), :]) * state + upd)
        return tuple(new)

    zero = jnp.zeros((2 * GLA_DV, LANES), F32)
    lax.fori_loop(0, n_chunks, inter, (zero, zero), unroll=GLA_UNROLL)

    blk = 256

    def finish(j, _):
        rows = pl.ds(pl.multiple_of(j * blk, blk), blk)
        o = of_ref[rows, :] + ob_ref[rows, :]
        g = ng_ref[...]
        outs = []
        for h in range(2):
            oh = o[:, h * GLA_DV:(h + 1) * GLA_DV]
            ms = jnp.mean(oh * oh, axis=-1, keepdims=True)
            outs.append(oh * lax.rsqrt(ms + RMS_EPS) * g[:, h * GLA_DV:(h + 1) * GLA_DV])
        r = r_ref[rows, :].astype(F32)
        o_ref[rows, :] = (jnp.concatenate(outs, axis=1) * (r * jax.nn.sigmoid(r))).astype(o_ref.dtype)
        return 0

    lax.fori_loop(0, seq // blk, finish, 0)


def _gla(h_a, gate, wf, bf, wb, bb, norm_g, B, S):
    pairs = GLA_HEADS // 2
    kq, kv = 2 * GLA_DK, 2 * GLA_DV
    sq = pl.BlockSpec((None, S, kq), lambda b, g: (b, 0, g))
    sk = pl.BlockSpec((None, S, kq), lambda b, g: (b, 0, pairs + g))
    sv = pl.BlockSpec((None, S, kv), lambda b, g: (b, 0, (2 * GLA_QK) // kv + g))
    sr = pl.BlockSpec((None, S, kv), lambda b, g: (b, 0, (2 * GLA_QK + GLA_VW) // kv + g))
    sg = pl.BlockSpec((None, S, LANES), lambda b, g: (b, 0, 0))
    sw = pl.BlockSpec((LANES, kq), lambda b, g: (0, g))
    sb = pl.BlockSpec((1, kq), lambda b, g: (0, g))
    sn = pl.BlockSpec((1, kv), lambda b, g: (0, g))
    return pl.pallas_call(
        functools.partial(_gla_kernel, seq=S),
        grid=(B, pairs),
        in_specs=[sq, sk, sv, sr, sg, sw, sb, sw, sb, sn],
        out_specs=pl.BlockSpec((None, S, kv), lambda b, g: (b, 0, g)),
        out_shape=jax.ShapeDtypeStruct((B, S, GLA_VW), BF16),
        scratch_shapes=[pltpu.VMEM((2, S, kq), F32),
                        pltpu.VMEM((2, S, kq), BF16),
                        pltpu.VMEM((2, S, kq), BF16),
                        pltpu.VMEM((S // GLA_CHUNK, kv, GLA_CHUNK), BF16),
                        pltpu.VMEM((2, S // GLA_CHUNK, kq), F32),
                        pltpu.VMEM((S, kv), F32), pltpu.VMEM((S, kv), F32)],
        compiler_params=_params("parallel", "parallel"),
        name="gla_mixer",
    )(h_a, h_a, h_a, h_a, gate, wf, bf, wb, bb, norm_g)


def _dil_kernel(q_ref, k_ref, v_ref, o_ref, m_ref, l_ref, acc_ref, bias_ref, s_ref, *, seq):
    R = DIL_RADIUS
    pair = pl.program_id(1)
    lane = lax.broadcasted_iota(jnp.int32, (1, LANES), 1)
    first_head = lane < DIL_DH
    scale = DIL_DH ** -0.5

    for p_idx, (window, dil) in enumerate(DIL_PATTERNS):
        assert window // (2 * dil) == R
        length = seq // dil
        tq = min(DIL_TQ, length)
        nk = tq + 2 * R
        tpr = length // tq
        shift = tpr.bit_length() - 1
        assert tpr == 1 << shift and tq % R == 0
        qi = lax.broadcasted_iota(jnp.int32, (tq, nk), 0)
        kj = lax.broadcasted_iota(jnp.int32, (tq, nk), 1)
        dist = jnp.abs(kj - R - qi)
        for h in range(2):
            slope = 1.0 / jnp.left_shift(jnp.ones((tq, nk), jnp.int32), 2 * pair + h + 1).astype(F32)
            bias_ref[h, :tq, :nk] = jnp.where(dist <= R, -(dil * dist).astype(F32) * slope, MASK_VALUE)
        ones = jnp.ones((nk, LANES), BF16)

        def rows_at(start, n, dil=dil):
            return pl.ds(start, n, stride=dil) if dil > 1 else pl.ds(pl.multiple_of(start, R), n)

        n_tiles = seq // tq
        assert n_tiles % 2 == 0

        def tile_pos(idx, dil=dil, tq=tq, tpr=tpr, shift=shift):
            res = lax.shift_right_logical(idx, shift)
            t = jnp.bitwise_and(idx, tpr - 1)
            start = res + dil * tq * t
            has_prev = t > 0
            has_next = t < tpr - 1
            prev = jnp.where(has_prev, start - dil * R, start)
            nxt = jnp.where(has_next, start + dil * tq, start)
            return start, prev, nxt, has_prev, has_next

        def scores(idx, slot, tq=tq, nk=nk, rows_at=rows_at, tile_pos=tile_pos):
            start, prev, nxt, _, _ = tile_pos(idx)
            q = (q_ref[rows_at(start, tq), :] * scale).astype(BF16)
            kcat = jnp.concatenate([k_ref[rows_at(prev, R), :], k_ref[rows_at(start, tq), :],
                                    k_ref[rows_at(nxt, R), :]], axis=0).astype(BF16)
            for h in range(2):
                qh = jnp.where(first_head if h == 0 else ~first_head, q, jnp.zeros_like(q))
                s_ref[slot, h, :tq, :nk] = _mm_nt(qh, kcat)

        def values(idx, slot, tq=tq, nk=nk, p_idx=p_idx, rows_at=rows_at, kj=kj, ones=ones,
                   tile_pos=tile_pos):
            start, prev, nxt, has_prev, has_next = tile_pos(idx)
            vcat = jnp.concatenate([v_ref[rows_at(prev, R), :], v_ref[rows_at(start, tq), :],
                                    v_ref[rows_at(nxt, R), :]], axis=0).astype(BF16)
            vones = jnp.concatenate([vcat, ones], axis=1)
            in_seq = ((kj >= R) | has_prev) & ((kj < R + tq) | has_next)
            ms, pvs = [], []
            for h in range(2):
                s = jnp.where(in_seq, s_ref[slot, h, :tq, :nk] + bias_ref[h, :tq, :nk], MASK_VALUE)
                m = jnp.max(s, axis=-1, keepdims=True)
                ms.append(m)
                pvs.append(_mm(jnp.exp(s - m).astype(BF16), vones))
            out_rows = rows_at(start, tq)
            m_ref[p_idx, out_rows, :] = jnp.where(first_head, ms[0], ms[1])
            l_ref[p_idx, out_rows, :] = jnp.where(first_head, pvs[0][:, LANES:], pvs[1][:, LANES:])
            acc_ref[p_idx, out_rows, :] = jnp.where(first_head, pvs[0][:, :LANES], pvs[1][:, :LANES])

        scores(0, 0)

        def body(i2, _, n_tiles=n_tiles, scores=scores, values=values):
            for u in range(2):
                idx = 2 * i2 + u
                scores(jnp.minimum(idx + 1, n_tiles - 1), 1 - u)
                values(idx, u)
            return 0

        lax.fori_loop(0, n_tiles // 2, body, 0, unroll=DIL_UNROLL)

    blk = 256
    n_pat = len(DIL_PATTERNS)

    def finish(j, _):
        rows = pl.ds(pl.multiple_of(j * blk, blk), blk)
        ms = [m_ref[p, rows, :] for p in range(n_pat)]
        m = functools.reduce(jnp.maximum, ms)
        ws = [jnp.exp(x - m) for x in ms]
        l = functools.reduce(lambda a, b: a + b, [w * l_ref[p, rows, :] for p, w in enumerate(ws)])
        acc = functools.reduce(lambda a, b: a + b, [w * acc_ref[p, rows, :] for p, w in enumerate(ws)])
        o_ref[rows, :] = (acc / l).astype(o_ref.dtype)
        return 0

    lax.fori_loop(0, seq // blk, finish, 0)


def _dilated(h_b, B, S):
    pairs = DIL_HEADS // 2
    sq = pl.BlockSpec((None, S, LANES), lambda b, g: (b, 0, g))
    sk = pl.BlockSpec((None, S, LANES), lambda b, g: (b, 0, pairs + g))
    sv = pl.BlockSpec((None, S, LANES), lambda b, g: (b, 0, 2 * pairs + g))
    return pl.pallas_call(
        functools.partial(_dil_kernel, seq=S),
        grid=(B, pairs),
        in_specs=[sq, sk, sv],
        out_specs=pl.BlockSpec((None, S, LANES), lambda b, g: (b, 0, g)),
        out_shape=jax.ShapeDtypeStruct((B, S, DIL_W), BF16),
        scratch_shapes=[pltpu.VMEM((len(DIL_PATTERNS), S, LANES), F32)] * 3
                       + [pltpu.VMEM((2, DIL_TQ, DIL_TQ + 2 * DIL_RADIUS), F32),
                          pltpu.VMEM((2, 2, DIL_TQ, DIL_TQ + 2 * DIL_RADIUS), F32)],
        compiler_params=_params("parallel", "parallel"),
        name="dilated_mixer",
    )(h_b, h_b, h_b)


def _route(lt):
    g = [lt[i:i + 1, :] for i in range(MOE_GROUPS)]
    gmax = functools.reduce(jnp.maximum, g)
    gexp = [jnp.exp(x - gmax) for x in g]
    gsum = functools.reduce(lambda a, b: a + b, gexp)
    gprob = [x / gsum for x in gexp]
    g_val, g_idx = gprob[0], jnp.zeros_like(gprob[0], dtype=jnp.int32)
    for i in range(1, MOE_GROUPS):
        better = gprob[i] > g_val
        g_idx = jnp.where(better, i, g_idx)
        g_val = jnp.where(better, gprob[i], g_val)
    e = []
    for j in range(MOE_EPG):
        x = lt[MOE_GROUPS + j:MOE_GROUPS + j + 1, :]
        for gi in range(1, MOE_GROUPS):
            r0 = MOE_GROUPS + gi * MOE_EPG + j
            x = jnp.where(g_idx == gi, lt[r0:r0 + 1, :], x)
        e.append(x)
    emax = functools.reduce(jnp.maximum, e)
    eexp = [jnp.exp(x - emax) for x in e]
    esum = functools.reduce(lambda a, b: a + b, eexp)
    eprob = [x / esum for x in eexp]
    v1, i1 = eprob[0], jnp.zeros_like(g_idx)
    for j in range(1, MOE_EPG):
        better = eprob[j] > v1
        i1 = jnp.where(better, j, i1)
        v1 = jnp.where(better, eprob[j], v1)
    v2, i2 = jnp.full_like(v1, -1.0), jnp.zeros_like(g_idx)
    for j in range(MOE_EPG):
        cand = jnp.where(i1 == j, -1.0, eprob[j])
        better = cand > v2
        i2 = jnp.where(better, j, i2)
        v2 = jnp.where(better, cand, v2)
    den = v1 + v2
    ids = jnp.concatenate([g_idx * MOE_EPG + i1, g_idx * MOE_EPG + i2], axis=0)
    wts = jnp.concatenate([g_val * (v1 / den), g_val * (v2 / den)], axis=0)
    return ids, wts


def _proj_ln_route_kernel(*refs, n_in):
    x_ref = refs[0]
    o_refs = refs[1:1 + n_in]
    w_refs = refs[1 + n_in:1 + 2 * n_in]
    g_ref, b_ref, wr_ref, br_ref, x1_ref, x1b_ref, ids_ref, wts_ref = refs[1 + 2 * n_in:]
    m = _mm(o_refs[0][...], w_refs[0][...])
    for o_ref, w_ref in zip(o_refs[1:], w_refs[1:]):
        m = m + _mm(o_ref[...], w_ref[...])
    x1 = _layer_norm(ALPHA * x_ref[...] + m, g_ref[...], b_ref[...])
    x1_ref[...] = x1
    x_hi = x1.astype(BF16)
    x1b_ref[...] = x_hi
    x_lo = (x1 - x_hi.astype(F32)).astype(BF16)
    wr = wr_ref[...]
    w_hi = wr.astype(BF16)
    w_lo = (wr - w_hi.astype(F32)).astype(BF16)
    lt = _mm_nt(w_hi, x_hi) + _mm_nt(w_hi, x_lo) + _mm_nt(w_lo, x_hi) + br_ref[...]
    ids, wts = _route(lt)
    ids_ref[...] = ids
    wts_ref[...] = wts


def _proj_ln_route(x2d, outs, ws, ln_g, ln_b, wr, br):
    T = x2d.shape[0]
    tm = ROW_TILE
    n_in = len(outs)
    row = lambda w: pl.BlockSpec((tm, w), lambda i: (i, 0))
    tok = pl.BlockSpec((2, tm), lambda i: (0, i))
    return pl.pallas_call(
        functools.partial(_proj_ln_route_kernel, n_in=n_in),
        grid=(T // tm,),
        in_specs=[row(D_MODEL)] + [row(o.shape[1]) for o in outs] + [_full(w.shape) for w in ws]
                 + [_full(ln_g.shape), _full(ln_b.shape), _full(wr.shape), _full(br.shape)],
        out_specs=[row(D_MODEL), row(D_MODEL), tok, tok],
        out_shape=[jax.ShapeDtypeStruct((T, D_MODEL), F32), jax.ShapeDtypeStruct((T, D_MODEL), BF16),
                   jax.ShapeDtypeStruct((2, T), jnp.int32), jax.ShapeDtypeStruct((2, T), F32)],
        compiler_params=_params("parallel"),
        name="proj_ln_route",
    )(x2d, *outs, *ws, ln_g, ln_b, wr, br)


def _moe_kernel(te_ref, nu_ref, xs_ref, w1_ref, w3_ref, w2_ref, rw_ref, ys_ref, w13_s, w2_s):
    i = pl.program_id(0)

    @pl.when((i == 0) | (te_ref[i] != te_ref[jnp.maximum(i - 1, 0)]))
    def _():
        w13_s[:, :MOE_FF] = w1_ref[...].astype(BF16)
        w13_s[:, MOE_FF:] = w3_ref[...].astype(BF16)
        w2_s[...] = w2_ref[...].astype(BF16)

    @pl.when(i < nu_ref[0])
    def _():
        h = _mm(xs_ref[...], w13_s[...])
        h1, h3 = h[:, :MOE_FF], h[:, MOE_FF:]
        hidden = (h1 * jax.nn.sigmoid(h1)) * h3
        y = _mm(hidden.astype(BF16), w2_s[...])
        ys_ref[...] = (y * rw_ref[...]).astype(ys_ref.dtype)

    @pl.when(i >= nu_ref[0])
    def _():
        ys_ref[...] = jnp.zeros_like(ys_ref)


def _moe_experts(xs, w1, w3, w2, layer, row_w, tile_expert, n_used):
    P = xs.shape[0]
    tm = MOE_TILE
    grid_spec = pltpu.PrefetchScalarGridSpec(
        num_scalar_prefetch=2,
        grid=(P // tm,),
        in_specs=[pl.BlockSpec((tm, D_MODEL), lambda i, te, nu: (i, 0)),
                  pl.BlockSpec((None, None, D_MODEL, MOE_FF), lambda i, te, nu: (layer, te[i], 0, 0)),
                  pl.BlockSpec((None, None, D_MODEL, MOE_FF), lambda i, te, nu: (layer, te[i], 0, 0)),
                  pl.BlockSpec((None, None, MOE_FF, D_MODEL), lambda i, te, nu: (layer, te[i], 0, 0)),
                  pl.BlockSpec((tm, 1), lambda i, te, nu: (i, 0))],
        out_specs=pl.BlockSpec((tm, D_MODEL), lambda i, te, nu: (i, 0)),
        scratch_shapes=[pltpu.VMEM((D_MODEL, 2 * MOE_FF), BF16), pltpu.VMEM((MOE_FF, D_MODEL), BF16)],
    )
    return pl.pallas_call(
        _moe_kernel,
        grid_spec=grid_spec,
        out_shape=jax.ShapeDtypeStruct((P, D_MODEL), BF16),
        compiler_params=_params("arbitrary"),
        name="moe_experts",
    )(tile_expert, n_used, xs, w1, w3, w2, row_w)


def _moe_plan(ids, wts, T):
    tm = MOE_TILE
    n_assign = 2 * T
    e_flat = ids.reshape(n_assign)
    onehot = (e_flat[:, None] == jnp.arange(MOE_EXPERTS, dtype=jnp.int32)[None, :]).astype(jnp.int32)
    csum = jnp.cumsum(onehot, axis=0)
    counts = csum[-1]
    padded = ((counts + tm - 1) // tm) * tm
    ends = jnp.cumsum(padded)
    offs = ends - padded
    pos = jnp.sum(onehot * (csum - onehot + offs[None, :]), axis=1)
    P = n_assign + MOE_EXPERTS * tm
    n_tiles = P // tm
    tile_start = jnp.arange(n_tiles, dtype=jnp.int32) * tm
    tile_expert = jnp.minimum(jnp.sum((tile_start[:, None] >= ends[None, :]).astype(jnp.int32), axis=1),
                              MOE_EXPERTS - 1).astype(jnp.int32)
    n_used = (ends[-1:] // tm).astype(jnp.int32)
    tok = jnp.arange(n_assign, dtype=jnp.int32) % T
    _, tok_sorted, w_sorted = lax.sort((e_flat, tok, wts.reshape(n_assign)), num_keys=1, is_stable=True)
    t_onehot = (tile_expert[:, None] == jnp.arange(MOE_EXPERTS, dtype=jnp.int32)[None, :]).astype(jnp.int32)
    t_rank0 = tile_start - jnp.sum(t_onehot * offs[None, :], axis=1)
    t_count = jnp.sum(t_onehot * counts[None, :], axis=1)
    t_first = jnp.sum(t_onehot * (jnp.cumsum(counts) - counts)[None, :], axis=1)
    r = t_rank0[:, None] + jnp.arange(tm, dtype=jnp.int32)[None, :]
    valid = (r < t_count[:, None]).reshape(P)
    u = jnp.clip(t_first[:, None] + r, 0, n_assign - 1).reshape(P)
    src_tok = jnp.where(valid, tok_sorted.at[u].get(mode="promise_in_bounds"), 0)
    row_w = jnp.where(valid, w_sorted.at[u].get(mode="promise_in_bounds"), 0.0)
    return src_tok, row_w.reshape(P, 1), tile_expert, n_used, pos


def _combine_ln_kernel(x_ref, y0_ref, y1_ref, g_ref, b_ref, o_ref):
    y = ALPHA * x_ref[...] + y0_ref[...].astype(F32) + y1_ref[...].astype(F32)
    o_ref[...] = _layer_norm(y, g_ref[...], b_ref[...])


def _combine_ln(x2d, y0, y1, ln_g, ln_b):
    T = x2d.shape[0]
    tm = ROW_TILE
    row = pl.BlockSpec((tm, D_MODEL), lambda i: (i, 0))
    return pl.pallas_call(
        _combine_ln_kernel,
        grid=(T // tm,),
        in_specs=[row, row, row, _full(ln_g.shape), _full(ln_b.shape)],
        out_specs=row,
        out_shape=jax.ShapeDtypeStruct((T, D_MODEL), F32),
        compiler_params=_params("parallel"),
        name="combine_ln",
    )(x2d, y0, y1, ln_g, ln_b)


def _hier_moe_ln(x1, x1b, plan, w1, w3, w2, layer, ln_g, ln_b):
    T = x1.shape[0]
    src_tok, row_w, tile_expert, n_used, pos = plan
    xs = x1b.at[src_tok].get(mode="promise_in_bounds")
    ys = _moe_experts(xs, w1, w3, w2, layer, row_w, tile_expert, n_used)
    y0 = ys.at[pos[:T]].get(mode="promise_in_bounds")
    y1 = ys.at[pos[T:]].get(mode="promise_in_bounds")
    return _combine_ln(x1, y0, y1, ln_g, ln_b)


def _rope_block(x, c, sa, sb):
    return x * c + pltpu.roll(x, LANES - 32, 1) * sa + pltpu.roll(x, 32, 1) * sb


def _mla_in_kernel(x_ref, win_ref, qg_ref, kvg_ref, wq_ref, wk_ref, wv_ref, c_ref, sa_ref, sb_ref,
                   q_ref, k_ref, v_ref):
    x = x_ref[...].astype(BF16)
    h = _mm(x, win_ref[...])
    c_q = h[:, :MLA_Q_LORA]
    c_kv = h[:, MLA_Q_LORA:MLA_Q_LORA + MLA_KV_LORA]
    k_rope = h[:, MLA_Q_LORA + MLA_KV_LORA:]
    cqn = c_q * lax.rsqrt(jnp.mean(c_q * c_q, axis=-1, keepdims=True) + RMS_EPS) * qg_ref[...]
    ckn = c_kv * lax.rsqrt(jnp.mean(c_kv * c_kv, axis=-1, keepdims=True) + RMS_EPS) * kvg_ref[...]
    cqn, ckn = cqn.astype(BF16), ckn.astype(BF16)
    scale = (MLA_NOPE + MLA_ROPE) ** -0.5 * LOG2_E
    q = _mm(cqn, wq_ref[...]) * scale
    kn = _mm(ckn, wk_ref[...])
    v = _mm(ckn, wv_ref[...]).astype(v_ref.dtype)
    c, sa, sb = c_ref[...], sa_ref[...], sb_ref[...]
    kr = _rope_block(k_rope, c, sa, sb).astype(k_ref.dtype)
    ones = jnp.ones((x.shape[0], LANES), v_ref.dtype)
    for g in range(MLA_HEADS // 2):
        lo = 2 * LANES * g
        v_ref[:, lo:lo + LANES] = v[:, LANES * g:LANES * (g + 1)]
        v_ref[:, lo + LANES:lo + 2 * LANES] = ones
        q_ref[:, lo:lo + LANES] = q[:, lo:lo + LANES].astype(q_ref.dtype)
        q_ref[:, lo + LANES:lo + 2 * LANES] = _rope_block(q[:, lo + LANES:lo + 2 * LANES], c, sa, sb
                                                          ).astype(q_ref.dtype)
        k_ref[:, lo:lo + LANES] = kn[:, LANES * g:LANES * (g + 1)].astype(k_ref.dtype)
        k_ref[:, lo + LANES:lo + 2 * LANES] = kr


def _mla_in(x2d, win, qg, kvg, wq, wk, wv, c, sa, sb, S):
    T = x2d.shape[0]
    tm = ROW_TILE
    row = lambda w: pl.BlockSpec((tm, w), lambda i: (i, 0))
    n_s = S // tm
    tab = pl.BlockSpec((tm, LANES), lambda i: (i % n_s, 0))
    pairs = MLA_HEADS // 2
    return pl.pallas_call(
        _mla_in_kernel,
        grid=(T // tm,),
        in_specs=[row(D_MODEL), _full(win.shape), _full(qg.shape), _full(kvg.shape), _full(wq.shape),
                  _full(wk.shape), _full(wv.shape), tab, tab, tab],
        out_specs=[row(pairs * 2 * LANES)] * 3,
        out_shape=[jax.ShapeDtypeStruct((T, pairs * 2 * LANES), BF16)] * 3,
        compiler_params=_params("parallel"),
        name="mla_in_proj",
    )(x2d, win, qg, kvg, wq, wk, wv, c, sa, sb)


def _mla_attn_kernel(q_ref, k_ref, v_ref, o_ref, sa_ref, sb_ref, *, seq):
    lane2 = lax.broadcasted_iota(jnp.int32, (1, 2 * LANES), 1)
    rl = lane2 - LANES
    in_a = (lane2 < MLA_NOPE) | ((rl >= 0) & (rl < 16)) | ((rl >= 32) & (rl < 48))
    in_b = ((lane2 >= MLA_NOPE) & (lane2 < LANES)) | ((rl >= 16) & (rl < 32)) | ((rl >= 48) & (rl < 64))
    lane = lax.broadcasted_iota(jnp.int32, (1, LANES), 1)
    q = q_ref[...]
    tq = q.shape[0]
    tk = min(ATT_TK, seq)
    n_k = seq // tk
    q_heads = (jnp.where(in_a, q, jnp.zeros_like(q)), jnp.where(in_b, q, jnp.zeros_like(q)))
    s_refs = (sa_ref, sb_ref)

    def score_chunk(h, j, m):
        s = _mm_nt(q_heads[h], k_ref[j * tk:(j + 1) * tk, :])
        s_refs[h][:, j * tk:(j + 1) * tk] = s
        for c in range(tk // LANES):
            m = jnp.maximum(m, s[:, c * LANES:(c + 1) * LANES])
        return m

    def value_chunk(h, j, m_row, acc):
        p = jnp.exp2(s_refs[h][:, j * tk:(j + 1) * tk] - m_row).astype(BF16)
        return acc + _mm(p, v_ref[j * tk:(j + 1) * tk, :])

    neg = jnp.full((tq, LANES), -jnp.inf, F32)
    m = neg
    for j in range(n_k):
        m = score_chunk(0, j, m)
    m_a = jnp.max(m, axis=-1, keepdims=True)
    acc_a = jnp.zeros((tq, 2 * LANES), F32)
    m = neg
    for j in range(n_k):
        acc_a = value_chunk(0, j, m_a, acc_a)
        m = score_chunk(1, j, m)
    m_b = jnp.max(m, axis=-1, keepdims=True)
    acc_b = jnp.zeros((tq, 2 * LANES), F32)
    for j in range(n_k):
        acc_b = value_chunk(1, j, m_b, acc_b)
    out_a = acc_a[:, :LANES] / acc_a[:, LANES:]
    out_b = acc_b[:, :LANES] / acc_b[:, LANES:]
    o_ref[...] = jnp.where(lane < MLA_V, out_a, out_b).astype(o_ref.dtype)


def _mla_attn(q, k, v, B, S):
    pairs = MLA_HEADS // 2
    tq = min(ATT_TQ, S)
    return pl.pallas_call(
        functools.partial(_mla_attn_kernel, seq=S),
        grid=(B, pairs, S // tq),
        in_specs=[pl.BlockSpec((None, tq, 2 * LANES), lambda b, g, i: (b, i, g)),
                  pl.BlockSpec((None, S, 2 * LANES), lambda b, g, i: (b, 0, g)),
                  pl.BlockSpec((None, S, 2 * LANES), lambda b, g, i: (b, 0, g))],
        out_specs=pl.BlockSpec((None, tq, LANES), lambda b, g, i: (b, i, g)),
        out_shape=jax.ShapeDtypeStruct((B, S, pairs * LANES), BF16),
        scratch_shapes=[pltpu.VMEM((tq, S), F32), pltpu.VMEM((tq, S), F32)],
        compiler_params=_params("parallel", "parallel", "arbitrary"),
        name="mla_attention",
    )(q, k, v)


def _prep_even(w_in, wa_f, ba_f, wa_b, ba_b, norm_g, w_out):
    o_q, o_k, o_v, o_r = 0, GLA_QK, 2 * GLA_QK, 2 * GLA_QK + GLA_VW
    o_af = o_r + GLA_VW
    o_ab = o_af + GLA_RANK
    o_qb = o_ab + GLA_RANK
    wa = w_in[:, :o_af].astype(BF16)
    wb = w_in[:, o_qb:].astype(BF16)
    wg = jnp.zeros((D_MODEL, LANES), F32).at[:, :2 * GLA_RANK].set(w_in[:, o_af:o_qb]).astype(BF16)
    wf = jnp.zeros((LANES, GLA_QK), F32).at[:GLA_RANK].set(wa_f)
    wb_gate = jnp.zeros((LANES, GLA_QK), F32).at[GLA_RANK:2 * GLA_RANK].set(wa_b)
    return dict(wa=wa, wb=wb, wg=wg, wf=wf, bf=ba_f.reshape(1, -1), wb_gate=wb_gate,
                bb=ba_b.reshape(1, -1), norm_g=norm_g.reshape(1, -1),
                wo_a=w_out[:GLA_VW].astype(BF16), wo_b=w_out[GLA_VW:].astype(BF16))


def _prep_odd(w_in, q_norm, kv_norm, w_uq, w_ukv, w_out, S):
    half = MLA_ROPE // 2
    pairs = MLA_HEADS // 2
    kr = w_in[:, MLA_Q_LORA + MLA_KV_LORA:]
    kr_rep = jnp.concatenate([kr[:, :half], kr[:, :half], kr[:, half:], kr[:, half:],
                              jnp.zeros((D_MODEL, LANES - 2 * MLA_ROPE), F32)], axis=1)
    win = jnp.concatenate([w_in[:, :MLA_Q_LORA + MLA_KV_LORA], kr_rep], axis=1).astype(BF16)
    uq = w_uq.reshape(MLA_Q_LORA, MLA_HEADS, MLA_NOPE + MLA_ROPE)
    nope = uq[:, :, :MLA_NOPE].reshape(MLA_Q_LORA, pairs, 2 * MLA_NOPE)
    r1 = uq[:, :, MLA_NOPE:MLA_NOPE + half].reshape(MLA_Q_LORA, pairs, 2 * half)
    r2 = uq[:, :, MLA_NOPE + half:].reshape(MLA_Q_LORA, pairs, 2 * half)
    pad = jnp.zeros((MLA_Q_LORA, pairs, LANES - 2 * MLA_ROPE), F32)
    wq = jnp.concatenate([nope, r1, r2, pad], axis=2).reshape(MLA_Q_LORA, pairs * 2 * LANES).astype(BF16)
    ukv = w_ukv.reshape(MLA_KV_LORA, MLA_HEADS, MLA_NOPE + MLA_V)
    wk = ukv[:, :, :MLA_NOPE].reshape(MLA_KV_LORA, MLA_HEADS * MLA_NOPE).astype(BF16)
    wv = ukv[:, :, MLA_NOPE:].reshape(MLA_KV_LORA, MLA_HEADS * MLA_V).astype(BF16)
    inv = 1.0 / (ROPE_BASE ** (jnp.arange(0, MLA_ROPE, 2, dtype=F32) / MLA_ROPE))
    ang = jnp.arange(S, dtype=F32)[:, None] * inv[None, :]
    cos, sin = jnp.cos(ang), jnp.sin(ang)
    z16 = jnp.zeros((S, half), F32)
    z64 = jnp.zeros((S, LANES - 2 * MLA_ROPE), F32)
    c = jnp.concatenate([cos, cos, cos, cos, z64], axis=1)
    sa = jnp.concatenate([-sin, -sin, z16, z16, z64], axis=1)
    sb = jnp.concatenate([z16, z16, sin, sin, z64], axis=1)
    return dict(win=win, qg=q_norm.reshape(1, -1), kvg=kv_norm.reshape(1, -1), wq=wq, wk=wk, wv=wv,
                c=c, sa=sa, sb=sb, wo=w_out.astype(BF16))


def _prep_moe(wg, bg, we, be):
    wr = jnp.zeros((32, D_MODEL), F32).at[:MOE_GROUPS].set(wg.T).at[MOE_GROUPS:MOE_GROUPS + MOE_EXPERTS].set(we.T)
    br = jnp.zeros((32, 1), F32).at[:MOE_GROUPS, 0].set(bg).at[MOE_GROUPS:MOE_GROUPS + MOE_EXPERTS, 0].set(be)
    return dict(wr=wr, br=br)


def _mix_route(i, x2d, B, S, prm):
    ev, od, moe, ln1_g, ln1_b = prm["ev"], prm["od"], prm["moe"], prm["ln1_g"], prm["ln1_b"]
    T = B * S
    if i % 2 == 0:
        p = ev[i // 2]
        h_a, h_b, gate = _even_in(x2d, p["wa"], p["wb"], p["wg"])
        o_a = _gla(h_a.reshape(B, S, -1), gate.reshape(B, S, -1), p["wf"], p["bf"], p["wb_gate"],
                   p["bb"], p["norm_g"], B, S)
        o_b = _dilated(h_b.reshape(B, S, -1), B, S)
        outs = [o_a.reshape(T, -1), o_b.reshape(T, -1)]
        ws = [p["wo_a"], p["wo_b"]]
    else:
        p = od[i // 2]
        q, k, v = _mla_in(x2d, p["win"], p["qg"], p["kvg"], p["wq"], p["wk"], p["wv"],
                          p["c"], p["sa"], p["sb"], S)
        o = _mla_attn(q.reshape(B, S, -1), k.reshape(B, S, -1), v.reshape(B, S, -1), B, S)
        outs = [o.reshape(T, -1)]
        ws = [p["wo"]]
    m = moe[i]
    x1, x1b, ids, wts = _proj_ln_route(x2d, outs, ws, ln1_g[i:i + 1], ln1_b[i:i + 1], m["wr"], m["br"])
    return x1, x1b, _moe_plan(ids, wts, T)


def _channel_mix(i, routed, prm):
    x1, x1b, plan = routed
    return _hier_moe_ln(x1, x1b, plan, prm["moe_w1"], prm["moe_w3"], prm["moe_w2"], i,
                        prm["ln2_g"][i:i + 1], prm["ln2_b"][i:i + 1])


def _after(x, token):
    x, _ = lax.optimization_barrier((x, token))
    return x


def _trunk_pair(xa, xb, prm):
    (Ba, S, D), Bb = xa.shape, xb.shape[0]
    a = xa.reshape(Ba * S, D)
    b = xb.reshape(Bb * S, D)
    routed_a = _mix_route(0, a, Ba, S, prm)
    for i in range(DEPTH):
        routed_b = _mix_route(i, _after(b, routed_a[2][0]), Bb, S, prm)
        a = _channel_mix(i, routed_a, prm)
        if i + 1 < DEPTH:
            routed_a = _mix_route(i + 1, _after(a, routed_b[2][0]), Ba, S, prm)
        b = _channel_mix(i, routed_b, prm)
    return a.reshape(Ba, S, D), b.reshape(Bb, S, D)


def kernel(x_prompt, x_sample, ev_w_in, ev_wa_f, ev_ba_f, ev_wa_b, ev_ba_b, ev_gla_norm, ev_w_out,
           od_w_in, od_q_norm, od_kv_norm, od_w_uq, od_w_ukv, od_w_out, ln1_g, ln1_b, ln2_g, ln2_b,
           moe_wg, moe_bg, moe_we, moe_be, moe_w1, moe_w3, moe_w2):
    S = x_prompt.shape[1]
    ev = [_prep_even(ev_w_in[j], ev_wa_f[j], ev_ba_f[j], ev_wa_b[j], ev_ba_b[j], ev_gla_norm[j], ev_w_out[j])
          for j in range(ev_w_in.shape[0])]
    od = [_prep_odd(od_w_in[j], od_q_norm[j], od_kv_norm[j], od_w_uq[j], od_w_ukv[j], od_w_out[j], S)
          for j in range(od_w_in.shape[0])]
    moe = [_prep_moe(moe_wg[i], moe_bg[i], moe_we[i], moe_be[i]) for i in range(DEPTH)]
    prm = dict(ev=ev, od=od, moe=moe, ln1_g=ln1_g, ln1_b=ln1_b, ln2_g=ln2_g, ln2_b=ln2_b,
               moe_w1=moe_w1, moe_w3=moe_w3, moe_w2=moe_w2)
    return _trunk_pair(x_prompt, x_sample, prm)
```

```python
import functools

import numpy as np
import jax
import jax.numpy as jnp
from jax import lax
from jax.experimental import pallas as pl
from jax.experimental.pallas import tpu as pltpu

F32 = jnp.float32
BF16 = jnp.bfloat16
HIGHEST = lax.Precision.HIGHEST

D_MODEL = 1024
DEPTH = 2
GLA_HEADS, GLA_DK, GLA_DV = 4, 64, 128
GLA_QK, GLA_VW = GLA_HEADS * GLA_DK, GLA_HEADS * GLA_DV
GLA_RANK, GLA_TAU, GLA_CHUNK = 16, 16.0, 64
DIL_HEADS, DIL_DH = 8, 64
DIL_W = DIL_HEADS * DIL_DH
DIL_PATTERNS = ((128, 1), (512, 4), (2048, 16))
DIL_RADIUS = 64
MASK_VALUE = -1e30
MLA_HEADS, MLA_NOPE, MLA_ROPE, MLA_V = 16, 64, 32, 64
MLA_Q_LORA, MLA_KV_LORA = 384, 128
ROPE_BASE = 10000.0
MOE_GROUPS, MOE_EPG, MOE_EXPERTS, MOE_FF = 4, 4, 16, 512
ALPHA = (2 * DEPTH) ** 0.25
LN_EPS = 1e-5
RMS_EPS = 1e-6
LOG2_E = 1.4426950408889634

LANES = 128
VMEM_LIMIT = 56 * 1024 * 1024
ROW_TILE = 512
MOE_TILE = 512
ATT_TQ = 512
ATT_TK = 512
DIL_UNROLL = 2
GLA_UNROLL = 2
GLA_TILE = 256
DIL_TQ = 256


def _params(*sem):
    return pltpu.CompilerParams(dimension_semantics=sem, vmem_limit_bytes=VMEM_LIMIT)


def _full(shape):
    n = len(shape)
    return pl.BlockSpec(shape, lambda *_: (0,) * n)


_TOKEN_SPEC = pl.BlockSpec(memory_space=pl.ANY)


def _mm(a, b):
    return jnp.dot(a, b, preferred_element_type=F32)


def _mm_nt(a, b):
    return lax.dot_general(a, b, (((1,), (1,)), ((), ())), preferred_element_type=F32)


def _mm_tn(a, b):
    return lax.dot_general(a, b, (((0,), (0,)), ((), ())), preferred_element_type=F32)


def _layer_norm(y, g, b):
    mu = jnp.mean(y, axis=-1, keepdims=True)
    yc = y - mu
    var = jnp.mean(yc * yc, axis=-1, keepdims=True)
    return yc * lax.rsqrt(var + LN_EPS) * g + b


def _log_sigmoid(z):
    return jnp.minimum(z, 0.0) - jnp.log(1.0 + jnp.exp(-jnp.abs(z)))


def _even_in_kernel(x_ref, wa_ref, wb_ref, wg_ref, tok_ref, oa_ref, ob_ref, og_ref):
    x = x_ref[...].astype(BF16)
    oa_ref[...] = _mm(x, wa_ref[...]).astype(oa_ref.dtype)
    ob_ref[...] = _mm(x, wb_ref[...])
    og_ref[...] = _mm(x, wg_ref[...])


def _even_in(x2d, wa, wb, wg, token):
    T = x2d.shape[0]
    tm = ROW_TILE
    return pl.pallas_call(
        _even_in_kernel,
        grid=(T // tm,),
        in_specs=[pl.BlockSpec((tm, D_MODEL), lambda i: (i, 0)),
                  _full(wa.shape), _full(wb.shape), _full(wg.shape), _TOKEN_SPEC],
        out_specs=[pl.BlockSpec((tm, wa.shape[1]), lambda i: (i, 0)),
                   pl.BlockSpec((tm, wb.shape[1]), lambda i: (i, 0)),
                   pl.BlockSpec((tm, wg.shape[1]), lambda i: (i, 0))],
        out_shape=[jax.ShapeDtypeStruct((T, wa.shape[1]), BF16),
                   jax.ShapeDtypeStruct((T, wb.shape[1]), F32),
                   jax.ShapeDtypeStruct((T, wg.shape[1]), F32)],
        compiler_params=_params("parallel"),
        name="even_in_proj",
    )(x2d, wa, wb, wg, token)


def _gla_kernel(q_ref, k_ref, v_ref, r_ref, gate_ref, wf_ref, bf_ref, wb_ref, bb_ref, ng_ref,
                o_ref, la_ref, qd_ref, ke_ref, vt_ref, tot_ref, of_ref, ob_ref, *, seq):
    C = GLA_CHUNK
    n_chunks = seq // C
    tile = min(GLA_TILE, seq)
    cpt = tile // C
    gate = gate_ref[...]
    gate_hi = gate.astype(BF16)
    gate_lo = (gate - gate_hi.astype(F32)).astype(BF16)
    for d, (w_ref, b_ref) in enumerate(((wf_ref, bf_ref), (wb_ref, bb_ref))):
        w = w_ref[...]
        w_hi = w.astype(BF16)
        w_lo = (w - w_hi.astype(F32)).astype(BF16)
        z = _mm(gate_hi, w_hi) + _mm(gate_hi, w_lo) + _mm(gate_lo, w_hi) + b_ref[...]
        la_ref[d] = _log_sigmoid(z) * (1.0 / GLA_TAU)

    trow = lax.broadcasted_iota(jnp.int32, (tile, tile), 0)
    tcol = lax.broadcasted_iota(jnp.int32, (tile, tile), 1)
    same_chunk = (trow // C) == (tcol // C)
    keep = (same_chunk & (trow >= tcol), same_chunk & (tcol >= trow))
    rmod = lax.broadcasted_iota(jnp.int32, (tile, LANES), 0) % C
    lane = lax.broadcasted_iota(jnp.int32, (1, LANES), 1)
    head_lane = (lane < GLA_DK, lane >= GLA_DK)
    srow = lax.broadcasted_iota(jnp.int32, (2 * GLA_DV, LANES), 0)
    scol = lax.broadcasted_iota(jnp.int32, (2 * GLA_DV, LANES), 1)
    diag = (srow < GLA_DV) == (scol < GLA_DK)
    scale = GLA_DK ** -0.5

    def chunk_scan(x, d):
        step = 1
        while step < C:
            if d == 0:
                x = x + jnp.where(rmod >= step, pltpu.roll(x, step, 0), 0.0)
            else:
                x = x + jnp.where(rmod < C - step, pltpu.roll(x, tile - step, 0), 0.0)
            step *= 2
        return x

    def intra(j, _):
        r0 = pl.multiple_of(j * tile, tile)
        rows = pl.ds(r0, tile)
        q = q_ref[rows, :].astype(F32) * scale
        k = k_ref[rows, :].astype(F32)
        v = v_ref[rows, :]
        vf = v.astype(F32)
        for c in range(cpt):
            vt_ref[j * cpt + c] = vf[c * C:(c + 1) * C, :].T.astype(BF16)
        for d, out_ref in enumerate((of_ref, ob_ref)):
            la = la_ref[d, rows, :]
            b = chunk_scan(la, d)
            tots = [jnp.sum(la[c * C:(c + 1) * C, :], axis=0, keepdims=True) for c in range(cpt)]
            for c in range(cpt):
                tot_ref[d, pl.ds(j * cpt + c, 1), :] = tots[c]
            tot = jnp.concatenate([jnp.broadcast_to(t, (C, LANES)) for t in tots], axis=0)
            qd = (q * jnp.exp(b)).astype(BF16)
            kd = (k * jnp.exp(-b)).astype(BF16)
            qd_ref[d, rows, :] = qd
            ke_ref[d, rows, :] = (k * jnp.exp(tot - b)).astype(BF16)
            parts = []
            for h in range(2):
                qh = jnp.where(head_lane[h], qd, jnp.zeros_like(qd))
                s = jnp.where(keep[d], _mm_nt(qh, kd), 0.0)
                parts.append(_mm(s.astype(BF16), v[:, h * GLA_DV:(h + 1) * GLA_DV]))
            out_ref[rows, :] = jnp.concatenate(parts, axis=1)
        return 0

    lax.fori_loop(0, seq // tile, intra, 0)

    def inter(i, carry):
        new = []
        for d, (state, out_ref) in enumerate(zip(carry, (of_ref, ob_ref))):
            c = i if d == 0 else n_chunks - 1 - i
            rows = pl.ds(pl.multiple_of(c * C, C), C)
            out_ref[rows, :] += _mm_nt(qd_ref[d, rows, :], state.astype(BF16))
            upd = jnp.where(diag, _mm(vt_ref[c], ke_ref[d, rows, :]), 0.0)
            new.append(jnp.exp(tot_ref[d, pl.ds(c, 1), :]) * state + upd)
        return tuple(new)

    zero = jnp.zeros((2 * GLA_DV, LANES), F32)
    lax.fori_loop(0, n_chunks, inter, (zero, zero), unroll=GLA_UNROLL)

    blk = 256

    def finish(j, _):
        rows = pl.ds(pl.multiple_of(j * blk, blk), blk)
        o = of_ref[rows, :] + ob_ref[rows, :]
        g = ng_ref[...]
        outs = []
        for h in range(2):
            oh = o[:, h * GLA_DV:(h + 1) * GLA_DV]
            ms = jnp.mean(oh * oh, axis=-1, keepdims=True)
            outs.append(oh * lax.rsqrt(ms + RMS_EPS) * g[:, h * GLA_DV:(h + 1) * GLA_DV])
        r = r_ref[rows, :].astype(F32)
        o_ref[rows, :] = (jnp.concatenate(outs, axis=1) * (r * jax.nn.sigmoid(r))).astype(o_ref.dtype)
        return 0

    lax.fori_loop(0, seq // blk, finish, 0)


def _gla(h_a, gate, wf, bf, wb, bb, norm_g, B, S):
    pairs = GLA_HEADS // 2
    kq, kv = 2 * GLA_DK, 2 * GLA_DV
    sq = pl.BlockSpec((None, S, kq), lambda b, g: (b, 0, g))
    sk = pl.BlockSpec((None, S, kq), lambda b, g: (b, 0, pairs + g))
    sv = pl.BlockSpec((None, S, kv), lambda b, g: (b, 0, (2 * GLA_QK) // kv + g))
    sr = pl.BlockSpec((None, S, kv), lambda b, g: (b, 0, (2 * GLA_QK + GLA_VW) // kv + g))
    sg = pl.BlockSpec((None, S, LANES), lambda b, g: (b, 0, 0))
    sw = pl.BlockSpec((LANES, kq), lambda b, g: (0, g))
    sb = pl.BlockSpec((1, kq), lambda b, g: (0, g))
    sn = pl.BlockSpec((1, kv), lambda b, g: (0, g))
    return pl.pallas_call(
        functools.partial(_gla_kernel, seq=S),
        grid=(B, pairs),
        in_specs=[sq, sk, sv, sr, sg, sw, sb, sw, sb, sn],
        out_specs=pl.BlockSpec((None, S, kv), lambda b, g: (b, 0, g)),
        out_shape=jax.ShapeDtypeStruct((B, S, GLA_VW), BF16),
        scratch_shapes=[pltpu.VMEM((2, S, kq), F32),
                        pltpu.VMEM((2, S, kq), BF16),
                        pltpu.VMEM((2, S, kq), BF16),
                        pltpu.VMEM((S // GLA_CHUNK, kv, GLA_CHUNK), BF16),
                        pltpu.VMEM((2, S // GLA_CHUNK, kq), F32),
                        pltpu.VMEM((S, kv), F32), pltpu.VMEM((S, kv), F32)],
        compiler_params=_params("parallel", "parallel"),
        name="gla_mixer",
    )(h_a, h_a, h_a, h_a, gate, wf, bf, wb, bb, norm_g)


def _dil_kernel(q_ref, k_ref, v_ref, o_ref, m_ref, l_ref, acc_ref, bias_ref, s_ref, *, seq):
    R = DIL_RADIUS
    pair = pl.program_id(1)
    lane = lax.broadcasted_iota(jnp.int32, (1, LANES), 1)
    first_head = lane < DIL_DH
    scale = DIL_DH ** -0.5

    for p_idx, (window, dil) in enumerate(DIL_PATTERNS):
        assert window // (2 * dil) == R
        length = seq // dil
        tq = min(DIL_TQ, length)
        nk = tq + 2 * R
        tpr = length // tq
        shift = tpr.bit_length() - 1
        assert tpr == 1 << shift and tq % R == 0
        qi = lax.broadcasted_iota(jnp.int32, (tq, nk), 0)
        kj = lax.broadcasted_iota(jnp.int32, (tq, nk), 1)
        dist = jnp.abs(kj - R - qi)
        for h in range(2):
            slope = 1.0 / jnp.left_shift(jnp.ones((tq, nk), jnp.int32), 2 * pair + h + 1).astype(F32)
            bias_ref[h, :tq, :nk] = jnp.where(dist <= R, -(dil * dist).astype(F32) * slope, MASK_VALUE)
        ones = jnp.ones((nk, LANES), BF16)

        def rows_at(start, n, dil=dil):
            return pl.ds(start, n, stride=dil) if dil > 1 else pl.ds(pl.multiple_of(start, R), n)

        n_tiles = seq // tq
        assert n_tiles % 2 == 0

        def tile_pos(idx, dil=dil, tq=tq, tpr=tpr, shift=shift):
            res = lax.shift_right_logical(idx, shift)
            t = jnp.bitwise_and(idx, tpr - 1)
            start = res + dil * tq * t
            has_prev = t > 0
            has_next = t < tpr - 1
            prev = jnp.where(has_prev, start - dil * R, start)
            nxt = jnp.where(has_next, start + dil * tq, start)
            return start, prev, nxt, has_prev, has_next

        def scores(idx, slot, tq=tq, nk=nk, rows_at=rows_at, tile_pos=tile_pos):
            start, prev, nxt, _, _ = tile_pos(idx)
            q = (q_ref[rows_at(start, tq), :] * scale).astype(BF16)
            kcat = jnp.concatenate([k_ref[rows_at(prev, R), :], k_ref[rows_at(start, tq), :],
                                    k_ref[rows_at(nxt, R), :]], axis=0).astype(BF16)
            for h in range(2):
                qh = jnp.where(first_head if h == 0 else ~first_head, q, jnp.zeros_like(q))
                s_ref[slot, h, :tq, :nk] = _mm_nt(qh, kcat)

        def values(idx, slot, tq=tq, nk=nk, p_idx=p_idx, rows_at=rows_at, kj=kj, ones=ones,
                   tile_pos=tile_pos):
            start, prev, nxt, has_prev, has_next = tile_pos(idx)
            vcat = jnp.concatenate([v_ref[rows_at(prev, R), :], v_ref[rows_at(start, tq), :],
                                    v_ref[rows_at(nxt, R), :]], axis=0).astype(BF16)
            vones = jnp.concatenate([vcat, ones], axis=1)
            in_seq = ((kj >= R) | has_prev) & ((kj < R + tq) | has_next)
            ms, pvs = [], []
            for h in range(2):
                s = jnp.where(in_seq, s_ref[slot, h, :tq, :nk] + bias_ref[h, :tq, :nk], MASK_VALUE)
                m = jnp.max(s, axis=-1, keepdims=True)
                ms.append(m)
                pvs.append(_mm(jnp.exp(s - m).astype(BF16), vones))
            out_rows = rows_at(start, tq)
            m_ref[p_idx, out_rows, :] = jnp.where(first_head, ms[0], ms[1])
            l_ref[p_idx, out_rows, :] = jnp.where(first_head, pvs[0][:, LANES:], pvs[1][:, LANES:])
            acc_ref[p_idx, out_rows, :] = jnp.where(first_head, pvs[0][:, :LANES], pvs[1][:, :LANES])

        scores(0, 0)

        def body(i2, _, n_tiles=n_tiles, scores=scores, values=values):
            for u in range(2):
                idx = 2 * i2 + u
                scores(jnp.minimum(idx + 1, n_tiles - 1), 1 - u)
                values(idx, u)
            return 0

        lax.fori_loop(0, n_tiles // 2, body, 0, unroll=DIL_UNROLL)

    blk = 256
    n_pat = len(DIL_PATTERNS)

    def finish(j, _):
        rows = pl.ds(pl.multiple_of(j * blk, blk), blk)
        ms = [m_ref[p, rows, :] for p in range(n_pat)]
        m = functools.reduce(jnp.maximum, ms)
        ws = [jnp.exp(x - m) for x in ms]
        l = functools.reduce(lambda a, b: a + b, [w * l_ref[p, rows, :] for p, w in enumerate(ws)])
        acc = functools.reduce(lambda a, b: a + b, [w * acc_ref[p, rows, :] for p, w in enumerate(ws)])
        o_ref[rows, :] = (acc / l).astype(o_ref.dtype)
        return 0

    lax.fori_loop(0, seq // blk, finish, 0)


def _dilated(h_b, B, S):
    pairs = DIL_HEADS // 2
    sq = pl.BlockSpec((None, S, LANES), lambda b, g: (b, 0, g))
    sk = pl.BlockSpec((None, S, LANES), lambda b, g: (b, 0, pairs + g))
    sv = pl.BlockSpec((None, S, LANES), lambda b, g: (b, 0, 2 * pairs + g))
    return pl.pallas_call(
        functools.partial(_dil_kernel, seq=S),
        grid=(B, pairs),
        in_specs=[sq, sk, sv],
        out_specs=pl.BlockSpec((None, S, LANES), lambda b, g: (b, 0, g)),
        out_shape=jax.ShapeDtypeStruct((B, S, DIL_W), BF16),
        scratch_shapes=[pltpu.VMEM((len(DIL_PATTERNS), S, LANES), F32)] * 3
                       + [pltpu.VMEM((2, DIL_TQ, DIL_TQ + 2 * DIL_RADIUS), F32),
                          pltpu.VMEM((2, 2, DIL_TQ, DIL_TQ + 2 * DIL_RADIUS), F32)],
        compiler_params=_params("parallel", "parallel"),
        name="dilated_mixer",
    )(h_b, h_b, h_b)


def _route(lt):
    g = [lt[i:i + 1, :] for i in range(MOE_GROUPS)]
    gmax = functools.reduce(jnp.maximum, g)
    gexp = [jnp.exp(x - gmax) for x in g]
    gsum = functools.reduce(lambda a, b: a + b, gexp)
    gprob = [x / gsum for x in gexp]
    g_val, g_idx = gprob[0], jnp.zeros_like(gprob[0], dtype=jnp.int32)
    for i in range(1, MOE_GROUPS):
        better = gprob[i] > g_val
        g_idx = jnp.where(better, i, g_idx)
        g_val = jnp.where(better, gprob[i], g_val)
    e = []
    for j in range(MOE_EPG):
        x = lt[MOE_GROUPS + j:MOE_GROUPS + j + 1, :]
        for gi in range(1, MOE_GROUPS):
            r0 = MOE_GROUPS + gi * MOE_EPG + j
            x = jnp.where(g_idx == gi, lt[r0:r0 + 1, :], x)
        e.append(x)
    emax = functools.reduce(jnp.maximum, e)
    eexp = [jnp.exp(x - emax) for x in e]
    esum = functools.reduce(lambda a, b: a + b, eexp)
    eprob = [x / esum for x in eexp]
    v1, i1 = eprob[0], jnp.zeros_like(g_idx)
    for j in range(1, MOE_EPG):
        better = eprob[j] > v1
        i1 = jnp.where(better, j, i1)
        v1 = jnp.where(better, eprob[j], v1)
    v2, i2 = jnp.full_like(v1, -1.0), jnp.zeros_like(g_idx)
    for j in range(MOE_EPG):
        cand = jnp.where(i1 == j, -1.0, eprob[j])
        better = cand > v2
        i2 = jnp.where(better, j, i2)
        v2 = jnp.where(better, cand, v2)
    den = v1 + v2
    ids = jnp.concatenate([g_idx * MOE_EPG + i1, g_idx * MOE_EPG + i2], axis=0)
    wts = jnp.concatenate([g_val * (v1 / den), g_val * (v2 / den)], axis=0)
    return ids, wts


def _proj_ln_route_kernel(*refs, n_in):
    x_ref = refs[0]
    o_refs = refs[1:1 + n_in]
    w_refs = refs[1 + n_in:1 + 2 * n_in]
    g_ref, b_ref, wr_ref, br_ref, x1_ref, x1b_ref, ids_ref, wts_ref = refs[1 + 2 * n_in:]
    m = _mm(o_refs[0][...], w_refs[0][...])
    for o_ref, w_ref in zip(o_refs[1:], w_refs[1:]):
        m = m + _mm(o_ref[...], w_ref[...])
    x1 = _layer_norm(ALPHA * x_ref[...] + m, g_ref[...], b_ref[...])
    x1_ref[...] = x1
    x_hi = x1.astype(BF16)
    x1b_ref[...] = x_hi
    x_lo = (x1 - x_hi.astype(F32)).astype(BF16)
    wr = wr_ref[...]
    w_hi = wr.astype(BF16)
    w_lo = (wr - w_hi.astype(F32)).astype(BF16)
    lt = _mm_nt(w_hi, x_hi) + _mm_nt(w_hi, x_lo) + _mm_nt(w_lo, x_hi) + br_ref[...]
    ids, wts = _route(lt)
    ids_ref[...] = ids
    wts_ref[...] = wts


def _proj_ln_route(x2d, outs, ws, ln_g, ln_b, wr, br):
    T = x2d.shape[0]
    tm = ROW_TILE
    n_in = len(outs)
    row = lambda w: pl.BlockSpec((tm, w), lambda i: (i, 0))
    tok = pl.BlockSpec((2, tm), lambda i: (0, i))
    return pl.pallas_call(
        functools.partial(_proj_ln_route_kernel, n_in=n_in),
        grid=(T // tm,),
        in_specs=[row(D_MODEL)] + [row(o.shape[1]) for o in outs] + [_full(w.shape) for w in ws]
                 + [_full(ln_g.shape), _full(ln_b.shape), _full(wr.shape), _full(br.shape)],
        out_specs=[row(D_MODEL), row(D_MODEL), tok, tok],
        out_shape=[jax.ShapeDtypeStruct((T, D_MODEL), F32), jax.ShapeDtypeStruct((T, D_MODEL), BF16),
                   jax.ShapeDtypeStruct((2, T), jnp.int32), jax.ShapeDtypeStruct((2, T), F32)],
        compiler_params=_params("parallel"),
        name="proj_ln_route",
    )(x2d, *outs, *ws, ln_g, ln_b, wr, br)


def _moe_kernel(te_ref, nu_ref, xs_ref, w1_ref, w3_ref, w2_ref, rw_ref, tok_ref, ys_ref, w13_s, w2_s):
    i = pl.program_id(0)

    @pl.when((i == 0) | (te_ref[i] != te_ref[jnp.maximum(i - 1, 0)]))
    def _():
        w13_s[:, :MOE_FF] = w1_ref[...].astype(BF16)
        w13_s[:, MOE_FF:] = w3_ref[...].astype(BF16)
        w2_s[...] = w2_ref[...].astype(BF16)

    @pl.when(i < nu_ref[0])
    def _():
        h = _mm(xs_ref[...], w13_s[...])
        h1, h3 = h[:, :MOE_FF], h[:, MOE_FF:]
        hidden = (h1 * jax.nn.sigmoid(h1)) * h3
        y = _mm(hidden.astype(BF16), w2_s[...])
        ys_ref[...] = (y * rw_ref[...]).astype(ys_ref.dtype)

    @pl.when(i >= nu_ref[0])
    def _():
        ys_ref[...] = jnp.zeros_like(ys_ref)


def _moe_experts(xs, w1, w3, w2, layer, row_w, tile_expert, n_used, token):
    P = xs.shape[0]
    tm = MOE_TILE
    grid_spec = pltpu.PrefetchScalarGridSpec(
        num_scalar_prefetch=2,
        grid=(P // tm,),
        in_specs=[pl.BlockSpec((tm, D_MODEL), lambda i, te, nu: (i, 0)),
                  pl.BlockSpec((None, None, D_MODEL, MOE_FF), lambda i, te, nu: (layer, te[i], 0, 0)),
                  pl.BlockSpec((None, None, D_MODEL, MOE_FF), lambda i, te, nu: (layer, te[i], 0, 0)),
                  pl.BlockSpec((None, None, MOE_FF, D_MODEL), lambda i, te, nu: (layer, te[i], 0, 0)),
                  pl.BlockSpec((tm, 1), lambda i, te, nu: (i, 0)), _TOKEN_SPEC],
        out_specs=pl.BlockSpec((tm, D_MODEL), lambda i, te, nu: (i, 0)),
        scratch_shapes=[pltpu.VMEM((D_MODEL, 2 * MOE_FF), BF16), pltpu.VMEM((MOE_FF, D_MODEL), BF16)],
    )
    return pl.pallas_call(
        _moe_kernel,
        grid_spec=grid_spec,
        out_shape=jax.ShapeDtypeStruct((P, D_MODEL), BF16),
        compiler_params=_params("arbitrary"),
        name="moe_experts",
    )(tile_expert, n_used, xs, w1, w3, w2, row_w, token)


def _moe_plan(ids, wts, T):
    tm = MOE_TILE
    n_assign = 2 * T
    e_flat = ids.reshape(n_assign)
    onehot = (e_flat[:, None] == jnp.arange(MOE_EXPERTS, dtype=jnp.int32)[None, :]).astype(jnp.int32)
    csum = jnp.cumsum(onehot, axis=0)
    counts = csum[-1]
    padded = ((counts + tm - 1) // tm) * tm
    ends = jnp.cumsum(padded)
    offs = ends - padded
    pos = jnp.sum(onehot * (csum - onehot + offs[None, :]), axis=1)
    P = n_assign + MOE_EXPERTS * tm
    n_tiles = P // tm
    tile_start = jnp.arange(n_tiles, dtype=jnp.int32) * tm
    tile_expert = jnp.minimum(jnp.sum((tile_start[:, None] >= ends[None, :]).astype(jnp.int32), axis=1),
                              MOE_EXPERTS - 1).astype(jnp.int32)
    n_used = (ends[-1:] // tm).astype(jnp.int32)
    tok = jnp.arange(n_assign, dtype=jnp.int32) % T
    _, tok_sorted, w_sorted = lax.sort((e_flat, tok, wts.reshape(n_assign)), num_keys=1, is_stable=True)
    t_onehot = (tile_expert[:, None] == jnp.arange(MOE_EXPERTS, dtype=jnp.int32)[None, :]).astype(jnp.int32)
    t_rank0 = tile_start - jnp.sum(t_onehot * offs[None, :], axis=1)
    t_count = jnp.sum(t_onehot * counts[None, :], axis=1)
    t_first = jnp.sum(t_onehot * (jnp.cumsum(counts) - counts)[None, :], axis=1)
    r = t_rank0[:, None] + jnp.arange(tm, dtype=jnp.int32)[None, :]
    valid = (r < t_count[:, None]).reshape(P)
    u = jnp.clip(t_first[:, None] + r, 0, n_assign - 1).reshape(P)
    src_tok = jnp.where(valid, tok_sorted.at[u].get(mode="promise_in_bounds"), 0)
    row_w = jnp.where(valid, w_sorted.at[u].get(mode="promise_in_bounds"), 0.0)
    return src_tok, row_w.reshape(P, 1), tile_expert, n_used, pos


def _combine_ln_kernel(x_ref, y0_ref, y1_ref, g_ref, b_ref, tok_ref, o_ref):
    y = ALPHA * x_ref[...] + y0_ref[...].astype(F32) + y1_ref[...].astype(F32)
    o_ref[...] = _layer_norm(y, g_ref[...], b_ref[...])


def _combine_ln(x2d, y0, y1, ln_g, ln_b, token):
    T = x2d.shape[0]
    tm = ROW_TILE
    row = pl.BlockSpec((tm, D_MODEL), lambda i: (i, 0))
    return pl.pallas_call(
        _combine_ln_kernel,
        grid=(T // tm,),
        in_specs=[row, row, row, _full(ln_g.shape), _full(ln_b.shape), _TOKEN_SPEC],
        out_specs=row,
        out_shape=jax.ShapeDtypeStruct((T, D_MODEL), F32),
        compiler_params=_params("parallel"),
        name="combine_ln",
    )(x2d, y0, y1, ln_g, ln_b, token)


def _experts_stage(i, routed, prm, token):
    _, x1b, (src_tok, row_w, tile_expert, n_used, _) = routed
    xs = x1b.at[src_tok].get(mode="promise_in_bounds")
    return _moe_experts(xs, prm["moe_w1"], prm["moe_w3"], prm["moe_w2"], i, row_w, tile_expert, n_used, token)


def _combine_stage(i, routed, ys, prm, token):
    x1, _, plan = routed
    T = x1.shape[0]
    pos = plan[4]
    y0 = ys.at[pos[:T]].get(mode="promise_in_bounds")
    y1 = ys.at[pos[T:]].get(mode="promise_in_bounds")
    return _combine_ln(x1, y0, y1, prm["ln2_g"][i:i + 1], prm["ln2_b"][i:i + 1], token)


def _rope_block(x, c, sa, sb):
    return x * c + pltpu.roll(x, LANES - 32, 1) * sa + pltpu.roll(x, 32, 1) * sb


def _mla_in_kernel(x_ref, win_ref, qg_ref, kvg_ref, wq_ref, wk_ref, wv_ref, c_ref, sa_ref, sb_ref,
                   tok_ref, q_ref, k_ref, v_ref):
    x = x_ref[...].astype(BF16)
    h = _mm(x, win_ref[...])
    c_q = h[:, :MLA_Q_LORA]
    c_kv = h[:, MLA_Q_LORA:MLA_Q_LORA + MLA_KV_LORA]
    k_rope = h[:, MLA_Q_LORA + MLA_KV_LORA:]
    cqn = c_q * lax.rsqrt(jnp.mean(c_q * c_q, axis=-1, keepdims=True) + RMS_EPS) * qg_ref[...]
    ckn = c_kv * lax.rsqrt(jnp.mean(c_kv * c_kv, axis=-1, keepdims=True) + RMS_EPS) * kvg_ref[...]
    cqn, ckn = cqn.astype(BF16), ckn.astype(BF16)
    scale = (MLA_NOPE + MLA_ROPE) ** -0.5 * LOG2_E
    q = _mm(cqn, wq_ref[...]) * scale
    kn = _mm(ckn, wk_ref[...])
    v = _mm(ckn, wv_ref[...]).astype(v_ref.dtype)
    c, sa, sb = c_ref[...], sa_ref[...], sb_ref[...]
    kr = _rope_block(k_rope, c, sa, sb).astype(k_ref.dtype)
    ones = jnp.ones((x.shape[0], LANES), v_ref.dtype)
    for g in range(MLA_HEADS // 2):
        lo = 2 * LANES * g
        v_ref[:, lo:lo + LANES] = v[:, LANES * g:LANES * (g + 1)]
        v_ref[:, lo + LANES:lo + 2 * LANES] = ones
        q_ref[:, lo:lo + LANES] = q[:, lo:lo + LANES].astype(q_ref.dtype)
        q_ref[:, lo + LANES:lo + 2 * LANES] = _rope_block(q[:, lo + LANES:lo + 2 * LANES], c, sa, sb
                                                          ).astype(q_ref.dtype)
        k_ref[:, lo:lo + LANES] = kn[:, LANES * g:LANES * (g + 1)].astype(k_ref.dtype)
        k_ref[:, lo + LANES:lo + 2 * LANES] = kr


def _mla_in(x2d, win, qg, kvg, wq, wk, wv, c, sa, sb, S, token):
    T = x2d.shape[0]
    tm = ROW_TILE
    row = lambda w: pl.BlockSpec((tm, w), lambda i: (i, 0))
    n_s = S // tm
    tab = pl.BlockSpec((tm, LANES), lambda i: (i % n_s, 0))
    pairs = MLA_HEADS // 2
    return pl.pallas_call(
        _mla_in_kernel,
        grid=(T // tm,),
        in_specs=[row(D_MODEL), _full(win.shape), _full(qg.shape), _full(kvg.shape), _full(wq.shape),
                  _full(wk.shape), _full(wv.shape), tab, tab, tab, _TOKEN_SPEC],
        out_specs=[row(pairs * 2 * LANES)] * 3,
        out_shape=[jax.ShapeDtypeStruct((T, pairs * 2 * LANES), BF16)] * 3,
        compiler_params=_params("parallel"),
        name="mla_in_proj",
    )(x2d, win, qg, kvg, wq, wk, wv, c, sa, sb, token)


def _mla_attn_kernel(q_ref, k_ref, v_ref, o_ref, sa_ref, sb_ref, *, seq):
    lane2 = lax.broadcasted_iota(jnp.int32, (1, 2 * LANES), 1)
    rl = lane2 - LANES
    in_a = (lane2 < MLA_NOPE) | ((rl >= 0) & (rl < 16)) | ((rl >= 32) & (rl < 48))
    in_b = ((lane2 >= MLA_NOPE) & (lane2 < LANES)) | ((rl >= 16) & (rl < 32)) | ((rl >= 48) & (rl < 64))
    lane = lax.broadcasted_iota(jnp.int32, (1, LANES), 1)
    q = q_ref[...]
    tq = q.shape[0]
    tk = min(ATT_TK, seq)
    n_k = seq // tk
    q_heads = (jnp.where(in_a, q, jnp.zeros_like(q)), jnp.where(in_b, q, jnp.zeros_like(q)))
    s_refs = (sa_ref, sb_ref)

    def score_chunk(h, j, m):
        s = _mm_nt(q_heads[h], k_ref[j * tk:(j + 1) * tk, :])
        s_refs[h][:, j * tk:(j + 1) * tk] = s
        for c in range(tk // LANES):
            m = jnp.maximum(m, s[:, c * LANES:(c + 1) * LANES])
        return m

    def value_chunk(h, j, m_row, acc):
        p = jnp.exp2(s_refs[h][:, j * tk:(j + 1) * tk] - m_row).astype(BF16)
        return acc + _mm(p, v_ref[j * tk:(j + 1) * tk, :])

    neg = jnp.full((tq, LANES), -jnp.inf, F32)
    m = neg
    for j in range(n_k):
        m = score_chunk(0, j, m)
    m_a = jnp.max(m, axis=-1, keepdims=True)
    acc_a = jnp.zeros((tq, 2 * LANES), F32)
    m = neg
    for j in range(n_k):
        acc_a = value_chunk(0, j, m_a, acc_a)
        m = score_chunk(1, j, m)
    m_b = jnp.max(m, axis=-1, keepdims=True)
    acc_b = jnp.zeros((tq, 2 * LANES), F32)
    for j in range(n_k):
        acc_b = value_chunk(1, j, m_b, acc_b)
    out_a = acc_a[:, :LANES] / acc_a[:, LANES:]
    out_b = acc_b[:, :LANES] / acc_b[:, LANES:]
    o_ref[...] = jnp.where(lane < MLA_V, out_a, out_b).astype(o_ref.dtype)


def _mla_attn(q, k, v, B, S):
    pairs = MLA_HEADS // 2
    tq = min(ATT_TQ, S)
    return pl.pallas_call(
        functools.partial(_mla_attn_kernel, seq=S),
        grid=(B, pairs, S // tq),
        in_specs=[pl.BlockSpec((None, tq, 2 * LANES), lambda b, g, i: (b, i, g)),
                  pl.BlockSpec((None, S, 2 * LANES), lambda b, g, i: (b, 0, g)),
                  pl.BlockSpec((None, S, 2 * LANES), lambda b, g, i: (b, 0, g))],
        out_specs=pl.BlockSpec((None, tq, LANES), lambda b, g, i: (b, i, g)),
        out_shape=jax.ShapeDtypeStruct((B, S, pairs * LANES), BF16),
        scratch_shapes=[pltpu.VMEM((tq, S), F32), pltpu.VMEM((tq, S), F32)],
        compiler_params=_params("parallel", "parallel", "arbitrary"),
        name="mla_attention",
    )(q, k, v)


def _prep_even(w_in, wa_f, ba_f, wa_b, ba_b, norm_g, w_out):
    o_q, o_k, o_v, o_r = 0, GLA_QK, 2 * GLA_QK, 2 * GLA_QK + GLA_VW
    o_af = o_r + GLA_VW
    o_ab = o_af + GLA_RANK
    o_qb = o_ab + GLA_RANK
    wa = w_in[:, :o_af].astype(BF16)
    wb = w_in[:, o_qb:].astype(BF16)
    wg = jnp.zeros((D_MODEL, LANES), F32).at[:, :2 * GLA_RANK].set(w_in[:, o_af:o_qb]).astype(BF16)
    wf = jnp.zeros((LANES, GLA_QK), F32).at[:GLA_RANK].set(wa_f)
    wb_gate = jnp.zeros((LANES, GLA_QK), F32).at[GLA_RANK:2 * GLA_RANK].set(wa_b)
    return dict(wa=wa, wb=wb, wg=wg, wf=wf, bf=ba_f.reshape(1, -1), wb_gate=wb_gate,
                bb=ba_b.reshape(1, -1), norm_g=norm_g.reshape(1, -1),
                wo_a=w_out[:GLA_VW].astype(BF16), wo_b=w_out[GLA_VW:].astype(BF16))


def _prep_odd(w_in, q_norm, kv_norm, w_uq, w_ukv, w_out, S):
    half = MLA_ROPE // 2
    pairs = MLA_HEADS // 2
    kr = w_in[:, MLA_Q_LORA + MLA_KV_LORA:]
    kr_rep = jnp.concatenate([kr[:, :half], kr[:, :half], kr[:, half:], kr[:, half:],
                              jnp.zeros((D_MODEL, LANES - 2 * MLA_ROPE), F32)], axis=1)
    win = jnp.concatenate([w_in[:, :MLA_Q_LORA + MLA_KV_LORA], kr_rep], axis=1).astype(BF16)
    uq = w_uq.reshape(MLA_Q_LORA, MLA_HEADS, MLA_NOPE + MLA_ROPE)
    nope = uq[:, :, :MLA_NOPE].reshape(MLA_Q_LORA, pairs, 2 * MLA_NOPE)
    r1 = uq[:, :, MLA_NOPE:MLA_NOPE + half].reshape(MLA_Q_LORA, pairs, 2 * half)
    r2 = uq[:, :, MLA_NOPE + half:].reshape(MLA_Q_LORA, pairs, 2 * half)
    pad = jnp.zeros((MLA_Q_LORA, pairs, LANES - 2 * MLA_ROPE), F32)
    wq = jnp.concatenate([nope, r1, r2, pad], axis=2).reshape(MLA_Q_LORA, pairs * 2 * LANES).astype(BF16)
    ukv = w_ukv.reshape(MLA_KV_LORA, MLA_HEADS, MLA_NOPE + MLA_V)
    wk = ukv[:, :, :MLA_NOPE].reshape(MLA_KV_LORA, MLA_HEADS * MLA_NOPE).astype(BF16)
    wv = ukv[:, :, MLA_NOPE:].reshape(MLA_KV_LORA, MLA_HEADS * MLA_V).astype(BF16)
    inv = 1.0 / (ROPE_BASE ** (jnp.arange(0, MLA_ROPE, 2, dtype=F32) / MLA_ROPE))
    ang = jnp.arange(S, dtype=F32)[:, None] * inv[None, :]
    cos, sin = jnp.cos(ang), jnp.sin(ang)
    z16 = jnp.zeros((S, half), F32)
    z64 = jnp.zeros((S, LANES - 2 * MLA_ROPE), F32)
    c = jnp.concatenate([cos, cos, cos, cos, z64], axis=1)
    sa = jnp.concatenate([-sin, -sin, z16, z16, z64], axis=1)
    sb = jnp.concatenate([z16, z16, sin, sin, z64], axis=1)
    return dict(win=win, qg=q_norm.reshape(1, -1), kvg=kv_norm.reshape(1, -1), wq=wq, wk=wk, wv=wv,
                c=c, sa=sa, sb=sb, wo=w_out.astype(BF16))


def _prep_moe(wg, bg, we, be):
    wr = jnp.zeros((32, D_MODEL), F32).at[:MOE_GROUPS].set(wg.T).at[MOE_GROUPS:MOE_GROUPS + MOE_EXPERTS].set(we.T)
    br = jnp.zeros((32, 1), F32).at[:MOE_GROUPS, 0].set(bg).at[MOE_GROUPS:MOE_GROUPS + MOE_EXPERTS, 0].set(be)
    return dict(wr=wr, br=br)


def _mix_route(i, x2d, B, S, prm, token):
    ev, od, moe, ln1_g, ln1_b = prm["ev"], prm["od"], prm["moe"], prm["ln1_g"], prm["ln1_b"]
    T = B * S
    if i % 2 == 0:
        p = ev[i // 2]
        h_a, h_b, gate = _even_in(x2d, p["wa"], p["wb"], p["wg"], token)
        o_a = _gla(h_a.reshape(B, S, -1), gate.reshape(B, S, -1), p["wf"], p["bf"], p["wb_gate"],
                   p["bb"], p["norm_g"], B, S)
        o_b = _dilated(h_b.reshape(B, S, -1), B, S)
        outs = [o_a.reshape(T, -1), o_b.reshape(T, -1)]
        ws = [p["wo_a"], p["wo_b"]]
    else:
        p = od[i // 2]
        q, k, v = _mla_in(x2d, p["win"], p["qg"], p["kvg"], p["wq"], p["wk"], p["wv"],
                          p["c"], p["sa"], p["sb"], S, token)
        o = _mla_attn(q.reshape(B, S, -1), k.reshape(B, S, -1), v.reshape(B, S, -1), B, S)
        outs = [o.reshape(T, -1)]
        ws = [p["wo"]]
    m = moe[i]
    x1, x1b, ids, wts = _proj_ln_route(x2d, outs, ws, ln1_g[i:i + 1], ln1_b[i:i + 1], m["wr"], m["br"])
    return x1, x1b, _moe_plan(ids, wts, T)


def _plan_token(routed):
    return routed[2][3]


def _rows_token(rows):
    return rows[:8, :LANES]


def _trunk_pair(xa, xb, prm):
    (Ba, S, D), Bb = xa.shape, xb.shape[0]
    a = xa.reshape(Ba * S, D)
    b = xb.reshape(Bb * S, D)
    tok = jnp.zeros((1,), jnp.int32)
    for i in range(DEPTH):
        routed_a = _mix_route(i, a, Ba, S, prm, tok)
        if i > 0:
            b = _combine_stage(i - 1, routed_b, ys_b, prm, _plan_token(routed_a))
        routed_b = _mix_route(i, b, Bb, S, prm, _plan_token(routed_a))
        ys_a = _experts_stage(i, routed_a, prm, _plan_token(routed_b))
        ys_b = _experts_stage(i, routed_b, prm, _rows_token(ys_a))
        a = _combine_stage(i, routed_a, ys_a, prm, _rows_token(ys_b))
        tok = _rows_token(a)
    b = _combine_stage(DEPTH - 1, routed_b, ys_b, prm, tok)
    return a.reshape(Ba, S, D), b.reshape(Bb, S, D)


def kernel(x_prompt, x_sample, ev_w_in, ev_wa_f, ev_ba_f, ev_wa_b, ev_ba_b, ev_gla_norm, ev_w_out,
           od_w_in, od_q_norm, od_kv_norm, od_w_uq, od_w_ukv, od_w_out, ln1_g, ln1_b, ln2_g, ln2_b,
           moe_wg, moe_bg, moe_we, moe_be, moe_w1, moe_w3, moe_w2):
    S = x_prompt.shape[1]
    ev = [_prep_even(ev_w_in[j], ev_wa_f[j], ev_ba_f[j], ev_wa_b[j], ev_ba_b[j], ev_gla_norm[j], ev_w_out[j])
          for j in range(ev_w_in.shape[0])]
    od = [_prep_odd(od_w_in[j], od_q_norm[j], od_kv_norm[j], od_w_uq[j], od_w_ukv[j], od_w_out[j], S)
          for j in range(od_w_in.shape[0])]
    moe = [_prep_moe(moe_wg[i], moe_bg[i], moe_we[i], moe_be[i]) for i in range(DEPTH)]
    prm = dict(ev=ev, od=od, moe=moe, ln1_g=ln1_g, ln1_b=ln1_b, ln2_g=ln2_g, ln2_b=ln2_b,
               moe_w1=moe_w1, moe_w3=moe_w3, moe_w2=moe_w2)
    return _trunk_pair(x_prompt, x_sample, prm)
```

```python
import functools

import numpy as np
import jax
import jax.numpy as jnp
from jax import lax
from jax.experimental import pallas as pl
from jax.experimental.pallas import tpu as pltpu

F32 = jnp.float32
BF16 = jnp.bfloat16
HIGHEST = lax.Precision.HIGHEST

D_MODEL = 1024
DEPTH = 2
GLA_HEADS, GLA_DK, GLA_DV = 4, 64, 128
GLA_QK, GLA_VW = GLA_HEADS * GLA_DK, GLA_HEADS * GLA_DV
GLA_RANK, GLA_TAU, GLA_CHUNK = 16, 16.0, 64
DIL_HEADS, DIL_DH = 8, 64
DIL_W = DIL_HEADS * DIL_DH
DIL_PATTERNS = ((128, 1), (512, 4), (2048, 16))
DIL_RADIUS = 64
MASK_VALUE = -1e30
MLA_HEADS, MLA_NOPE, MLA_ROPE, MLA_V = 16, 64, 32, 64
MLA_Q_LORA, MLA_KV_LORA = 384, 128
ROPE_BASE = 10000.0
MOE_GROUPS, MOE_EPG, MOE_EXPERTS, MOE_FF = 4, 4, 16, 512
ALPHA = (2 * DEPTH) ** 0.25
LN_EPS = 1e-5
RMS_EPS = 1e-6
LOG2_E = 1.4426950408889634

LANES = 128
VMEM_LIMIT = 56 * 1024 * 1024
ROW_TILE = 512
MOE_TILE = 512
ATT_TQ = 512
ATT_TK = 512
DIL_UNROLL = 2
GLA_UNROLL = 4
GLA_TILE = 256
DIL_TQ = 256


def _params(*sem):
    return pltpu.CompilerParams(dimension_semantics=sem, vmem_limit_bytes=VMEM_LIMIT)


def _full(shape):
    n = len(shape)
    return pl.BlockSpec(shape, lambda *_: (0,) * n)


_TOKEN_SPEC = pl.BlockSpec(memory_space=pl.ANY)


def _mm(a, b):
    return jnp.dot(a, b, preferred_element_type=F32)


def _mm_nt(a, b):
    return lax.dot_general(a, b, (((1,), (1,)), ((), ())), preferred_element_type=F32)


def _mm_tn(a, b):
    return lax.dot_general(a, b, (((0,), (0,)), ((), ())), preferred_element_type=F32)


def _layer_norm(y, g, b):
    mu = jnp.mean(y, axis=-1, keepdims=True)
    yc = y - mu
    var = jnp.mean(yc * yc, axis=-1, keepdims=True)
    return yc * lax.rsqrt(var + LN_EPS) * g + b


def _log_sigmoid(z):
    return jnp.minimum(z, 0.0) - jnp.log(1.0 + jnp.exp(-jnp.abs(z)))


def _even_in_kernel(x_ref, wa_ref, wb_ref, wg_ref, tok_ref, oa_ref, ob_ref, og_ref):
    x = x_ref[...].astype(BF16)
    oa_ref[...] = _mm(x, wa_ref[...]).astype(oa_ref.dtype)
    ob_ref[...] = _mm(x, wb_ref[...])
    og_ref[...] = _mm(x, wg_ref[...])


def _even_in(x2d, wa, wb, wg, token):
    T = x2d.shape[0]
    tm = ROW_TILE
    return pl.pallas_call(
        _even_in_kernel,
        grid=(T // tm,),
        in_specs=[pl.BlockSpec((tm, D_MODEL), lambda i: (i, 0)),
                  _full(wa.shape), _full(wb.shape), _full(wg.shape), _TOKEN_SPEC],
        out_specs=[pl.BlockSpec((tm, wa.shape[1]), lambda i: (i, 0)),
                   pl.BlockSpec((tm, wb.shape[1]), lambda i: (i, 0)),
                   pl.BlockSpec((tm, wg.shape[1]), lambda i: (i, 0))],
        out_shape=[jax.ShapeDtypeStruct((T, wa.shape[1]), BF16),
                   jax.ShapeDtypeStruct((T, wb.shape[1]), F32),
                   jax.ShapeDtypeStruct((T, wg.shape[1]), F32)],
        compiler_params=_params("parallel"),
        name="even_in_proj",
    )(x2d, wa, wb, wg, token)


def _gla_kernel(q_ref, k_ref, v_ref, r_ref, gate_ref, wf_ref, bf_ref, wb_ref, bb_ref, ng_ref,
                o_ref, la_ref, qd_ref, ke_ref, vt_ref, tot_ref, of_ref, ob_ref, *, seq):
    C = GLA_CHUNK
    n_chunks = seq // C
    tile = min(GLA_TILE, seq)
    cpt = tile // C
    gate = gate_ref[...]
    gate_hi = gate.astype(BF16)
    gate_lo = (gate - gate_hi.astype(F32)).astype(BF16)
    for d, (w_ref, b_ref) in enumerate(((wf_ref, bf_ref), (wb_ref, bb_ref))):
        w = w_ref[...]
        w_hi = w.astype(BF16)
        w_lo = (w - w_hi.astype(F32)).astype(BF16)
        z = _mm(gate_hi, w_hi) + _mm(gate_hi, w_lo) + _mm(gate_lo, w_hi) + b_ref[...]
        la_ref[d] = _log_sigmoid(z) * (1.0 / GLA_TAU)

    trow = lax.broadcasted_iota(jnp.int32, (tile, tile), 0)
    tcol = lax.broadcasted_iota(jnp.int32, (tile, tile), 1)
    same_chunk = (trow // C) == (tcol // C)
    keep = (same_chunk & (trow >= tcol), same_chunk & (tcol >= trow))
    rmod = lax.broadcasted_iota(jnp.int32, (tile, LANES), 0) % C
    lane = lax.broadcasted_iota(jnp.int32, (1, LANES), 1)
    head_lane = (lane < GLA_DK, lane >= GLA_DK)
    srow = lax.broadcasted_iota(jnp.int32, (2 * GLA_DV, LANES), 0)
    scol = lax.broadcasted_iota(jnp.int32, (2 * GLA_DV, LANES), 1)
    diag = (srow < GLA_DV) == (scol < GLA_DK)
    scale = GLA_DK ** -0.5

    def chunk_scan(x, d):
        step = 1
        while step < C:
            if d == 0:
                x = x + jnp.where(rmod >= step, pltpu.roll(x, step, 0), 0.0)
            else:
                x = x + jnp.where(rmod < C - step, pltpu.roll(x, tile - step, 0), 0.0)
            step *= 2
        return x

    def intra(j, _):
        r0 = pl.multiple_of(j * tile, tile)
        rows = pl.ds(r0, tile)
        q = q_ref[rows, :].astype(F32) * scale
        k = k_ref[rows, :].astype(F32)
        v = v_ref[rows, :]
        vf = v.astype(F32)
        for c in range(cpt):
            vt_ref[j * cpt + c] = vf[c * C:(c + 1) * C, :].T.astype(BF16)
        for d, out_ref in enumerate((of_ref, ob_ref)):
            la = la_ref[d, rows, :]
            b = chunk_scan(la, d)
            tots = [jnp.sum(la[c * C:(c + 1) * C, :], axis=0, keepdims=True) for c in range(cpt)]
            for c in range(cpt):
                tot_ref[d, pl.ds(j * cpt + c, 1), :] = tots[c]
            tot = jnp.concatenate([jnp.broadcast_to(t, (C, LANES)) for t in tots], axis=0)
            qd = (q * jnp.exp(b)).astype(BF16)
            kd = (k * jnp.exp(-b)).astype(BF16)
            qd_ref[d, rows, :] = qd
            ke_ref[d, rows, :] = (k * jnp.exp(tot - b)).astype(BF16)
            parts = []
            for h in range(2):
                qh = jnp.where(head_lane[h], qd, jnp.zeros_like(qd))
                s = jnp.where(keep[d], _mm_nt(qh, kd), 0.0)
                parts.append(_mm(s.astype(BF16), v[:, h * GLA_DV:(h + 1) * GLA_DV]))
            out_ref[rows, :] = jnp.concatenate(parts, axis=1)
        return 0

    lax.fori_loop(0, seq // tile, intra, 0, unroll=2)

    def inter(i, carry):
        new = []
        for d, (state, out_ref) in enumerate(zip(carry, (of_ref, ob_ref))):
            c = i if d == 0 else n_chunks - 1 - i
            rows = pl.ds(pl.multiple_of(c * C, C), C)
            out_ref[rows, :] += _mm_nt(qd_ref[d, rows, :], state.astype(BF16))
            upd = jnp.where(diag, _mm(vt_ref[c], ke_ref[d, rows, :]), 0.0)
            new.append(jnp.exp(tot_ref[d, pl.ds(c, 1), :]) * state + upd)
        return tuple(new)

    zero = jnp.zeros((2 * GLA_DV, LANES), F32)
    lax.fori_loop(0, n_chunks, inter, (zero, zero), unroll=GLA_UNROLL)

    blk = 256

    def finish(j, _):
        rows = pl.ds(pl.multiple_of(j * blk, blk), blk)
        o = of_ref[rows, :] + ob_ref[rows, :]
        g = ng_ref[...]
        outs = []
        for h in range(2):
            oh = o[:, h * GLA_DV:(h + 1) * GLA_DV]
            ms = jnp.mean(oh * oh, axis=-1, keepdims=True)
            outs.append(oh * lax.rsqrt(ms + RMS_EPS) * g[:, h * GLA_DV:(h + 1) * GLA_DV])
        r = r_ref[rows, :].astype(F32)
        o_ref[rows, :] = (jnp.concatenate(outs, axis=1) * (r * jax.nn.sigmoid(r))).astype(o_ref.dtype)
        return 0

    lax.fori_loop(0, seq // blk, finish, 0)


def _gla(h_a, gate, wf, bf, wb, bb, norm_g, B, S):
    pairs = GLA_HEADS // 2
    kq, kv = 2 * GLA_DK, 2 * GLA_DV
    sq = pl.BlockSpec((None, S, kq), lambda b, g: (b, 0, g))
    sk = pl.BlockSpec((None, S, kq), lambda b, g: (b, 0, pairs + g))
    sv = pl.BlockSpec((None, S, kv), lambda b, g: (b, 0, (2 * GLA_QK) // kv + g))
    sr = pl.BlockSpec((None, S, kv), lambda b, g: (b, 0, (2 * GLA_QK + GLA_VW) // kv + g))
    sg = pl.BlockSpec((None, S, LANES), lambda b, g: (b, 0, 0))
    sw = pl.BlockSpec((LANES, kq), lambda b, g: (0, g))
    sb = pl.BlockSpec((1, kq), lambda b, g: (0, g))
    sn = pl.BlockSpec((1, kv), lambda b, g: (0, g))
    return pl.pallas_call(
        functools.partial(_gla_kernel, seq=S),
        grid=(B, pairs),
        in_specs=[sq, sk, sv, sr, sg, sw, sb, sw, sb, sn],
        out_specs=pl.BlockSpec((None, S, kv), lambda b, g: (b, 0, g)),
        out_shape=jax.ShapeDtypeStruct((B, S, GLA_VW), BF16),
        scratch_shapes=[pltpu.VMEM((2, S, kq), F32),
                        pltpu.VMEM((2, S, kq), BF16),
                        pltpu.VMEM((2, S, kq), BF16),
                        pltpu.VMEM((S // GLA_CHUNK, kv, GLA_CHUNK), BF16),
                        pltpu.VMEM((2, S // GLA_CHUNK, kq), F32),
                        pltpu.VMEM((S, kv), F32), pltpu.VMEM((S, kv), F32)],
        compiler_params=_params("parallel", "parallel"),
        name="gla_mixer",
    )(h_a, h_a, h_a, h_a, gate, wf, bf, wb, bb, norm_g)


def _dil_kernel(q_ref, k_ref, v_ref, o_ref, m_ref, l_ref, acc_ref, bias_ref, s_ref, *, seq):
    R = DIL_RADIUS
    pair = pl.program_id(1)
    lane = lax.broadcasted_iota(jnp.int32, (1, LANES), 1)
    first_head = lane < DIL_DH
    scale = DIL_DH ** -0.5

    for p_idx, (window, dil) in enumerate(DIL_PATTERNS):
        assert window // (2 * dil) == R
        length = seq // dil
        tq = min(DIL_TQ, length)
        nk = tq + 2 * R
        tpr = length // tq
        shift = tpr.bit_length() - 1
        assert tpr == 1 << shift and tq % R == 0
        qi = lax.broadcasted_iota(jnp.int32, (tq, nk), 0)
        kj = lax.broadcasted_iota(jnp.int32, (tq, nk), 1)
        dist = jnp.abs(kj - R - qi)
        for h in range(2):
            slope = 1.0 / jnp.left_shift(jnp.ones((tq, nk), jnp.int32), 2 * pair + h + 1).astype(F32)
            bias_ref[h, :tq, :nk] = jnp.where(dist <= R, -(dil * dist).astype(F32) * slope, MASK_VALUE)
        ones = jnp.ones((nk, LANES), BF16)

        def rows_at(start, n, dil=dil):
            return pl.ds(start, n, stride=dil) if dil > 1 else pl.ds(pl.multiple_of(start, R), n)

        n_tiles = seq // tq
        assert n_tiles % 2 == 0

        def tile_pos(idx, dil=dil, tq=tq, tpr=tpr, shift=shift):
            res = lax.shift_right_logical(idx, shift)
            t = jnp.bitwise_and(idx, tpr - 1)
            start = res + dil * tq * t
            has_prev = t > 0
            has_next = t < tpr - 1
            prev = jnp.where(has_prev, start - dil * R, start)
            nxt = jnp.where(has_next, start + dil * tq, start)
            return start, prev, nxt, has_prev, has_next

        def scores(idx, slot, tq=tq, nk=nk, rows_at=rows_at, tile_pos=tile_pos):
            start, prev, nxt, _, _ = tile_pos(idx)
            q = (q_ref[rows_at(start, tq), :] * scale).astype(BF16)
            kcat = jnp.concatenate([k_ref[rows_at(prev, R), :], k_ref[rows_at(start, tq), :],
                                    k_ref[rows_at(nxt, R), :]], axis=0).astype(BF16)
            for h in range(2):
                qh = jnp.where(first_head if h == 0 else ~first_head, q, jnp.zeros_like(q))
                s_ref[slot, h, :tq, :nk] = _mm_nt(qh, kcat)

        def values(idx, slot, tq=tq, nk=nk, p_idx=p_idx, rows_at=rows_at, kj=kj, ones=ones,
                   tile_pos=tile_pos):
            start, prev, nxt, has_prev, has_next = tile_pos(idx)
            vcat = jnp.concatenate([v_ref[rows_at(prev, R), :], v_ref[rows_at(start, tq), :],
                                    v_ref[rows_at(nxt, R), :]], axis=0).astype(BF16)
            vones = jnp.concatenate([vcat, ones], axis=1)
            in_seq = ((kj >= R) | has_prev) & ((kj < R + tq) | has_next)
            ms, pvs = [], []
            for h in range(2):
                s = jnp.where(in_seq, s_ref[slot, h, :tq, :nk] + bias_ref[h, :tq, :nk], MASK_VALUE)
                m = jnp.max(s, axis=-1, keepdims=True)
                ms.append(m)
                pvs.append(_mm(jnp.exp(s - m).astype(BF16), vones))
            out_rows = rows_at(start, tq)
            m_ref[p_idx, out_rows, :] = jnp.where(first_head, ms[0], ms[1])
            l_ref[p_idx, out_rows, :] = jnp.where(first_head, pvs[0][:, LANES:], pvs[1][:, LANES:])
            acc_ref[p_idx, out_rows, :] = jnp.where(first_head, pvs[0][:, :LANES], pvs[1][:, :LANES])

        scores(0, 0)

        def body(i2, _, n_tiles=n_tiles, scores=scores, values=values):
            for u in range(2):
                idx = 2 * i2 + u
                scores(jnp.minimum(idx + 1, n_tiles - 1), 1 - u)
                values(idx, u)
            return 0

        lax.fori_loop(0, n_tiles // 2, body, 0, unroll=DIL_UNROLL)

    blk = 256
    n_pat = len(DIL_PATTERNS)

    def finish(j, _):
        rows = pl.ds(pl.multiple_of(j * blk, blk), blk)
        ms = [m_ref[p, rows, :] for p in range(n_pat)]
        m = functools.reduce(jnp.maximum, ms)
        ws = [jnp.exp(x - m) for x in ms]
        l = functools.reduce(lambda a, b: a + b, [w * l_ref[p, rows, :] for p, w in enumerate(ws)])
        acc = functools.reduce(lambda a, b: a + b, [w * acc_ref[p, rows, :] for p, w in enumerate(ws)])
        o_ref[rows, :] = (acc / l).astype(o_ref.dtype)
        return 0

    lax.fori_loop(0, seq // blk, finish, 0)


def _dilated(h_b, B, S):
    pairs = DIL_HEADS // 2
    sq = pl.BlockSpec((None, S, LANES), lambda b, g: (b, 0, g))
    sk = pl.BlockSpec((None, S, LANES), lambda b, g: (b, 0, pairs + g))
    sv = pl.BlockSpec((None, S, LANES), lambda b, g: (b, 0, 2 * pairs + g))
    return pl.pallas_call(
        functools.partial(_dil_kernel, seq=S),
        grid=(B, pairs),
        in_specs=[sq, sk, sv],
        out_specs=pl.BlockSpec((None, S, LANES), lambda b, g: (b, 0, g)),
        out_shape=jax.ShapeDtypeStruct((B, S, DIL_W), BF16),
        scratch_shapes=[pltpu.VMEM((len(DIL_PATTERNS), S, LANES), F32)] * 3
                       + [pltpu.VMEM((2, DIL_TQ, DIL_TQ + 2 * DIL_RADIUS), F32),
                          pltpu.VMEM((2, 2, DIL_TQ, DIL_TQ + 2 * DIL_RADIUS), F32)],
        compiler_params=_params("parallel", "parallel"),
        name="dilated_mixer",
    )(h_b, h_b, h_b)


def _route(lt):
    g = [lt[i:i + 1, :] for i in range(MOE_GROUPS)]
    gmax = functools.reduce(jnp.maximum, g)
    gexp = [jnp.exp(x - gmax) for x in g]
    gsum = functools.reduce(lambda a, b: a + b, gexp)
    gprob = [x / gsum for x in gexp]
    g_val, g_idx = gprob[0], jnp.zeros_like(gprob[0], dtype=jnp.int32)
    for i in range(1, MOE_GROUPS):
        better = gprob[i] > g_val
        g_idx = jnp.where(better, i, g_idx)
        g_val = jnp.where(better, gprob[i], g_val)
    e = []
    for j in range(MOE_EPG):
        x = lt[MOE_GROUPS + j:MOE_GROUPS + j + 1, :]
        for gi in range(1, MOE_GROUPS):
            r0 = MOE_GROUPS + gi * MOE_EPG + j
            x = jnp.where(g_idx == gi, lt[r0:r0 + 1, :], x)
        e.append(x)
    emax = functools.reduce(jnp.maximum, e)
    eexp = [jnp.exp(x - emax) for x in e]
    esum = functools.reduce(lambda a, b: a + b, eexp)
    eprob = [x / esum for x in eexp]
    v1, i1 = eprob[0], jnp.zeros_like(g_idx)
    for j in range(1, MOE_EPG):
        better = eprob[j] > v1
        i1 = jnp.where(better, j, i1)
        v1 = jnp.where(better, eprob[j], v1)
    v2, i2 = jnp.full_like(v1, -1.0), jnp.zeros_like(g_idx)
    for j in range(MOE_EPG):
        cand = jnp.where(i1 == j, -1.0, eprob[j])
        better = cand > v2
        i2 = jnp.where(better, j, i2)
        v2 = jnp.where(better, cand, v2)
    den = v1 + v2
    ids = jnp.concatenate([g_idx * MOE_EPG + i1, g_idx * MOE_EPG + i2], axis=0)
    wts = jnp.concatenate([g_val * (v1 / den), g_val * (v2 / den)], axis=0)
    return ids, wts


def _proj_ln_route_kernel(*refs, n_in):
    x_ref = refs[0]
    o_refs = refs[1:1 + n_in]
    w_refs = refs[1 + n_in:1 + 2 * n_in]
    g_ref, b_ref, wr_ref, br_ref, x1_ref, x1b_ref, ids_ref, wts_ref = refs[1 + 2 * n_in:]
    m = _mm(o_refs[0][...], w_refs[0][...])
    for o_ref, w_ref in zip(o_refs[1:], w_refs[1:]):
        m = m + _mm(o_ref[...], w_ref[...])
    x1 = _layer_norm(ALPHA * x_ref[...] + m, g_ref[...], b_ref[...])
    x1_ref[...] = x1
    x_hi = x1.astype(BF16)
    x1b_ref[...] = x_hi
    x_lo = (x1 - x_hi.astype(F32)).astype(BF16)
    wr = wr_ref[...]
    w_hi = wr.astype(BF16)
    w_lo = (wr - w_hi.astype(F32)).astype(BF16)
    lt = _mm_nt(w_hi, x_hi) + _mm_nt(w_hi, x_lo) + _mm_nt(w_lo, x_hi) + br_ref[...]
    ids, wts = _route(lt)
    ids_ref[...] = ids
    wts_ref[...] = wts


def _proj_ln_route(x2d, outs, ws, ln_g, ln_b, wr, br):
    T = x2d.shape[0]
    tm = ROW_TILE
    n_in = len(outs)
    row = lambda w: pl.BlockSpec((tm, w), lambda i: (i, 0))
    tok = pl.BlockSpec((2, tm), lambda i: (0, i))
    return pl.pallas_call(
        functools.partial(_proj_ln_route_kernel, n_in=n_in),
        grid=(T // tm,),
        in_specs=[row(D_MODEL)] + [row(o.shape[1]) for o in outs] + [_full(w.shape) for w in ws]
                 + [_full(ln_g.shape), _full(ln_b.shape), _full(wr.shape), _full(br.shape)],
        out_specs=[row(D_MODEL), row(D_MODEL), tok, tok],
        out_shape=[jax.ShapeDtypeStruct((T, D_MODEL), F32), jax.ShapeDtypeStruct((T, D_MODEL), BF16),
                   jax.ShapeDtypeStruct((2, T), jnp.int32), jax.ShapeDtypeStruct((2, T), F32)],
        compiler_params=_params("parallel"),
        name="proj_ln_route",
    )(x2d, *outs, *ws, ln_g, ln_b, wr, br)


def _moe_kernel(te_ref, nu_ref, xs_ref, w1_ref, w3_ref, w2_ref, rw_ref, tok_ref, ys_ref, w13_s, w2_s):
    i = pl.program_id(0)

    @pl.when((i == 0) | (te_ref[i] != te_ref[jnp.maximum(i - 1, 0)]))
    def _():
        w13_s[:, :MOE_FF] = w1_ref[...].astype(BF16)
        w13_s[:, MOE_FF:] = w3_ref[...].astype(BF16)
        w2_s[...] = w2_ref[...].astype(BF16)

    @pl.when(i < nu_ref[0])
    def _():
        h = _mm(xs_ref[...], w13_s[...])
        h1, h3 = h[:, :MOE_FF], h[:, MOE_FF:]
        hidden = (h1 * jax.nn.sigmoid(h1)) * h3
        y = _mm(hidden.astype(BF16), w2_s[...])
        ys_ref[...] = (y * rw_ref[...]).astype(ys_ref.dtype)

    @pl.when(i >= nu_ref[0])
    def _():
        ys_ref[...] = jnp.zeros_like(ys_ref)


def _moe_experts(xs, w1, w3, w2, layer, row_w, tile_expert, n_used, token):
    P = xs.shape[0]
    tm = MOE_TILE
    grid_spec = pltpu.PrefetchScalarGridSpec(
        num_scalar_prefetch=2,
        grid=(P // tm,),
        in_specs=[pl.BlockSpec((tm, D_MODEL), lambda i, te, nu: (i, 0)),
                  pl.BlockSpec((None, None, D_MODEL, MOE_FF), lambda i, te, nu: (layer, te[i], 0, 0)),
                  pl.BlockSpec((None, None, D_MODEL, MOE_FF), lambda i, te, nu: (layer, te[i], 0, 0)),
                  pl.BlockSpec((None, None, MOE_FF, D_MODEL), lambda i, te, nu: (layer, te[i], 0, 0)),
                  pl.BlockSpec((tm, 1), lambda i, te, nu: (i, 0)), _TOKEN_SPEC],
        out_specs=pl.BlockSpec((tm, D_MODEL), lambda i, te, nu: (i, 0)),
        scratch_shapes=[pltpu.VMEM((D_MODEL, 2 * MOE_FF), BF16), pltpu.VMEM((MOE_FF, D_MODEL), BF16)],
    )
    return pl.pallas_call(
        _moe_kernel,
        grid_spec=grid_spec,
        out_shape=jax.ShapeDtypeStruct((P, D_MODEL), BF16),
        compiler_params=_params("arbitrary"),
        name="moe_experts",
    )(tile_expert, n_used, xs, w1, w3, w2, row_w, token)


def _moe_plan(ids, wts, T):
    tm = MOE_TILE
    n_assign = 2 * T
    e_flat = ids.reshape(n_assign)
    onehot = (e_flat[:, None] == jnp.arange(MOE_EXPERTS, dtype=jnp.int32)[None, :]).astype(jnp.int32)
    csum = jnp.cumsum(onehot, axis=0)
    counts = csum[-1]
    padded = ((counts + tm - 1) // tm) * tm
    ends = jnp.cumsum(padded)
    offs = ends - padded
    pos = jnp.sum(onehot * (csum - onehot + offs[None, :]), axis=1)
    P = n_assign + MOE_EXPERTS * tm
    n_tiles = P // tm
    tile_start = jnp.arange(n_tiles, dtype=jnp.int32) * tm
    tile_expert = jnp.minimum(jnp.sum((tile_start[:, None] >= ends[None, :]).astype(jnp.int32), axis=1),
                              MOE_EXPERTS - 1).astype(jnp.int32)
    n_used = (ends[-1:] // tm).astype(jnp.int32)
    tok = jnp.arange(n_assign, dtype=jnp.int32) % T
    _, tok_sorted, w_sorted = lax.sort((e_flat, tok, wts.reshape(n_assign)), num_keys=1, is_stable=True)
    t_onehot = (tile_expert[:, None] == jnp.arange(MOE_EXPERTS, dtype=jnp.int32)[None, :]).astype(jnp.int32)
    t_rank0 = tile_start - jnp.sum(t_onehot * offs[None, :], axis=1)
    t_count = jnp.sum(t_onehot * counts[None, :], axis=1)
    t_first = jnp.sum(t_onehot * (jnp.cumsum(counts) - counts)[None, :], axis=1)
    r = t_rank0[:, None] + jnp.arange(tm, dtype=jnp.int32)[None, :]
    valid = (r < t_count[:, None]).reshape(P)
    u = jnp.clip(t_first[:, None] + r, 0, n_assign - 1).reshape(P)
    src_tok = jnp.where(valid, tok_sorted.at[u].get(mode="promise_in_bounds"), 0)
    row_w = jnp.where(valid, w_sorted.at[u].get(mode="promise_in_bounds"), 0.0)
    return src_tok, row_w.reshape(P, 1), tile_expert, n_used, pos


def _combine_ln_kernel(x_ref, y0_ref, y1_ref, g_ref, b_ref, tok_ref, o_ref):
    y = ALPHA * x_ref[...] + y0_ref[...].astype(F32) + y1_ref[...].astype(F32)
    o_ref[...] = _layer_norm(y, g_ref[...], b_ref[...])


def _combine_ln(x2d, y0, y1, ln_g, ln_b, token):
    T = x2d.shape[0]
    tm = ROW_TILE
    row = pl.BlockSpec((tm, D_MODEL), lambda i: (i, 0))
    return pl.pallas_call(
        _combine_ln_kernel,
        grid=(T // tm,),
        in_specs=[row, row, row, _full(ln_g.shape), _full(ln_b.shape), _TOKEN_SPEC],
        out_specs=row,
        out_shape=jax.ShapeDtypeStruct((T, D_MODEL), F32),
        compiler_params=_params("parallel"),
        name="combine_ln",
    )(x2d, y0, y1, ln_g, ln_b, token)


def _experts_stage(i, routed, prm, token):
    _, x1b, (src_tok, row_w, tile_expert, n_used, _) = routed
    xs = x1b.at[src_tok].get(mode="promise_in_bounds")
    return _moe_experts(xs, prm["moe_w1"], prm["moe_w3"], prm["moe_w2"], i, row_w, tile_expert, n_used, token)


def _combine_stage(i, routed, ys, prm, token):
    x1, _, plan = routed
    T = x1.shape[0]
    pos = plan[4]
    y0 = ys.at[pos[:T]].get(mode="promise_in_bounds")
    y1 = ys.at[pos[T:]].get(mode="promise_in_bounds")
    return _combine_ln(x1, y0, y1, prm["ln2_g"][i:i + 1], prm["ln2_b"][i:i + 1], token)


def _rope_block(x, c, sa, sb):
    return x * c + pltpu.roll(x, LANES - 32, 1) * sa + pltpu.roll(x, 32, 1) * sb


def _mla_in_kernel(x_ref, win_ref, qg_ref, kvg_ref, wq_ref, wk_ref, wv_ref, c_ref, sa_ref, sb_ref,
                   tok_ref, q_ref, k_ref, v_ref):
    x = x_ref[...].astype(BF16)
    h = _mm(x, win_ref[...])
    c_q = h[:, :MLA_Q_LORA]
    c_kv = h[:, MLA_Q_LORA:MLA_Q_LORA + MLA_KV_LORA]
    k_rope = h[:, MLA_Q_LORA + MLA_KV_LORA:]
    cqn = c_q * lax.rsqrt(jnp.mean(c_q * c_q, axis=-1, keepdims=True) + RMS_EPS) * qg_ref[...]
    ckn = c_kv * lax.rsqrt(jnp.mean(c_kv * c_kv, axis=-1, keepdims=True) + RMS_EPS) * kvg_ref[...]
    cqn, ckn = cqn.astype(BF16), ckn.astype(BF16)
    scale = (MLA_NOPE + MLA_ROPE) ** -0.5 * LOG2_E
    q = _mm(cqn, wq_ref[...]) * scale
    kn = _mm(ckn, wk_ref[...])
    v = _mm(ckn, wv_ref[...]).astype(v_ref.dtype)
    c, sa, sb = c_ref[...], sa_ref[...], sb_ref[...]
    kr = _rope_block(k_rope, c, sa, sb).astype(k_ref.dtype)
    ones = jnp.ones((x.shape[0], LANES), v_ref.dtype)
    for g in range(MLA_HEADS // 2):
        lo = 2 * LANES * g
        v_ref[:, lo:lo + LANES] = v[:, LANES * g:LANES * (g + 1)]
        v_ref[:, lo + LANES:lo + 2 * LANES] = ones
        q_ref[:, lo:lo + LANES] = q[:, lo:lo + LANES].astype(q_ref.dtype)
        q_ref[:, lo + LANES:lo + 2 * LANES] = _rope_block(q[:, lo + LANES:lo + 2 * LANES], c, sa, sb
                                                          ).astype(q_ref.dtype)
        k_ref[:, lo:lo + LANES] = kn[:, LANES * g:LANES * (g + 1)].astype(k_ref.dtype)
        k_ref[:, lo + LANES:lo + 2 * LANES] = kr


def _mla_in(x2d, win, qg, kvg, wq, wk, wv, c, sa, sb, S, token):
    T = x2d.shape[0]
    tm = ROW_TILE
    row = lambda w: pl.BlockSpec((tm, w), lambda i: (i, 0))
    n_s = S // tm
    tab = pl.BlockSpec((tm, LANES), lambda i: (i % n_s, 0))
    pairs = MLA_HEADS // 2
    return pl.pallas_call(
        _mla_in_kernel,
        grid=(T // tm,),
        in_specs=[row(D_MODEL), _full(win.shape), _full(qg.shape), _full(kvg.shape), _full(wq.shape),
                  _full(wk.shape), _full(wv.shape), tab, tab, tab, _TOKEN_SPEC],
        out_specs=[row(pairs * 2 * LANES)] * 3,
        out_shape=[jax.ShapeDtypeStruct((T, pairs * 2 * LANES), BF16)] * 3,
        compiler_params=_params("parallel"),
        name="mla_in_proj",
    )(x2d, win, qg, kvg, wq, wk, wv, c, sa, sb, token)


def _mla_attn_kernel(q_ref, k_ref, v_ref, o_ref, sa_ref, sb_ref, *, seq):
    lane2 = lax.broadcasted_iota(jnp.int32, (1, 2 * LANES), 1)
    rl = lane2 - LANES
    in_a = (lane2 < MLA_NOPE) | ((rl >= 0) & (rl < 16)) | ((rl >= 32) & (rl < 48))
    in_b = ((lane2 >= MLA_NOPE) & (lane2 < LANES)) | ((rl >= 16) & (rl < 32)) | ((rl >= 48) & (rl < 64))
    lane = lax.broadcasted_iota(jnp.int32, (1, LANES), 1)
    q = q_ref[...]
    tq = q.shape[0]
    tk = min(ATT_TK, seq)
    n_k = seq // tk
    q_heads = (jnp.where(in_a, q, jnp.zeros_like(q)), jnp.where(in_b, q, jnp.zeros_like(q)))
    s_refs = (sa_ref, sb_ref)

    def score_chunk(h, j, m):
        s = _mm_nt(q_heads[h], k_ref[j * tk:(j + 1) * tk, :])
        s_refs[h][:, j * tk:(j + 1) * tk] = s
        for c in range(tk // LANES):
            m = jnp.maximum(m, s[:, c * LANES:(c + 1) * LANES])
        return m

    def value_chunk(h, j, m_row, acc):
        p = jnp.exp2(s_refs[h][:, j * tk:(j + 1) * tk] - m_row).astype(BF16)
        return acc + _mm(p, v_ref[j * tk:(j + 1) * tk, :])

    neg = jnp.full((tq, LANES), -jnp.inf, F32)
    m = neg
    for j in range(n_k):
        m = score_chunk(0, j, m)
    m_a = jnp.max(m, axis=-1, keepdims=True)
    acc_a = jnp.zeros((tq, 2 * LANES), F32)
    m = neg
    for j in range(n_k):
        acc_a = value_chunk(0, j, m_a, acc_a)
        m = score_chunk(1, j, m)
    m_b = jnp.max(m, axis=-1, keepdims=True)
    acc_b = jnp.zeros((tq, 2 * LANES), F32)
    for j in range(n_k):
        acc_b = value_chunk(1, j, m_b, acc_b)
    out_a = acc_a[:, :LANES] / acc_a[:, LANES:]
    out_b = acc_b[:, :LANES] / acc_b[:, LANES:]
    o_ref[...] = jnp.where(lane < MLA_V, out_a, out_b).astype(o_ref.dtype)


def _mla_attn(q, k, v, B, S):
    pairs = MLA_HEADS // 2
    tq = min(ATT_TQ, S)
    return pl.pallas_call(
        functools.partial(_mla_attn_kernel, seq=S),
        grid=(B, pairs, S // tq),
        in_specs=[pl.BlockSpec((None, tq, 2 * LANES), lambda b, g, i: (b, i, g)),
                  pl.BlockSpec((None, S, 2 * LANES), lambda b, g, i: (b, 0, g)),
                  pl.BlockSpec((None, S, 2 * LANES), lambda b, g, i: (b, 0, g))],
        out_specs=pl.BlockSpec((None, tq, LANES), lambda b, g, i: (b, i, g)),
        out_shape=jax.ShapeDtypeStruct((B, S, pairs * LANES), BF16),
        scratch_shapes=[pltpu.VMEM((tq, S), F32), pltpu.VMEM((tq, S), F32)],
        compiler_params=_params("parallel", "parallel", "arbitrary"),
        name="mla_attention",
    )(q, k, v)


def _prep_even(w_in, wa_f, ba_f, wa_b, ba_b, norm_g, w_out):
    o_q, o_k, o_v, o_r = 0, GLA_QK, 2 * GLA_QK, 2 * GLA_QK + GLA_VW
    o_af = o_r + GLA_VW
    o_ab = o_af + GLA_RANK
    o_qb = o_ab + GLA_RANK
    wa = w_in[:, :o_af].astype(BF16)
    wb = w_in[:, o_qb:].astype(BF16)
    wg = jnp.zeros((D_MODEL, LANES), F32).at[:, :2 * GLA_RANK].set(w_in[:, o_af:o_qb]).astype(BF16)
    wf = jnp.zeros((LANES, GLA_QK), F32).at[:GLA_RANK].set(wa_f)
    wb_gate = jnp.zeros((LANES, GLA_QK), F32).at[GLA_RANK:2 * GLA_RANK].set(wa_b)
    return dict(wa=wa, wb=wb, wg=wg, wf=wf, bf=ba_f.reshape(1, -1), wb_gate=wb_gate,
                bb=ba_b.reshape(1, -1), norm_g=norm_g.reshape(1, -1),
                wo_a=w_out[:GLA_VW].astype(BF16), wo_b=w_out[GLA_VW:].astype(BF16))


def _prep_odd(w_in, q_norm, kv_norm, w_uq, w_ukv, w_out, S):
    half = MLA_ROPE // 2
    pairs = MLA_HEADS // 2
    kr = w_in[:, MLA_Q_LORA + MLA_KV_LORA:]
    kr_rep = jnp.concatenate([kr[:, :half], kr[:, :half], kr[:, half:], kr[:, half:],
                              jnp.zeros((D_MODEL, LANES - 2 * MLA_ROPE), F32)], axis=1)
    win = jnp.concatenate([w_in[:, :MLA_Q_LORA + MLA_KV_LORA], kr_rep], axis=1).astype(BF16)
    uq = w_uq.reshape(MLA_Q_LORA, MLA_HEADS, MLA_NOPE + MLA_ROPE)
    nope = uq[:, :, :MLA_NOPE].reshape(MLA_Q_LORA, pairs, 2 * MLA_NOPE)
    r1 = uq[:, :, MLA_NOPE:MLA_NOPE + half].reshape(MLA_Q_LORA, pairs, 2 * half)
    r2 = uq[:, :, MLA_NOPE + half:].reshape(MLA_Q_LORA, pairs, 2 * half)
    pad = jnp.zeros((MLA_Q_LORA, pairs, LANES - 2 * MLA_ROPE), F32)
    wq = jnp.concatenate([nope, r1, r2, pad], axis=2).reshape(MLA_Q_LORA, pairs * 2 * LANES).astype(BF16)
    ukv = w_ukv.reshape(MLA_KV_LORA, MLA_HEADS, MLA_NOPE + MLA_V)
    wk = ukv[:, :, :MLA_NOPE].reshape(MLA_KV_LORA, MLA_HEADS * MLA_NOPE).astype(BF16)
    wv = ukv[:, :, MLA_NOPE:].reshape(MLA_KV_LORA, MLA_HEADS * MLA_V).astype(BF16)
    inv = 1.0 / (ROPE_BASE ** (jnp.arange(0, MLA_ROPE, 2, dtype=F32) / MLA_ROPE))
    ang = jnp.arange(S, dtype=F32)[:, None] * inv[None, :]
    cos, sin = jnp.cos(ang), jnp.sin(ang)
    z16 = jnp.zeros((S, half), F32)
    z64 = jnp.zeros((S, LANES - 2 * MLA_ROPE), F32)
    c = jnp.concatenate([cos, cos, cos, cos, z64], axis=1)
    sa = jnp.concatenate([-sin, -sin, z16, z16, z64], axis=1)
    sb = jnp.concatenate([z16, z16, sin, sin, z64], axis=1)
    return dict(win=win, qg=q_norm.reshape(1, -1), kvg=kv_norm.reshape(1, -1), wq=wq, wk=wk, wv=wv,
                c=c, sa=sa, sb=sb, wo=w_out.astype(BF16))


def _prep_moe(wg, bg, we, be):
    wr = jnp.zeros((32, D_MODEL), F32).at[:MOE_GROUPS].set(wg.T).at[MOE_GROUPS:MOE_GROUPS + MOE_EXPERTS].set(we.T)
    br = jnp.zeros((32, 1), F32).at[:MOE_GROUPS, 0].set(bg).at[MOE_GROUPS:MOE_GROUPS + MOE_EXPERTS, 0].set(be)
    return dict(wr=wr, br=br)


def _mix_route(i, x2d, B, S, prm, token):
    ev, od, moe, ln1_g, ln1_b = prm["ev"], prm["od"], prm["moe"], prm["ln1_g"], prm["ln1_b"]
    T = B * S
    if i % 2 == 0:
        p = ev[i // 2]
        h_a, h_b, gate = _even_in(x2d, p["wa"], p["wb"], p["wg"], token)
        o_a = _gla(h_a.reshape(B, S, -1), gate.reshape(B, S, -1), p["wf"], p["bf"], p["wb_gate"],
                   p["bb"], p["norm_g"], B, S)
        o_b = _dilated(h_b.reshape(B, S, -1), B, S)
        outs = [o_a.reshape(T, -1), o_b.reshape(T, -1)]
        ws = [p["wo_a"], p["wo_b"]]
    else:
        p = od[i // 2]
        q, k, v = _mla_in(x2d, p["win"], p["qg"], p["kvg"], p["wq"], p["wk"], p["wv"],
                          p["c"], p["sa"], p["sb"], S, token)
        o = _mla_attn(q.reshape(B, S, -1), k.reshape(B, S, -1), v.reshape(B, S, -1), B, S)
        outs = [o.reshape(T, -1)]
        ws = [p["wo"]]
    m = moe[i]
    x1, x1b, ids, wts = _proj_ln_route(x2d, outs, ws, ln1_g[i:i + 1], ln1_b[i:i + 1], m["wr"], m["br"])
    return x1, x1b, _moe_plan(ids, wts, T)


def _plan_token(routed):
    return routed[2][0]


def _rows_token(rows):
    return rows[:8, :LANES]


def _trunk_pair(xa, xb, prm):
    (Ba, S, D), Bb = xa.shape, xb.shape[0]
    a = xa.reshape(Ba * S, D)
    b = xb.reshape(Bb * S, D)
    tok = jnp.zeros((1,), jnp.int32)
    for i in range(DEPTH):
        routed_a = _mix_route(i, a, Ba, S, prm, tok)
        if i > 0:
            b = _combine_stage(i - 1, routed_b, ys_b, prm, _plan_token(routed_a))
        routed_b = _mix_route(i, b, Bb, S, prm, _plan_token(routed_a))
        ys_a = _experts_stage(i, routed_a, prm, _plan_token(routed_b))
        ys_b = _experts_stage(i, routed_b, prm, _rows_token(ys_a))
        a = _combine_stage(i, routed_a, ys_a, prm, _rows_token(ys_b))
        tok = _rows_token(a)
    b = _combine_stage(DEPTH - 1, routed_b, ys_b, prm, tok)
    return a.reshape(Ba, S, D), b.reshape(Bb, S, D)


def kernel(x_prompt, x_sample, ev_w_in, ev_wa_f, ev_ba_f, ev_wa_b, ev_ba_b, ev_gla_norm, ev_w_out,
           od_w_in, od_q_norm, od_kv_norm, od_w_uq, od_w_ukv, od_w_out, ln1_g, ln1_b, ln2_g, ln2_b,
           moe_wg, moe_bg, moe_we, moe_be, moe_w1, moe_w3, moe_w2):
    S = x_prompt.shape[1]
    ev = [_prep_even(ev_w_in[j], ev_wa_f[j], ev_ba_f[j], ev_wa_b[j], ev_ba_b[j], ev_gla_norm[j], ev_w_out[j])
          for j in range(ev_w_in.shape[0])]
    od = [_prep_odd(od_w_in[j], od_q_norm[j], od_kv_norm[j], od_w_uq[j], od_w_ukv[j], od_w_out[j], S)
          for j in range(od_w_in.shape[0])]
    moe = [_prep_moe(moe_wg[i], moe_bg[i], moe_we[i], moe_be[i]) for i in range(DEPTH)]
    prm = dict(ev=ev, od=od, moe=moe, ln1_g=ln1_g, ln1_b=ln1_b, ln2_g=ln2_g, ln2_b=ln2_b,
               moe_w1=moe_w1, moe_w3=moe_w3, moe_w2=moe_w2)
    return _trunk_pair(x_prompt, x_sample, prm)
```

```python
import functools

import numpy as np
import jax
import jax.numpy as jnp
from jax import lax
from jax.experimental import pallas as pl
from jax.experimental.pallas import tpu as pltpu

F32 = jnp.float32
BF16 = jnp.bfloat16
HIGHEST = lax.Precision.HIGHEST

D_MODEL = 1024
DEPTH = 2
GLA_HEADS, GLA_DK, GLA_DV = 4, 64, 128
GLA_QK, GLA_VW = GLA_HEADS * GLA_DK, GLA_HEADS * GLA_DV
GLA_RANK, GLA_TAU, GLA_CHUNK = 16, 16.0, 64
DIL_HEADS, DIL_DH = 8, 64
DIL_W = DIL_HEADS * DIL_DH
DIL_PATTERNS = ((128, 1), (512, 4), (2048, 16))
DIL_RADIUS = 64
MASK_VALUE = -1e30
MLA_HEADS, MLA_NOPE, MLA_ROPE, MLA_V = 16, 64, 32, 64
MLA_Q_LORA, MLA_KV_LORA = 384, 128
ROPE_BASE = 10000.0
MOE_GROUPS, MOE_EPG, MOE_EXPERTS, MOE_FF = 4, 4, 16, 512
ALPHA = (2 * DEPTH) ** 0.25
LN_EPS = 1e-5
RMS_EPS = 1e-6
LOG2_E = 1.4426950408889634

LANES = 128
VMEM_LIMIT = 56 * 1024 * 1024
ROW_TILE = 512
MOE_TILE = 512
ATT_TQ = 512
ATT_TK = 512
DIL_UNROLL = 2
GLA_UNROLL = 4
GLA_TILE = 256
DIL_TQ = 256


def _params(*sem):
    return pltpu.CompilerParams(dimension_semantics=sem, vmem_limit_bytes=VMEM_LIMIT)


def _full(shape):
    n = len(shape)
    return pl.BlockSpec(shape, lambda *_: (0,) * n)


_TOKEN_SPEC = pl.BlockSpec(memory_space=pl.ANY)


def _cost(flops, bytes_accessed, transcendentals=0):
    return pl.CostEstimate(flops=int(flops), bytes_accessed=int(bytes_accessed),
                           transcendentals=int(transcendentals))


def _mm(a, b):
    return jnp.dot(a, b, preferred_element_type=F32)


def _mm_nt(a, b):
    return lax.dot_general(a, b, (((1,), (1,)), ((), ())), preferred_element_type=F32)


def _mm_tn(a, b):
    return lax.dot_general(a, b, (((0,), (0,)), ((), ())), preferred_element_type=F32)


def _layer_norm(y, g, b):
    mu = jnp.mean(y, axis=-1, keepdims=True)
    yc = y - mu
    var = jnp.mean(yc * yc, axis=-1, keepdims=True)
    return yc * lax.rsqrt(var + LN_EPS) * g + b


def _log_sigmoid(z):
    return jnp.minimum(z, 0.0) - jnp.log(1.0 + jnp.exp(-jnp.abs(z)))


def _even_in_kernel(x_ref, wa_ref, wb_ref, wg_ref, tok_ref, oa_ref, ob_ref, og_ref):
    x = x_ref[...].astype(BF16)
    oa_ref[...] = _mm(x, wa_ref[...]).astype(oa_ref.dtype)
    ob_ref[...] = _mm(x, wb_ref[...])
    og_ref[...] = _mm(x, wg_ref[...])


def _even_in(x2d, wa, wb, wg, token):
    T = x2d.shape[0]
    tm = ROW_TILE
    return pl.pallas_call(
        _even_in_kernel,
        grid=(T // tm,),
        in_specs=[pl.BlockSpec((tm, D_MODEL), lambda i: (i, 0)),
                  _full(wa.shape), _full(wb.shape), _full(wg.shape), _TOKEN_SPEC],
        out_specs=[pl.BlockSpec((tm, wa.shape[1]), lambda i: (i, 0)),
                   pl.BlockSpec((tm, wb.shape[1]), lambda i: (i, 0)),
                   pl.BlockSpec((tm, wg.shape[1]), lambda i: (i, 0))],
        out_shape=[jax.ShapeDtypeStruct((T, wa.shape[1]), BF16),
                   jax.ShapeDtypeStruct((T, wb.shape[1]), F32),
                   jax.ShapeDtypeStruct((T, wg.shape[1]), F32)],
        compiler_params=_params("parallel"),
        cost_estimate=_cost(2 * T * D_MODEL * (wa.shape[1] + wb.shape[1] + wg.shape[1]),
                            T * (4 * D_MODEL + 2 * wa.shape[1] + 4 * wb.shape[1] + 4 * wg.shape[1])),
        name="even_in_proj",
    )(x2d, wa, wb, wg, token)


def _gla_kernel(q_ref, k_ref, v_ref, r_ref, gate_ref, wf_ref, bf_ref, wb_ref, bb_ref, ng_ref,
                o_ref, la_ref, qd_ref, ke_ref, vt_ref, tot_ref, of_ref, ob_ref, *, seq):
    C = GLA_CHUNK
    n_chunks = seq // C
    tile = min(GLA_TILE, seq)
    cpt = tile // C
    gate = gate_ref[...]
    gate_hi = gate.astype(BF16)
    gate_lo = (gate - gate_hi.astype(F32)).astype(BF16)
    for d, (w_ref, b_ref) in enumerate(((wf_ref, bf_ref), (wb_ref, bb_ref))):
        w = w_ref[...]
        w_hi = w.astype(BF16)
        w_lo = (w - w_hi.astype(F32)).astype(BF16)
        z = _mm(gate_hi, w_hi) + _mm(gate_hi, w_lo) + _mm(gate_lo, w_hi) + b_ref[...]
        la_ref[d] = _log_sigmoid(z) * (1.0 / GLA_TAU)

    trow = lax.broadcasted_iota(jnp.int32, (tile, tile), 0)
    tcol = lax.broadcasted_iota(jnp.int32, (tile, tile), 1)
    same_chunk = (trow // C) == (tcol // C)
    keep = (same_chunk & (trow >= tcol), same_chunk & (tcol >= trow))
    rmod = lax.broadcasted_iota(jnp.int32, (tile, LANES), 0) % C
    lane = lax.broadcasted_iota(jnp.int32, (1, LANES), 1)
    head_lane = (lane < GLA_DK, lane >= GLA_DK)
    srow = lax.broadcasted_iota(jnp.int32, (2 * GLA_DV, LANES), 0)
    scol = lax.broadcasted_iota(jnp.int32, (2 * GLA_DV, LANES), 1)
    diag = (srow < GLA_DV) == (scol < GLA_DK)
    scale = GLA_DK ** -0.5

    def chunk_scan(x, d):
        step = 1
        while step < C:
            if d == 0:
                x = x + jnp.where(rmod >= step, pltpu.roll(x, step, 0), 0.0)
            else:
                x = x + jnp.where(rmod < C - step, pltpu.roll(x, tile - step, 0), 0.0)
            step *= 2
        return x

    def intra(j, _):
        r0 = pl.multiple_of(j * tile, tile)
        rows = pl.ds(r0, tile)
        q = q_ref[rows, :].astype(F32) * scale
        k = k_ref[rows, :].astype(F32)
        v = v_ref[rows, :]
        vf = v.astype(F32)
        for c in range(cpt):
            vt_ref[j * cpt + c] = vf[c * C:(c + 1) * C, :].T.astype(BF16)
        for d, out_ref in enumerate((of_ref, ob_ref)):
            la = la_ref[d, rows, :]
            b = chunk_scan(la, d)
            tots = [jnp.sum(la[c * C:(c + 1) * C, :], axis=0, keepdims=True) for c in range(cpt)]
            for c in range(cpt):
                tot_ref[d, pl.ds(j * cpt + c, 1), :] = tots[c]
            tot = jnp.concatenate([jnp.broadcast_to(t, (C, LANES)) for t in tots], axis=0)
            qd = (q * jnp.exp(b)).astype(BF16)
            kd = (k * jnp.exp(-b)).astype(BF16)
            qd_ref[d, rows, :] = qd
            ke_ref[d, rows, :] = (k * jnp.exp(tot - b)).astype(BF16)
            parts = []
            for h in range(2):
                qh = jnp.where(head_lane[h], qd, jnp.zeros_like(qd))
                s = jnp.where(keep[d], _mm_nt(qh, kd), 0.0)
                parts.append(_mm(s.astype(BF16), v[:, h * GLA_DV:(h + 1) * GLA_DV]))
            out_ref[rows, :] = jnp.concatenate(parts, axis=1)
        return 0

    lax.fori_loop(0, seq // tile, intra, 0, unroll=2)

    def inter(i, carry):
        new = []
        for d, (state, out_ref) in enumerate(zip(carry, (of_ref, ob_ref))):
            c = i if d == 0 else n_chunks - 1 - i
            rows = pl.ds(pl.multiple_of(c * C, C), C)
            out_ref[rows, :] += _mm_nt(qd_ref[d, rows, :], state.astype(BF16))
            upd = jnp.where(diag, _mm(vt_ref[c], ke_ref[d, rows, :]), 0.0)
            new.append(jnp.exp(tot_ref[d, pl.ds(c, 1), :]) * state + upd)
        return tuple(new)

    zero = jnp.zeros((2 * GLA_DV, LANES), F32)
    lax.fori_loop(0, n_chunks, inter, (zero, zero), unroll=GLA_UNROLL)

    blk = 256

    def finish(j, _):
        rows = pl.ds(pl.multiple_of(j * blk, blk), blk)
        o = of_ref[rows, :] + ob_ref[rows, :]
        g = ng_ref[...]
        outs = []
        for h in range(2):
            oh = o[:, h * GLA_DV:(h + 1) * GLA_DV]
            ms = jnp.mean(oh * oh, axis=-1, keepdims=True)
            outs.append(oh * lax.rsqrt(ms + RMS_EPS) * g[:, h * GLA_DV:(h + 1) * GLA_DV])
        r = r_ref[rows, :].astype(F32)
        o_ref[rows, :] = (jnp.concatenate(outs, axis=1) * (r * jax.nn.sigmoid(r))).astype(o_ref.dtype)
        return 0

    lax.fori_loop(0, seq // blk, finish, 0)


def _gla(h_a, gate, wf, bf, wb, bb, norm_g, B, S):
    pairs = GLA_HEADS // 2
    kq, kv = 2 * GLA_DK, 2 * GLA_DV
    sq = pl.BlockSpec((None, S, kq), lambda b, g: (b, 0, g))
    sk = pl.BlockSpec((None, S, kq), lambda b, g: (b, 0, pairs + g))
    sv = pl.BlockSpec((None, S, kv), lambda b, g: (b, 0, (2 * GLA_QK) // kv + g))
    sr = pl.BlockSpec((None, S, kv), lambda b, g: (b, 0, (2 * GLA_QK + GLA_VW) // kv + g))
    sg = pl.BlockSpec((None, S, LANES), lambda b, g: (b, 0, 0))
    sw = pl.BlockSpec((LANES, kq), lambda b, g: (0, g))
    sb = pl.BlockSpec((1, kq), lambda b, g: (0, g))
    sn = pl.BlockSpec((1, kv), lambda b, g: (0, g))
    return pl.pallas_call(
        functools.partial(_gla_kernel, seq=S),
        grid=(B, pairs),
        in_specs=[sq, sk, sv, sr, sg, sw, sb, sw, sb, sn],
        out_specs=pl.BlockSpec((None, S, kv), lambda b, g: (b, 0, g)),
        out_shape=jax.ShapeDtypeStruct((B, S, GLA_VW), BF16),
        scratch_shapes=[pltpu.VMEM((2, S, kq), F32),
                        pltpu.VMEM((2, S, kq), BF16),
                        pltpu.VMEM((2, S, kq), BF16),
                        pltpu.VMEM((S // GLA_CHUNK, kv, GLA_CHUNK), BF16),
                        pltpu.VMEM((2, S // GLA_CHUNK, kq), F32),
                        pltpu.VMEM((S, kv), F32), pltpu.VMEM((S, kv), F32)],
        compiler_params=_params("parallel", "parallel"),
        cost_estimate=_cost(B * pairs * S * (12 * LANES * kq + 8 * GLA_TILE * (kq + kv // 2) + 8 * kq * kv),
                            B * S * (2 * 1536 + 4 * pairs * LANES + 2 * GLA_VW), 8 * B * pairs * S * kq),
        name="gla_mixer",
    )(h_a, h_a, h_a, h_a, gate, wf, bf, wb, bb, norm_g)


def _dil_kernel(q_ref, k_ref, v_ref, o_ref, m_ref, l_ref, acc_ref, bias_ref, s_ref, *, seq):
    R = DIL_RADIUS
    pair = pl.program_id(1)
    lane = lax.broadcasted_iota(jnp.int32, (1, LANES), 1)
    first_head = lane < DIL_DH
    scale = DIL_DH ** -0.5

    for p_idx, (window, dil) in enumerate(DIL_PATTERNS):
        assert window // (2 * dil) == R
        length = seq // dil
        tq = min(DIL_TQ, length)
        nk = tq + 2 * R
        tpr = length // tq
        shift = tpr.bit_length() - 1
        assert tpr == 1 << shift and tq % R == 0
        qi = lax.broadcasted_iota(jnp.int32, (tq, nk), 0)
        kj = lax.broadcasted_iota(jnp.int32, (tq, nk), 1)
        dist = jnp.abs(kj - R - qi)
        for h in range(2):
            slope = 1.0 / jnp.left_shift(jnp.ones((tq, nk), jnp.int32), 2 * pair + h + 1).astype(F32)
            bias_ref[h, :tq, :nk] = jnp.where(dist <= R, -(dil * dist).astype(F32) * slope, MASK_VALUE)
        ones = jnp.ones((nk, LANES), BF16)

        def rows_at(start, n, dil=dil):
            return pl.ds(start, n, stride=dil) if dil > 1 else pl.ds(pl.multiple_of(start, R), n)

        n_tiles = seq // tq
        assert n_tiles % 2 == 0

        def tile_pos(idx, dil=dil, tq=tq, tpr=tpr, shift=shift):
            res = lax.shift_right_logical(idx, shift)
            t = jnp.bitwise_and(idx, tpr - 1)
            start = res + dil * tq * t
            has_prev = t > 0
            has_next = t < tpr - 1
            prev = jnp.where(has_prev, start - dil * R, start)
            nxt = jnp.where(has_next, start + dil * tq, start)
            return start, prev, nxt, has_prev, has_next

        def scores(idx, slot, tq=tq, nk=nk, rows_at=rows_at, tile_pos=tile_pos):
            start, prev, nxt, _, _ = tile_pos(idx)
            q = (q_ref[rows_at(start, tq), :] * scale).astype(BF16)
            kcat = jnp.concatenate([k_ref[rows_at(prev, R), :], k_ref[rows_at(start, tq), :],
                                    k_ref[rows_at(nxt, R), :]], axis=0).astype(BF16)
            for h in range(2):
                qh = jnp.where(first_head if h == 0 else ~first_head, q, jnp.zeros_like(q))
                s_ref[slot, h, :tq, :nk] = _mm_nt(qh, kcat)

        def values(idx, slot, tq=tq, nk=nk, p_idx=p_idx, rows_at=rows_at, kj=kj, ones=ones,
                   tile_pos=tile_pos):
            start, prev, nxt, has_prev, has_next = tile_pos(idx)
            vcat = jnp.concatenate([v_ref[rows_at(prev, R), :], v_ref[rows_at(start, tq), :],
                                    v_ref[rows_at(nxt, R), :]], axis=0).astype(BF16)
            vones = jnp.concatenate([vcat, ones], axis=1)
            in_seq = ((kj >= R) | has_prev) & ((kj < R + tq) | has_next)
            ms, pvs = [], []
            for h in range(2):
                s = jnp.where(in_seq, s_ref[slot, h, :tq, :nk] + bias_ref[h, :tq, :nk], MASK_VALUE)
                m = jnp.max(s, axis=-1, keepdims=True)
                ms.append(m)
                pvs.append(_mm(jnp.exp(s - m).astype(BF16), vones))
            out_rows = rows_at(start, tq)
            m_ref[p_idx, out_rows, :] = jnp.where(first_head, ms[0], ms[1])
            l_ref[p_idx, out_rows, :] = jnp.where(first_head, pvs[0][:, LANES:], pvs[1][:, LANES:])
            acc_ref[p_idx, out_rows, :] = jnp.where(first_head, pvs[0][:, :LANES], pvs[1][:, :LANES])

        scores(0, 0)

        def body(i2, _, n_tiles=n_tiles, scores=scores, values=values):
            for u in range(2):
                idx = 2 * i2 + u
                scores(jnp.minimum(idx + 1, n_tiles - 1), 1 - u)
                values(idx, u)
            return 0

        lax.fori_loop(0, n_tiles // 2, body, 0, unroll=DIL_UNROLL)

    blk = 256
    n_pat = len(DIL_PATTERNS)

    def finish(j, _):
        rows = pl.ds(pl.multiple_of(j * blk, blk), blk)
        ms = [m_ref[p, rows, :] for p in range(n_pat)]
        m = functools.reduce(jnp.maximum, ms)
        ws = [jnp.exp(x - m) for x in ms]
        l = functools.reduce(lambda a, b: a + b, [w * l_ref[p, rows, :] for p, w in enumerate(ws)])
        acc = functools.reduce(lambda a, b: a + b, [w * acc_ref[p, rows, :] for p, w in enumerate(ws)])
        o_ref[rows, :] = (acc / l).astype(o_ref.dtype)
        return 0

    lax.fori_loop(0, seq // blk, finish, 0)


def _dilated(h_b, B, S):
    pairs = DIL_HEADS // 2
    sq = pl.BlockSpec((None, S, LANES), lambda b, g: (b, 0, g))
    sk = pl.BlockSpec((None, S, LANES), lambda b, g: (b, 0, pairs + g))
    sv = pl.BlockSpec((None, S, LANES), lambda b, g: (b, 0, 2 * pairs + g))
    return pl.pallas_call(
        functools.partial(_dil_kernel, seq=S),
        grid=(B, pairs),
        in_specs=[sq, sk, sv],
        out_specs=pl.BlockSpec((None, S, LANES), lambda b, g: (b, 0, g)),
        out_shape=jax.ShapeDtypeStruct((B, S, DIL_W), BF16),
        scratch_shapes=[pltpu.VMEM((len(DIL_PATTERNS), S, LANES), F32)] * 3
                       + [pltpu.VMEM((2, DIL_TQ, DIL_TQ + 2 * DIL_RADIUS), F32),
                          pltpu.VMEM((2, 2, DIL_TQ, DIL_TQ + 2 * DIL_RADIUS), F32)],
        compiler_params=_params("parallel", "parallel"),
        cost_estimate=_cost(B * DIL_HEADS * S * len(DIL_PATTERNS) * 2 * (DIL_TQ + 2 * DIL_RADIUS) * 3 * LANES,
                            B * S * (4 * 3 * DIL_W + 2 * DIL_W),
                            B * DIL_HEADS * S * len(DIL_PATTERNS) * (DIL_TQ + 2 * DIL_RADIUS)),
        name="dilated_mixer",
    )(h_b, h_b, h_b)


def _route(lt):
    g = [lt[i:i + 1, :] for i in range(MOE_GROUPS)]
    gmax = functools.reduce(jnp.maximum, g)
    gexp = [jnp.exp(x - gmax) for x in g]
    gsum = functools.reduce(lambda a, b: a + b, gexp)
    gprob = [x / gsum for x in gexp]
    g_val, g_idx = gprob[0], jnp.zeros_like(gprob[0], dtype=jnp.int32)
    for i in range(1, MOE_GROUPS):
        better = gprob[i] > g_val
        g_idx = jnp.where(better, i, g_idx)
        g_val = jnp.where(better, gprob[i], g_val)
    e = []
    for j in range(MOE_EPG):
        x = lt[MOE_GROUPS + j:MOE_GROUPS + j + 1, :]
        for gi in range(1, MOE_GROUPS):
            r0 = MOE_GROUPS + gi * MOE_EPG + j
            x = jnp.where(g_idx == gi, lt[r0:r0 + 1, :], x)
        e.append(x)
    emax = functools.reduce(jnp.maximum, e)
    eexp = [jnp.exp(x - emax) for x in e]
    esum = functools.reduce(lambda a, b: a + b, eexp)
    eprob = [x / esum for x in eexp]
    v1, i1 = eprob[0], jnp.zeros_like(g_idx)
    for j in range(1, MOE_EPG):
        better = eprob[j] > v1
        i1 = jnp.where(better, j, i1)
        v1 = jnp.where(better, eprob[j], v1)
    v2, i2 = jnp.full_like(v1, -1.0), jnp.zeros_like(g_idx)
    for j in range(MOE_EPG):
        cand = jnp.where(i1 == j, -1.0, eprob[j])
        better = cand > v2
        i2 = jnp.where(better, j, i2)
        v2 = jnp.where(better, cand, v2)
    den = v1 + v2
    ids = jnp.concatenate([g_idx * MOE_EPG + i1, g_idx * MOE_EPG + i2], axis=0)
    wts = jnp.concatenate([g_val * (v1 / den), g_val * (v2 / den)], axis=0)
    return ids, wts


def _proj_ln_route_kernel(*refs, n_in):
    x_ref = refs[0]
    o_refs = refs[1:1 + n_in]
    w_refs = refs[1 + n_in:1 + 2 * n_in]
    g_ref, b_ref, wr_ref, br_ref, x1_ref, x1b_ref, ids_ref, wts_ref = refs[1 + 2 * n_in:]
    m = _mm(o_refs[0][...], w_refs[0][...])
    for o_ref, w_ref in zip(o_refs[1:], w_refs[1:]):
        m = m + _mm(o_ref[...], w_ref[...])
    x1 = _layer_norm(ALPHA * x_ref[...] + m, g_ref[...], b_ref[...])
    x1_ref[...] = x1
    x_hi = x1.astype(BF16)
    x1b_ref[...] = x_hi
    x_lo = (x1 - x_hi.astype(F32)).astype(BF16)
    wr = wr_ref[...]
    w_hi = wr.astype(BF16)
    w_lo = (wr - w_hi.astype(F32)).astype(BF16)
    lt = _mm_nt(w_hi, x_hi) + _mm_nt(w_hi, x_lo) + _mm_nt(w_lo, x_hi) + br_ref[...]
    ids, wts = _route(lt)
    ids_ref[...] = ids
    wts_ref[...] = wts


def _proj_ln_route(x2d, outs, ws, ln_g, ln_b, wr, br):
    T = x2d.shape[0]
    tm = ROW_TILE
    n_in = len(outs)
    row = lambda w: pl.BlockSpec((tm, w), lambda i: (i, 0))
    tok = pl.BlockSpec((2, tm), lambda i: (0, i))
    return pl.pallas_call(
        functools.partial(_proj_ln_route_kernel, n_in=n_in),
        grid=(T // tm,),
        in_specs=[row(D_MODEL)] + [row(o.shape[1]) for o in outs] + [_full(w.shape) for w in ws]
                 + [_full(ln_g.shape), _full(ln_b.shape), _full(wr.shape), _full(br.shape)],
        out_specs=[row(D_MODEL), row(D_MODEL), tok, tok],
        out_shape=[jax.ShapeDtypeStruct((T, D_MODEL), F32), jax.ShapeDtypeStruct((T, D_MODEL), BF16),
                   jax.ShapeDtypeStruct((2, T), jnp.int32), jax.ShapeDtypeStruct((2, T), F32)],
        compiler_params=_params("parallel"),
        cost_estimate=_cost(2 * T * D_MODEL * (sum(o.shape[1] for o in outs) + 3 * 32),
                            T * (4 * D_MODEL + 2 * sum(o.shape[1] for o in outs) + 6 * D_MODEL + 16)),
        name="proj_ln_route",
    )(x2d, *outs, *ws, ln_g, ln_b, wr, br)


def _moe_kernel(te_ref, nu_ref, xs_ref, w1_ref, w3_ref, w2_ref, rw_ref, tok_ref, ys_ref, w13_s, w2_s):
    i = pl.program_id(0)

    @pl.when((i == 0) | (te_ref[i] != te_ref[jnp.maximum(i - 1, 0)]))
    def _():
        w13_s[:, :MOE_FF] = w1_ref[...].astype(BF16)
        w13_s[:, MOE_FF:] = w3_ref[...].astype(BF16)
        w2_s[...] = w2_ref[...].astype(BF16)

    @pl.when(i < nu_ref[0])
    def _():
        h = _mm(xs_ref[...], w13_s[...])
        h1, h3 = h[:, :MOE_FF], h[:, MOE_FF:]
        hidden = (h1 * jax.nn.sigmoid(h1)) * h3
        y = _mm(hidden.astype(BF16), w2_s[...])
        ys_ref[...] = (y * rw_ref[...]).astype(ys_ref.dtype)

    @pl.when(i >= nu_ref[0])
    def _():
        ys_ref[...] = jnp.zeros_like(ys_ref)


def _moe_experts(xs, w1, w3, w2, layer, row_w, tile_expert, n_used, token):
    P = xs.shape[0]
    tm = MOE_TILE
    grid_spec = pltpu.PrefetchScalarGridSpec(
        num_scalar_prefetch=2,
        grid=(P // tm,),
        in_specs=[pl.BlockSpec((tm, D_MODEL), lambda i, te, nu: (i, 0)),
                  pl.BlockSpec((None, None, D_MODEL, MOE_FF), lambda i, te, nu: (layer, te[i], 0, 0)),
                  pl.BlockSpec((None, None, D_MODEL, MOE_FF), lambda i, te, nu: (layer, te[i], 0, 0)),
                  pl.BlockSpec((None, None, MOE_FF, D_MODEL), lambda i, te, nu: (layer, te[i], 0, 0)),
                  pl.BlockSpec((tm, 1), lambda i, te, nu: (i, 0)), _TOKEN_SPEC],
        out_specs=pl.BlockSpec((tm, D_MODEL), lambda i, te, nu: (i, 0)),
        scratch_shapes=[pltpu.VMEM((D_MODEL, 2 * MOE_FF), BF16), pltpu.VMEM((MOE_FF, D_MODEL), BF16)],
    )
    return pl.pallas_call(
        _moe_kernel,
        grid_spec=grid_spec,
        out_shape=jax.ShapeDtypeStruct((P, D_MODEL), BF16),
        compiler_params=_params("arbitrary"),
        cost_estimate=_cost(6 * P * D_MODEL * MOE_FF, 4 * P * D_MODEL + 12 * MOE_EXPERTS * D_MODEL * MOE_FF,
                            P * MOE_FF),
        name="moe_experts",
    )(tile_expert, n_used, xs, w1, w3, w2, row_w, token)


def _moe_plan(ids, wts, T):
    tm = MOE_TILE
    n_assign = 2 * T
    e_flat = ids.reshape(n_assign)
    onehot = (e_flat[:, None] == jnp.arange(MOE_EXPERTS, dtype=jnp.int32)[None, :]).astype(jnp.int32)
    csum = jnp.cumsum(onehot, axis=0)
    counts = csum[-1]
    padded = ((counts + tm - 1) // tm) * tm
    ends = jnp.cumsum(padded)
    offs = ends - padded
    pos = jnp.sum(onehot * (csum - onehot + offs[None, :]), axis=1)
    P = n_assign + MOE_EXPERTS * tm
    n_tiles = P // tm
    tile_start = jnp.arange(n_tiles, dtype=jnp.int32) * tm
    tile_expert = jnp.minimum(jnp.sum((tile_start[:, None] >= ends[None, :]).astype(jnp.int32), axis=1),
                              MOE_EXPERTS - 1).astype(jnp.int32)
    n_used = (ends[-1:] // tm).astype(jnp.int32)
    tok = jnp.arange(n_assign, dtype=jnp.int32) % T
    _, tok_sorted, w_sorted = lax.sort((e_flat, tok, wts.reshape(n_assign)), num_keys=1, is_stable=True)
    t_onehot = (tile_expert[:, None] == jnp.arange(MOE_EXPERTS, dtype=jnp.int32)[None, :]).astype(jnp.int32)
    t_rank0 = tile_start - jnp.sum(t_onehot * offs[None, :], axis=1)
    t_count = jnp.sum(t_onehot * counts[None, :], axis=1)
    t_first = jnp.sum(t_onehot * (jnp.cumsum(counts) - counts)[None, :], axis=1)
    r = t_rank0[:, None] + jnp.arange(tm, dtype=jnp.int32)[None, :]
    valid = (r < t_count[:, None]).reshape(P)
    u = jnp.clip(t_first[:, None] + r, 0, n_assign - 1).reshape(P)
    src_tok = jnp.where(valid, tok_sorted.at[u].get(mode="promise_in_bounds"), 0)
    row_w = jnp.where(valid, w_sorted.at[u].get(mode="promise_in_bounds"), 0.0)
    return src_tok, row_w.reshape(P, 1), tile_expert, n_used, pos


def _combine_ln_kernel(x_ref, y0_ref, y1_ref, g_ref, b_ref, tok_ref, o_ref):
    y = ALPHA * x_ref[...] + y0_ref[...].astype(F32) + y1_ref[...].astype(F32)
    o_ref[...] = _layer_norm(y, g_ref[...], b_ref[...])


def _combine_ln(x2d, y0, y1, ln_g, ln_b, token):
    T = x2d.shape[0]
    tm = ROW_TILE
    row = pl.BlockSpec((tm, D_MODEL), lambda i: (i, 0))
    return pl.pallas_call(
        _combine_ln_kernel,
        grid=(T // tm,),
        in_specs=[row, row, row, _full(ln_g.shape), _full(ln_b.shape), _TOKEN_SPEC],
        out_specs=row,
        out_shape=jax.ShapeDtypeStruct((T, D_MODEL), F32),
        compiler_params=_params("parallel"),
        cost_estimate=_cost(12 * T * D_MODEL, 12 * T * D_MODEL),
        name="combine_ln",
    )(x2d, y0, y1, ln_g, ln_b, token)


def _experts_stage(i, routed, prm, token):
    _, x1b, (src_tok, row_w, tile_expert, n_used, _) = routed
    xs = x1b.at[src_tok].get(mode="promise_in_bounds")
    return _moe_experts(xs, prm["moe_w1"], prm["moe_w3"], prm["moe_w2"], i, row_w, tile_expert, n_used, token)


def _combine_stage(i, routed, ys, prm, token):
    x1, _, plan = routed
    T = x1.shape[0]
    pos = plan[4]
    y0 = ys.at[pos[:T]].get(mode="promise_in_bounds")
    y1 = ys.at[pos[T:]].get(mode="promise_in_bounds")
    return _combine_ln(x1, y0, y1, prm["ln2_g"][i:i + 1], prm["ln2_b"][i:i + 1], token)


def _rope_block(x, c, sa, sb):
    return x * c + pltpu.roll(x, LANES - 32, 1) * sa + pltpu.roll(x, 32, 1) * sb


def _mla_in_kernel(x_ref, win_ref, qg_ref, kvg_ref, wq_ref, wk_ref, wv_ref, c_ref, sa_ref, sb_ref,
                   tok_ref, q_ref, k_ref, v_ref):
    x = x_ref[...].astype(BF16)
    h = _mm(x, win_ref[...])
    c_q = h[:, :MLA_Q_LORA]
    c_kv = h[:, MLA_Q_LORA:MLA_Q_LORA + MLA_KV_LORA]
    k_rope = h[:, MLA_Q_LORA + MLA_KV_LORA:]
    cqn = c_q * lax.rsqrt(jnp.mean(c_q * c_q, axis=-1, keepdims=True) + RMS_EPS) * qg_ref[...]
    ckn = c_kv * lax.rsqrt(jnp.mean(c_kv * c_kv, axis=-1, keepdims=True) + RMS_EPS) * kvg_ref[...]
    cqn, ckn = cqn.astype(BF16), ckn.astype(BF16)
    scale = (MLA_NOPE + MLA_ROPE) ** -0.5 * LOG2_E
    q = _mm(cqn, wq_ref[...]) * scale
    kn = _mm(ckn, wk_ref[...])
    v = _mm(ckn, wv_ref[...]).astype(v_ref.dtype)
    c, sa, sb = c_ref[...], sa_ref[...], sb_ref[...]
    kr = _rope_block(k_rope, c, sa, sb).astype(k_ref.dtype)
    ones = jnp.ones((x.shape[0], LANES), v_ref.dtype)
    for g in range(MLA_HEADS // 2):
        lo = 2 * LANES * g
        v_ref[:, lo:lo + LANES] = v[:, LANES * g:LANES * (g + 1)]
        v_ref[:, lo + LANES:lo + 2 * LANES] = ones
        q_ref[:, lo:lo + LANES] = q[:, lo:lo + LANES].astype(q_ref.dtype)
        q_ref[:, lo + LANES:lo + 2 * LANES] = _rope_block(q[:, lo + LANES:lo + 2 * LANES], c, sa, sb
                                                          ).astype(q_ref.dtype)
        k_ref[:, lo:lo + LANES] = kn[:, LANES * g:LANES * (g + 1)].astype(k_ref.dtype)
        k_ref[:, lo + LANES:lo + 2 * LANES] = kr


def _mla_in(x2d, win, qg, kvg, wq, wk, wv, c, sa, sb, S, token):
    T = x2d.shape[0]
    tm = ROW_TILE
    row = lambda w: pl.BlockSpec((tm, w), lambda i: (i, 0))
    n_s = S // tm
    tab = pl.BlockSpec((tm, LANES), lambda i: (i % n_s, 0))
    pairs = MLA_HEADS // 2
    return pl.pallas_call(
        _mla_in_kernel,
        grid=(T // tm,),
        in_specs=[row(D_MODEL), _full(win.shape), _full(qg.shape), _full(kvg.shape), _full(wq.shape),
                  _full(wk.shape), _full(wv.shape), tab, tab, tab, _TOKEN_SPEC],
        out_specs=[row(pairs * 2 * LANES)] * 3,
        out_shape=[jax.ShapeDtypeStruct((T, pairs * 2 * LANES), BF16)] * 3,
        compiler_params=_params("parallel"),
        cost_estimate=_cost(2 * T * (D_MODEL * win.shape[1] + MLA_Q_LORA * wq.shape[1]
                                     + MLA_KV_LORA * (wk.shape[1] + wv.shape[1])),
                            T * (4 * D_MODEL + 3 * 2 * pairs * 2 * LANES)),
        name="mla_in_proj",
    )(x2d, win, qg, kvg, wq, wk, wv, c, sa, sb, token)


def _mla_attn_kernel(q_ref, k_ref, v_ref, o_ref, sa_ref, sb_ref, *, seq):
    lane2 = lax.broadcasted_iota(jnp.int32, (1, 2 * LANES), 1)
    rl = lane2 - LANES
    in_a = (lane2 < MLA_NOPE) | ((rl >= 0) & (rl < 16)) | ((rl >= 32) & (rl < 48))
    in_b = ((lane2 >= MLA_NOPE) & (lane2 < LANES)) | ((rl >= 16) & (rl < 32)) | ((rl >= 48) & (rl < 64))
    lane = lax.broadcasted_iota(jnp.int32, (1, LANES), 1)
    q = q_ref[...]
    tq = q.shape[0]
    tk = min(ATT_TK, seq)
    n_k = seq // tk
    q_heads = (jnp.where(in_a, q, jnp.zeros_like(q)), jnp.where(in_b, q, jnp.zeros_like(q)))
    s_refs = (sa_ref, sb_ref)

    def score_chunk(h, j, m):
        s = _mm_nt(q_heads[h], k_ref[j * tk:(j + 1) * tk, :])
        s_refs[h][:, j * tk:(j + 1) * tk] = s
        for c in range(tk // LANES):
            m = jnp.maximum(m, s[:, c * LANES:(c + 1) * LANES])
        return m

    def value_chunk(h, j, m_row, acc):
        p = jnp.exp2(s_refs[h][:, j * tk:(j + 1) * tk] - m_row).astype(BF16)
        return acc + _mm(p, v_ref[j * tk:(j + 1) * tk, :])

    neg = jnp.full((tq, LANES), -jnp.inf, F32)
    m = neg
    for j in range(n_k):
        m = score_chunk(0, j, m)
    m_a = jnp.max(m, axis=-1, keepdims=True)
    acc_a = jnp.zeros((tq, 2 * LANES), F32)
    m = neg
    for j in range(n_k):
        acc_a = value_chunk(0, j, m_a, acc_a)
        m = score_chunk(1, j, m)
    m_b = jnp.max(m, axis=-1, keepdims=True)
    acc_b = jnp.zeros((tq, 2 * LANES), F32)
    for j in range(n_k):
        acc_b = value_chunk(1, j, m_b, acc_b)
    out_a = acc_a[:, :LANES] / acc_a[:, LANES:]
    out_b = acc_b[:, :LANES] / acc_b[:, LANES:]
    o_ref[...] = jnp.where(lane < MLA_V, out_a, out_b).astype(o_ref.dtype)


def _mla_attn(q, k, v, B, S):
    pairs = MLA_HEADS // 2
    tq = min(ATT_TQ, S)
    return pl.pallas_call(
        functools.partial(_mla_attn_kernel, seq=S),
        grid=(B, pairs, S // tq),
        in_specs=[pl.BlockSpec((None, tq, 2 * LANES), lambda b, g, i: (b, i, g)),
                  pl.BlockSpec((None, S, 2 * LANES), lambda b, g, i: (b, 0, g)),
                  pl.BlockSpec((None, S, 2 * LANES), lambda b, g, i: (b, 0, g))],
        out_specs=pl.BlockSpec((None, tq, LANES), lambda b, g, i: (b, i, g)),
        out_shape=jax.ShapeDtypeStruct((B, S, pairs * LANES), BF16),
        scratch_shapes=[pltpu.VMEM((tq, S), F32), pltpu.VMEM((tq, S), F32)],
        compiler_params=_params("parallel", "parallel", "arbitrary"),
        cost_estimate=_cost(B * MLA_HEADS * S * S * 2 * 2 * 2 * LANES,
                            B * S * pairs * LANES * (3 * 2 * 2 + 2), B * MLA_HEADS * S * S),
        name="mla_attention",
    )(q, k, v)


def _prep_even(w_in, wa_f, ba_f, wa_b, ba_b, norm_g, w_out):
    o_q, o_k, o_v, o_r = 0, GLA_QK, 2 * GLA_QK, 2 * GLA_QK + GLA_VW
    o_af = o_r + GLA_VW
    o_ab = o_af + GLA_RANK
    o_qb = o_ab + GLA_RANK
    wa = w_in[:, :o_af].astype(BF16)
    wb = w_in[:, o_qb:].astype(BF16)
    wg = jnp.zeros((D_MODEL, LANES), F32).at[:, :2 * GLA_RANK].set(w_in[:, o_af:o_qb]).astype(BF16)
    wf = jnp.zeros((LANES, GLA_QK), F32).at[:GLA_RANK].set(wa_f)
    wb_gate = jnp.zeros((LANES, GLA_QK), F32).at[GLA_RANK:2 * GLA_RANK].set(wa_b)
    return dict(wa=wa, wb=wb, wg=wg, wf=wf, bf=ba_f.reshape(1, -1), wb_gate=wb_gate,
                bb=ba_b.reshape(1, -1), norm_g=norm_g.reshape(1, -1),
                wo_a=w_out[:GLA_VW].astype(BF16), wo_b=w_out[GLA_VW:].astype(BF16))


def _prep_odd(w_in, q_norm, kv_norm, w_uq, w_ukv, w_out, S):
    half = MLA_ROPE // 2
    pairs = MLA_HEADS // 2
    kr = w_in[:, MLA_Q_LORA + MLA_KV_LORA:]
    kr_rep = jnp.concatenate([kr[:, :half], kr[:, :half], kr[:, half:], kr[:, half:],
                              jnp.zeros((D_MODEL, LANES - 2 * MLA_ROPE), F32)], axis=1)
    win = jnp.concatenate([w_in[:, :MLA_Q_LORA + MLA_KV_LORA], kr_rep], axis=1).astype(BF16)
    uq = w_uq.reshape(MLA_Q_LORA, MLA_HEADS, MLA_NOPE + MLA_ROPE)
    nope = uq[:, :, :MLA_NOPE].reshape(MLA_Q_LORA, pairs, 2 * MLA_NOPE)
    r1 = uq[:, :, MLA_NOPE:MLA_NOPE + half].reshape(MLA_Q_LORA, pairs, 2 * half)
    r2 = uq[:, :, MLA_NOPE + half:].reshape(MLA_Q_LORA, pairs, 2 * half)
    pad = jnp.zeros((MLA_Q_LORA, pairs, LANES - 2 * MLA_ROPE), F32)
    wq = jnp.concatenate([nope, r1, r2, pad], axis=2).reshape(MLA_Q_LORA, pairs * 2 * LANES).astype(BF16)
    ukv = w_ukv.reshape(MLA_KV_LORA, MLA_HEADS, MLA_NOPE + MLA_V)
    wk = ukv[:, :, :MLA_NOPE].reshape(MLA_KV_LORA, MLA_HEADS * MLA_NOPE).astype(BF16)
    wv = ukv[:, :, MLA_NOPE:].reshape(MLA_KV_LORA, MLA_HEADS * MLA_V).astype(BF16)
    inv = 1.0 / (ROPE_BASE ** (jnp.arange(0, MLA_ROPE, 2, dtype=F32) / MLA_ROPE))
    ang = jnp.arange(S, dtype=F32)[:, None] * inv[None, :]
    cos, sin = jnp.cos(ang), jnp.sin(ang)
    z16 = jnp.zeros((S, half), F32)
    z64 = jnp.zeros((S, LANES - 2 * MLA_ROPE), F32)
    c = jnp.concatenate([cos, cos, cos, cos, z64], axis=1)
    sa = jnp.concatenate([-sin, -sin, z16, z16, z64], axis=1)
    sb = jnp.concatenate([z16, z16, sin, sin, z64], axis=1)
    return dict(win=win, qg=q_norm.reshape(1, -1), kvg=kv_norm.reshape(1, -1), wq=wq, wk=wk, wv=wv,
                c=c, sa=sa, sb=sb, wo=w_out.astype(BF16))


def _prep_moe(wg, bg, we, be):
    wr = jnp.zeros((32, D_MODEL), F32).at[:MOE_GROUPS].set(wg.T).at[MOE_GROUPS:MOE_GROUPS + MOE_EXPERTS].set(we.T)
    br = jnp.zeros((32, 1), F32).at[:MOE_GROUPS, 0].set(bg).at[MOE_GROUPS:MOE_GROUPS + MOE_EXPERTS, 0].set(be)
    return dict(wr=wr, br=br)


def _mix_route(i, x2d, B, S, prm, token):
    ev, od, moe, ln1_g, ln1_b = prm["ev"], prm["od"], prm["moe"], prm["ln1_g"], prm["ln1_b"]
    T = B * S
    if i % 2 == 0:
        p = ev[i // 2]
        h_a, h_b, gate = _even_in(x2d, p["wa"], p["wb"], p["wg"], token)
        o_a = _gla(h_a.reshape(B, S, -1), gate.reshape(B, S, -1), p["wf"], p["bf"], p["wb_gate"],
                   p["bb"], p["norm_g"], B, S)
        o_b = _dilated(h_b.reshape(B, S, -1), B, S)
        outs = [o_a.reshape(T, -1), o_b.reshape(T, -1)]
        ws = [p["wo_a"], p["wo_b"]]
    else:
        p = od[i // 2]
        q, k, v = _mla_in(x2d, p["win"], p["qg"], p["kvg"], p["wq"], p["wk"], p["wv"],
                          p["c"], p["sa"], p["sb"], S, token)
        o = _mla_attn(q.reshape(B, S, -1), k.reshape(B, S, -1), v.reshape(B, S, -1), B, S)
        outs = [o.reshape(T, -1)]
        ws = [p["wo"]]
    m = moe[i]
    x1, x1b, ids, wts = _proj_ln_route(x2d, outs, ws, ln1_g[i:i + 1], ln1_b[i:i + 1], m["wr"], m["br"])
    return x1, x1b, _moe_plan(ids, wts, T)


def _plan_token(routed):
    return routed[2][0]


def _rows_token(rows):
    return rows[:8, :LANES]


def _trunk_pair(xa, xb, prm):
    (Ba, S, D), Bb = xa.shape, xb.shape[0]
    a = xa.reshape(Ba * S, D)
    b = xb.reshape(Bb * S, D)
    tok = jnp.zeros((1,), jnp.int32)
    for i in range(DEPTH):
        routed_a = _mix_route(i, a, Ba, S, prm, tok)
        if i > 0:
            b = _combine_stage(i - 1, routed_b, ys_b, prm, _plan_token(routed_a))
        routed_b = _mix_route(i, b, Bb, S, prm, _plan_token(routed_a))
        ys_a = _experts_stage(i, routed_a, prm, _plan_token(routed_b))
        ys_b = _experts_stage(i, routed_b, prm, _rows_token(ys_a))
        a = _combine_stage(i, routed_a, ys_a, prm, _rows_token(ys_b))
        tok = _rows_token(a)
    b = _combine_stage(DEPTH - 1, routed_b, ys_b, prm, tok)
    return a.reshape(Ba, S, D), b.reshape(Bb, S, D)


def kernel(x_prompt, x_sample, ev_w_in, ev_wa_f, ev_ba_f, ev_wa_b, ev_ba_b, ev_gla_norm, ev_w_out,
           od_w_in, od_q_norm, od_kv_norm, od_w_uq, od_w_ukv, od_w_out, ln1_g, ln1_b, ln2_g, ln2_b,
           moe_wg, moe_bg, moe_we, moe_be, moe_w1, moe_w3, moe_w2):
    S = x_prompt.shape[1]
    ev = [_prep_even(ev_w_in[j], ev_wa_f[j], ev_ba_f[j], ev_wa_b[j], ev_ba_b[j], ev_gla_norm[j], ev_w_out[j])
          for j in range(ev_w_in.shape[0])]
    od = [_prep_odd(od_w_in[j], od_q_norm[j], od_kv_norm[j], od_w_uq[j], od_w_ukv[j], od_w_out[j], S)
          for j in range(od_w_in.shape[0])]
    moe = [_prep_moe(moe_wg[i], moe_bg[i], moe_we[i], moe_be[i]) for i in range(DEPTH)]
    prm = dict(ev=ev, od=od, moe=moe, ln1_g=ln1_g, ln1_b=ln1_b, ln2_g=ln2_g, ln2_b=ln2_b,
               moe_w1=moe_w1, moe_w3=moe_w3, moe_w2=moe_w2)
    return _trunk_pair(x_prompt, x_sample, prm)
```

```python
import functools

import numpy as np
import jax
import jax.numpy as jnp
from jax import lax
from jax.experimental import pallas as pl
from jax.experimental.pallas import tpu as pltpu

F32 = jnp.float32
BF16 = jnp.bfloat16
HIGHEST = lax.Precision.HIGHEST

D_MODEL = 1024
DEPTH = 2
GLA_HEADS, GLA_DK, GLA_DV = 4, 64, 128
GLA_QK, GLA_VW = GLA_HEADS * GLA_DK, GLA_HEADS * GLA_DV
GLA_RANK, GLA_TAU, GLA_CHUNK = 16, 16.0, 64
DIL_HEADS, DIL_DH = 8, 64
DIL_W = DIL_HEADS * DIL_DH
DIL_PATTERNS = ((128, 1), (512, 4), (2048, 16))
DIL_RADIUS = 64
MASK_VALUE = -1e30
MLA_HEADS, MLA_NOPE, MLA_ROPE, MLA_V = 16, 64, 32, 64
MLA_Q_LORA, MLA_KV_LORA = 384, 128
ROPE_BASE = 10000.0
MOE_GROUPS, MOE_EPG, MOE_EXPERTS, MOE_FF = 4, 4, 16, 512
ALPHA = (2 * DEPTH) ** 0.25
LN_EPS = 1e-5
RMS_EPS = 1e-6
LOG2_E = 1.4426950408889634

LANES = 128
VMEM_LIMIT = 56 * 1024 * 1024
ROW_TILE = 512
MOE_TILE = 512
ATT_TQ = 512
ATT_TK = 512
DIL_UNROLL = 2
GLA_UNROLL = 4
GLA_TILE = 256
DIL_TQ = 256


def _params(*sem):
    return pltpu.CompilerParams(dimension_semantics=sem, vmem_limit_bytes=VMEM_LIMIT)


def _full(shape):
    n = len(shape)
    return pl.BlockSpec(shape, lambda *_: (0,) * n)


_TOKEN_SPEC = pl.BlockSpec(memory_space=pl.ANY)


def _cost(flops, bytes_accessed, transcendentals=0):
    return pl.CostEstimate(flops=int(flops), bytes_accessed=int(bytes_accessed),
                           transcendentals=int(transcendentals))


def _mm(a, b):
    return jnp.dot(a, b, preferred_element_type=F32)


def _mm_nt(a, b):
    return lax.dot_general(a, b, (((1,), (1,)), ((), ())), preferred_element_type=F32)


def _mm_tn(a, b):
    return lax.dot_general(a, b, (((0,), (0,)), ((), ())), preferred_element_type=F32)


def _layer_norm(y, g, b):
    mu = jnp.mean(y, axis=-1, keepdims=True)
    yc = y - mu
    var = jnp.mean(yc * yc, axis=-1, keepdims=True)
    return yc * lax.rsqrt(var + LN_EPS) * g + b


def _log_sigmoid(z):
    return jnp.minimum(z, 0.0) - jnp.log(1.0 + jnp.exp(-jnp.abs(z)))


def _even_in_kernel(x_ref, wa_ref, wb_ref, wg_ref, tok_ref, oa_ref, ob_ref, og_ref):
    x = x_ref[...].astype(BF16)
    oa_ref[...] = _mm(x, wa_ref[...]).astype(oa_ref.dtype)
    ob_ref[...] = _mm(x, wb_ref[...])
    og_ref[...] = _mm(x, wg_ref[...])


def _even_in(x2d, wa, wb, wg, token):
    T = x2d.shape[0]
    tm = ROW_TILE
    return pl.pallas_call(
        _even_in_kernel,
        grid=(T // tm,),
        in_specs=[pl.BlockSpec((tm, D_MODEL), lambda i: (i, 0)),
                  _full(wa.shape), _full(wb.shape), _full(wg.shape), _TOKEN_SPEC],
        out_specs=[pl.BlockSpec((tm, wa.shape[1]), lambda i: (i, 0)),
                   pl.BlockSpec((tm, wb.shape[1]), lambda i: (i, 0)),
                   pl.BlockSpec((tm, wg.shape[1]), lambda i: (i, 0))],
        out_shape=[jax.ShapeDtypeStruct((T, wa.shape[1]), BF16),
                   jax.ShapeDtypeStruct((T, wb.shape[1]), F32),
                   jax.ShapeDtypeStruct((T, wg.shape[1]), F32)],
        compiler_params=_params("parallel"),
        cost_estimate=_cost(2 * T * D_MODEL * (wa.shape[1] + wb.shape[1] + wg.shape[1]),
                            T * (4 * D_MODEL + 2 * wa.shape[1] + 4 * wb.shape[1] + 4 * wg.shape[1])),
        name="even_in_proj",
    )(x2d, wa, wb, wg, token)


def _gla_kernel(q_ref, k_ref, v_ref, r_ref, gate_ref, wf_ref, bf_ref, wb_ref, bb_ref, ng_ref,
                o_ref, la_ref, qd_ref, ke_ref, vt_ref, tot_ref, of_ref, ob_ref, *, seq):
    C = GLA_CHUNK
    n_chunks = seq // C
    tile = min(GLA_TILE, seq)
    cpt = tile // C
    gate = gate_ref[...]
    gate_hi = gate.astype(BF16)
    gate_lo = (gate - gate_hi.astype(F32)).astype(BF16)
    for d, (w_ref, b_ref) in enumerate(((wf_ref, bf_ref), (wb_ref, bb_ref))):
        w = w_ref[...]
        w_hi = w.astype(BF16)
        w_lo = (w - w_hi.astype(F32)).astype(BF16)
        z = _mm(gate_hi, w_hi) + _mm(gate_hi, w_lo) + _mm(gate_lo, w_hi) + b_ref[...]
        la_ref[d] = _log_sigmoid(z) * (1.0 / GLA_TAU)

    trow = lax.broadcasted_iota(jnp.int32, (tile, tile), 0)
    tcol = lax.broadcasted_iota(jnp.int32, (tile, tile), 1)
    same_chunk = (trow // C) == (tcol // C)
    keep = (same_chunk & (trow >= tcol), same_chunk & (tcol >= trow))
    rmod = lax.broadcasted_iota(jnp.int32, (tile, LANES), 0) % C
    lane = lax.broadcasted_iota(jnp.int32, (1, LANES), 1)
    head_lane = (lane < GLA_DK, lane >= GLA_DK)
    srow = lax.broadcasted_iota(jnp.int32, (2 * GLA_DV, LANES), 0)
    scol = lax.broadcasted_iota(jnp.int32, (2 * GLA_DV, LANES), 1)
    diag = (srow < GLA_DV) == (scol < GLA_DK)
    scale = GLA_DK ** -0.5

    def chunk_scan(x, d):
        step = 1
        while step < C:
            if d == 0:
                x = x + jnp.where(rmod >= step, pltpu.roll(x, step, 0), 0.0)
            else:
                x = x + jnp.where(rmod < C - step, pltpu.roll(x, tile - step, 0), 0.0)
            step *= 2
        return x

    def intra(j, _):
        r0 = pl.multiple_of(j * tile, tile)
        rows = pl.ds(r0, tile)
        q = q_ref[rows, :].astype(F32) * scale
        k = k_ref[rows, :].astype(F32)
        v = v_ref[rows, :]
        vf = v.astype(F32)
        for c in range(cpt):
            vt_ref[j * cpt + c] = vf[c * C:(c + 1) * C, :].T.astype(BF16)
        for d, out_ref in enumerate((of_ref, ob_ref)):
            la = la_ref[d, rows, :]
            b = chunk_scan(la, d)
            tots = [jnp.sum(la[c * C:(c + 1) * C, :], axis=0, keepdims=True) for c in range(cpt)]
            for c in range(cpt):
                tot_ref[d, pl.ds(j * cpt + c, 1), :] = tots[c]
            tot = jnp.concatenate([jnp.broadcast_to(t, (C, LANES)) for t in tots], axis=0)
            qd = (q * jnp.exp(b)).astype(BF16)
            kd = (k * jnp.exp(-b)).astype(BF16)
            qd_ref[d, rows, :] = qd
            ke_ref[d, rows, :] = (k * jnp.exp(tot - b)).astype(BF16)
            parts = []
            for h in range(2):
                qh = jnp.where(head_lane[h], qd, jnp.zeros_like(qd))
                s = jnp.where(keep[d], _mm_nt(qh, kd), 0.0)
                parts.append(_mm(s.astype(BF16), v[:, h * GLA_DV:(h + 1) * GLA_DV]))
            out_ref[rows, :] = jnp.concatenate(parts, axis=1)
        return 0

    lax.fori_loop(0, seq // tile, intra, 0, unroll=2)

    def inter(i, carry):
        new = []
        for d, (state, out_ref) in enumerate(zip(carry, (of_ref, ob_ref))):
            c = i if d == 0 else n_chunks - 1 - i
            rows = pl.ds(pl.multiple_of(c * C, C), C)
            out_ref[rows, :] += _mm_nt(qd_ref[d, rows, :], state.astype(BF16))
            upd = jnp.where(diag, _mm(vt_ref[c], ke_ref[d, rows, :]), 0.0)
            new.append(jnp.exp(tot_ref[d, pl.ds(c, 1), :]) * state + upd)
        return tuple(new)

    zero = jnp.zeros((2 * GLA_DV, LANES), F32)
    lax.fori_loop(0, n_chunks, inter, (zero, zero), unroll=GLA_UNROLL)

    blk = 256

    def finish(j, _):
        rows = pl.ds(pl.multiple_of(j * blk, blk), blk)
        o = of_ref[rows, :] + ob_ref[rows, :]
        g = ng_ref[...]
        outs = []
        for h in range(2):
            oh = o[:, h * GLA_DV:(h + 1) * GLA_DV]
            ms = jnp.mean(oh * oh, axis=-1, keepdims=True)
            outs.append(oh * lax.rsqrt(ms + RMS_EPS) * g[:, h * GLA_DV:(h + 1) * GLA_DV])
        r = r_ref[rows, :].astype(F32)
        o_ref[rows, :] = (jnp.concatenate(outs, axis=1) * (r * jax.nn.sigmoid(r))).astype(o_ref.dtype)
        return 0

    lax.fori_loop(0, seq // blk, finish, 0)


def _gla(h_a, gate, wf, bf, wb, bb, norm_g, B, S):
    pairs = GLA_HEADS // 2
    kq, kv = 2 * GLA_DK, 2 * GLA_DV
    sq = pl.BlockSpec((None, S, kq), lambda b, g: (b, 0, g))
    sk = pl.BlockSpec((None, S, kq), lambda b, g: (b, 0, pairs + g))
    sv = pl.BlockSpec((None, S, kv), lambda b, g: (b, 0, (2 * GLA_QK) // kv + g))
    sr = pl.BlockSpec((None, S, kv), lambda b, g: (b, 0, (2 * GLA_QK + GLA_VW) // kv + g))
    sg = pl.BlockSpec((None, S, LANES), lambda b, g: (b, 0, 0))
    sw = pl.BlockSpec((LANES, kq), lambda b, g: (0, g))
    sb = pl.BlockSpec((1, kq), lambda b, g: (0, g))
    sn = pl.BlockSpec((1, kv), lambda b, g: (0, g))
    return pl.pallas_call(
        functools.partial(_gla_kernel, seq=S),
        grid=(B, pairs),
        in_specs=[sq, sk, sv, sr, sg, sw, sb, sw, sb, sn],
        out_specs=pl.BlockSpec((None, S, kv), lambda b, g: (b, 0, g)),
        out_shape=jax.ShapeDtypeStruct((B, S, GLA_VW), BF16),
        scratch_shapes=[pltpu.VMEM((2, S, kq), F32),
                        pltpu.VMEM((2, S, kq), BF16),
                        pltpu.VMEM((2, S, kq), BF16),
                        pltpu.VMEM((S // GLA_CHUNK, kv, GLA_CHUNK), BF16),
                        pltpu.VMEM((2, S // GLA_CHUNK, kq), F32),
                        pltpu.VMEM((S, kv), F32), pltpu.VMEM((S, kv), F32)],
        compiler_params=_params("parallel", "parallel"),
        cost_estimate=_cost(B * pairs * S * (12 * LANES * kq + 8 * GLA_TILE * (kq + kv // 2) + 8 * kq * kv),
                            B * S * (2 * 1536 + 4 * pairs * LANES + 2 * GLA_VW), 8 * B * pairs * S * kq),
        name="gla_mixer",
    )(h_a, h_a, h_a, h_a, gate, wf, bf, wb, bb, norm_g)


def _dil_kernel(q_ref, k_ref, v_ref, o_ref, m_ref, l_ref, acc_ref, bias_ref, s_ref, *, seq):
    R = DIL_RADIUS
    pair = pl.program_id(1)
    lane = lax.broadcasted_iota(jnp.int32, (1, LANES), 1)
    first_head = lane < DIL_DH
    scale = DIL_DH ** -0.5

    for p_idx, (window, dil) in enumerate(DIL_PATTERNS):
        assert window // (2 * dil) == R
        length = seq // dil
        tq = min(DIL_TQ, length)
        nk = tq + 2 * R
        tpr = length // tq
        shift = tpr.bit_length() - 1
        assert tpr == 1 << shift and tq % R == 0
        qi = lax.broadcasted_iota(jnp.int32, (tq, nk), 0)
        kj = lax.broadcasted_iota(jnp.int32, (tq, nk), 1)
        dist = jnp.abs(kj - R - qi)
        for h in range(2):
            slope = 1.0 / jnp.left_shift(jnp.ones((tq, nk), jnp.int32), 2 * pair + h + 1).astype(F32)
            bias_ref[h, :tq, :nk] = jnp.where(dist <= R, -(dil * dist).astype(F32) * slope, MASK_VALUE)
        ones = jnp.ones((nk, LANES), BF16)

        def rows_at(start, n, dil=dil):
            return pl.ds(start, n, stride=dil) if dil > 1 else pl.ds(pl.multiple_of(start, R), n)

        n_tiles = seq // tq
        assert n_tiles % 2 == 0

        def tile_pos(idx, dil=dil, tq=tq, tpr=tpr, shift=shift):
            res = lax.shift_right_logical(idx, shift)
            t = jnp.bitwise_and(idx, tpr - 1)
            start = res + dil * tq * t
            has_prev = t > 0
            has_next = t < tpr - 1
            prev = jnp.where(has_prev, start - dil * R, start)
            nxt = jnp.where(has_next, start + dil * tq, start)
            return start, prev, nxt, has_prev, has_next

        def scores(idx, slot, tq=tq, nk=nk, rows_at=rows_at, tile_pos=tile_pos):
            start, prev, nxt, _, _ = tile_pos(idx)
            q = (q_ref[rows_at(start, tq), :] * scale).astype(BF16)
            kcat = jnp.concatenate([k_ref[rows_at(prev, R), :], k_ref[rows_at(start, tq), :],
                                    k_ref[rows_at(nxt, R), :]], axis=0).astype(BF16)
            for h in range(2):
                qh = jnp.where(first_head if h == 0 else ~first_head, q, jnp.zeros_like(q))
                s_ref[slot, h, :tq, :nk] = _mm_nt(qh, kcat)

        def values(idx, slot, tq=tq, nk=nk, p_idx=p_idx, rows_at=rows_at, kj=kj, ones=ones,
                   tile_pos=tile_pos):
            start, prev, nxt, has_prev, has_next = tile_pos(idx)
            vcat = jnp.concatenate([v_ref[rows_at(prev, R), :], v_ref[rows_at(start, tq), :],
                                    v_ref[rows_at(nxt, R), :]], axis=0).astype(BF16)
            vones = jnp.concatenate([vcat, ones], axis=1)
            in_seq = ((kj >= R) | has_prev) & ((kj < R + tq) | has_next)
            ms, pvs = [], []
            for h in range(2):
                s = jnp.where(in_seq, s_ref[slot, h, :tq, :nk] + bias_ref[h, :tq, :nk], MASK_VALUE)
                m = jnp.max(s, axis=-1, keepdims=True)
                ms.append(m)
                pvs.append(_mm(jnp.exp(s - m).astype(BF16), vones))
            out_rows = rows_at(start, tq)
            m_ref[p_idx, out_rows, :] = jnp.where(first_head, ms[0], ms[1])
            l_ref[p_idx, out_rows, :] = jnp.where(first_head, pvs[0][:, LANES:], pvs[1][:, LANES:])
            acc_ref[p_idx, out_rows, :] = jnp.where(first_head, pvs[0][:, :LANES], pvs[1][:, :LANES])

        scores(0, 0)

        def body(i2, _, n_tiles=n_tiles, scores=scores, values=values):
            for u in range(2):
                idx = 2 * i2 + u
                scores(jnp.minimum(idx + 1, n_tiles - 1), 1 - u)
                values(idx, u)
            return 0

        lax.fori_loop(0, n_tiles // 2, body, 0, unroll=DIL_UNROLL)

    blk = 256
    n_pat = len(DIL_PATTERNS)

    def finish(j, _):
        rows = pl.ds(pl.multiple_of(j * blk, blk), blk)
        ms = [m_ref[p, rows, :] for p in range(n_pat)]
        m = functools.reduce(jnp.maximum, ms)
        ws = [jnp.exp(x - m) for x in ms]
        l = functools.reduce(lambda a, b: a + b, [w * l_ref[p, rows, :] for p, w in enumerate(ws)])
        acc = functools.reduce(lambda a, b: a + b, [w * acc_ref[p, rows, :] for p, w in enumerate(ws)])
        o_ref[rows, :] = (acc / l).astype(o_ref.dtype)
        return 0

    lax.fori_loop(0, seq // blk, finish, 0)


def _dilated(h_b, B, S):
    pairs = DIL_HEADS // 2
    sq = pl.BlockSpec((None, S, LANES), lambda b, g: (b, 0, g))
    sk = pl.BlockSpec((None, S, LANES), lambda b, g: (b, 0, pairs + g))
    sv = pl.BlockSpec((None, S, LANES), lambda b, g: (b, 0, 2 * pairs + g))
    return pl.pallas_call(
        functools.partial(_dil_kernel, seq=S),
        grid=(B, pairs),
        in_specs=[sq, sk, sv],
        out_specs=pl.BlockSpec((None, S, LANES), lambda b, g: (b, 0, g)),
        out_shape=jax.ShapeDtypeStruct((B, S, DIL_W), BF16),
        scratch_shapes=[pltpu.VMEM((len(DIL_PATTERNS), S, LANES), F32)] * 3
                       + [pltpu.VMEM((2, DIL_TQ, DIL_TQ + 2 * DIL_RADIUS), F32),
                          pltpu.VMEM((2, 2, DIL_TQ, DIL_TQ + 2 * DIL_RADIUS), F32)],
        compiler_params=_params("parallel", "parallel"),
        cost_estimate=_cost(B * DIL_HEADS * S * len(DIL_PATTERNS) * 2 * (DIL_TQ + 2 * DIL_RADIUS) * 3 * LANES,
                            B * S * (4 * 3 * DIL_W + 2 * DIL_W),
                            B * DIL_HEADS * S * len(DIL_PATTERNS) * (DIL_TQ + 2 * DIL_RADIUS)),
        name="dilated_mixer",
    )(h_b, h_b, h_b)


def _route(lt):
    g = [lt[i:i + 1, :] for i in range(MOE_GROUPS)]
    gmax = functools.reduce(jnp.maximum, g)
    gexp = [jnp.exp(x - gmax) for x in g]
    gsum = functools.reduce(lambda a, b: a + b, gexp)
    gprob = [x / gsum for x in gexp]
    g_val, g_idx = gprob[0], jnp.zeros_like(gprob[0], dtype=jnp.int32)
    for i in range(1, MOE_GROUPS):
        better = gprob[i] > g_val
        g_idx = jnp.where(better, i, g_idx)
        g_val = jnp.where(better, gprob[i], g_val)
    e = []
    for j in range(MOE_EPG):
        x = lt[MOE_GROUPS + j:MOE_GROUPS + j + 1, :]
        for gi in range(1, MOE_GROUPS):
            r0 = MOE_GROUPS + gi * MOE_EPG + j
            x = jnp.where(g_idx == gi, lt[r0:r0 + 1, :], x)
        e.append(x)
    emax = functools.reduce(jnp.maximum, e)
    eexp = [jnp.exp(x - emax) for x in e]
    esum = functools.reduce(lambda a, b: a + b, eexp)
    eprob = [x / esum for x in eexp]
    v1, i1 = eprob[0], jnp.zeros_like(g_idx)
    for j in range(1, MOE_EPG):
        better = eprob[j] > v1
        i1 = jnp.where(better, j, i1)
        v1 = jnp.where(better, eprob[j], v1)
    v2, i2 = jnp.full_like(v1, -1.0), jnp.zeros_like(g_idx)
    for j in range(MOE_EPG):
        cand = jnp.where(i1 == j, -1.0, eprob[j])
        better = cand > v2
        i2 = jnp.where(better, j, i2)
        v2 = jnp.where(better, cand, v2)
    den = v1 + v2
    ids = jnp.concatenate([g_idx * MOE_EPG + i1, g_idx * MOE_EPG + i2], axis=0)
    wts = jnp.concatenate([g_val * (v1 / den), g_val * (v2 / den)], axis=0)
    return ids, wts


def _proj_ln_route_kernel(*refs, n_in):
    x_ref = refs[0]
    o_refs = refs[1:1 + n_in]
    w_refs = refs[1 + n_in:1 + 2 * n_in]
    g_ref, b_ref, wr_ref, br_ref, x1_ref, x1b_ref, ids_ref, wts_ref = refs[1 + 2 * n_in:]
    m = _mm(o_refs[0][...], w_refs[0][...])
    for o_ref, w_ref in zip(o_refs[1:], w_refs[1:]):
        m = m + _mm(o_ref[...], w_ref[...])
    x1 = _layer_norm(ALPHA * x_ref[...] + m, g_ref[...], b_ref[...])
    x1_ref[...] = x1
    x_hi = x1.astype(BF16)
    x1b_ref[...] = x_hi
    x_lo = (x1 - x_hi.astype(F32)).astype(BF16)
    wr = wr_ref[...]
    w_hi = wr.astype(BF16)
    w_lo = (wr - w_hi.astype(F32)).astype(BF16)
    lt = _mm_nt(w_hi, x_hi) + _mm_nt(w_hi, x_lo) + _mm_nt(w_lo, x_hi) + br_ref[...]
    ids, wts = _route(lt)
    ids_ref[...] = ids
    wts_ref[...] = wts


def _proj_ln_route(x2d, outs, ws, ln_g, ln_b, wr, br):
    T = x2d.shape[0]
    tm = ROW_TILE
    n_in = len(outs)
    row = lambda w: pl.BlockSpec((tm, w), lambda i: (i, 0))
    tok = pl.BlockSpec((2, tm), lambda i: (0, i))
    return pl.pallas_call(
        functools.partial(_proj_ln_route_kernel, n_in=n_in),
        grid=(T // tm,),
        in_specs=[row(D_MODEL)] + [row(o.shape[1]) for o in outs] + [_full(w.shape) for w in ws]
                 + [_full(ln_g.shape), _full(ln_b.shape), _full(wr.shape), _full(br.shape)],
        out_specs=[row(D_MODEL), row(D_MODEL), tok, tok],
        out_shape=[jax.ShapeDtypeStruct((T, D_MODEL), F32), jax.ShapeDtypeStruct((T, D_MODEL), BF16),
                   jax.ShapeDtypeStruct((2, T), jnp.int32), jax.ShapeDtypeStruct((2, T), F32)],
        compiler_params=_params("parallel"),
        cost_estimate=_cost(2 * T * D_MODEL * (sum(o.shape[1] for o in outs) + 3 * 32),
                            T * (4 * D_MODEL + 2 * sum(o.shape[1] for o in outs) + 6 * D_MODEL + 16)),
        name="proj_ln_route",
    )(x2d, *outs, *ws, ln_g, ln_b, wr, br)


def _moe_kernel(te_ref, nu_ref, xs_ref, w1_ref, w3_ref, w2_ref, rw_ref, tok_ref, ys_ref, w13_s, w2_s):
    i = pl.program_id(0)

    @pl.when((i == 0) | (te_ref[i] != te_ref[jnp.maximum(i - 1, 0)]))
    def _():
        w13_s[:, :MOE_FF] = w1_ref[...].astype(BF16)
        w13_s[:, MOE_FF:] = w3_ref[...].astype(BF16)
        w2_s[...] = w2_ref[...].astype(BF16)

    @pl.when(i < nu_ref[0])
    def _():
        h = _mm(xs_ref[...], w13_s[...])
        h1, h3 = h[:, :MOE_FF], h[:, MOE_FF:]
        hidden = (h1 * jax.nn.sigmoid(h1)) * h3
        y = _mm(hidden.astype(BF16), w2_s[...])
        ys_ref[...] = (y * rw_ref[...]).astype(ys_ref.dtype)

    @pl.when(i >= nu_ref[0])
    def _():
        ys_ref[...] = jnp.zeros_like(ys_ref)


def _moe_experts(xs, w1, w3, w2, layer, row_w, tile_expert, n_used, token):
    P = xs.shape[0]
    tm = MOE_TILE
    grid_spec = pltpu.PrefetchScalarGridSpec(
        num_scalar_prefetch=2,
        grid=(P // tm,),
        in_specs=[pl.BlockSpec((tm, D_MODEL), lambda i, te, nu: (i, 0)),
                  pl.BlockSpec((None, None, D_MODEL, MOE_FF), lambda i, te, nu: (layer, te[i], 0, 0)),
                  pl.BlockSpec((None, None, D_MODEL, MOE_FF), lambda i, te, nu: (layer, te[i], 0, 0)),
                  pl.BlockSpec((None, None, MOE_FF, D_MODEL), lambda i, te, nu: (layer, te[i], 0, 0)),
                  pl.BlockSpec((tm, 1), lambda i, te, nu: (i, 0)), _TOKEN_SPEC],
        out_specs=pl.BlockSpec((tm, D_MODEL), lambda i, te, nu: (i, 0)),
        scratch_shapes=[pltpu.VMEM((D_MODEL, 2 * MOE_FF), BF16), pltpu.VMEM((MOE_FF, D_MODEL), BF16)],
    )
    return pl.pallas_call(
        _moe_kernel,
        grid_spec=grid_spec,
        out_shape=jax.ShapeDtypeStruct((P, D_MODEL), BF16),
        compiler_params=_params("arbitrary"),
        cost_estimate=_cost(6 * P * D_MODEL * MOE_FF, 4 * P * D_MODEL + 12 * MOE_EXPERTS * D_MODEL * MOE_FF,
                            P * MOE_FF),
        name="moe_experts",
    )(tile_expert, n_used, xs, w1, w3, w2, row_w, token)


def _moe_plan(ids, wts, T):
    tm = MOE_TILE
    n_assign = 2 * T
    e_flat = ids.reshape(n_assign)
    onehot = (e_flat[:, None] == jnp.arange(MOE_EXPERTS, dtype=jnp.int32)[None, :]).astype(jnp.int32)
    csum = jnp.cumsum(onehot, axis=0)
    counts = csum[-1]
    padded = ((counts + tm - 1) // tm) * tm
    ends = jnp.cumsum(padded)
    offs = ends - padded
    pos = jnp.sum(onehot * (csum - onehot + offs[None, :]), axis=1)
    P = n_assign + MOE_EXPERTS * tm
    n_tiles = P // tm
    tile_start = jnp.arange(n_tiles, dtype=jnp.int32) * tm
    tile_expert = jnp.minimum(jnp.sum((tile_start[:, None] >= ends[None, :]).astype(jnp.int32), axis=1),
                              MOE_EXPERTS - 1).astype(jnp.int32)
    n_used = (ends[-1:] // tm).astype(jnp.int32)
    tok = jnp.arange(n_assign, dtype=jnp.int32) % T
    _, tok_sorted, w_sorted = lax.sort((e_flat, tok, wts.reshape(n_assign)), num_keys=1, is_stable=True)
    t_onehot = (tile_expert[:, None] == jnp.arange(MOE_EXPERTS, dtype=jnp.int32)[None, :]).astype(jnp.int32)
    t_rank0 = tile_start - jnp.sum(t_onehot * offs[None, :], axis=1)
    t_count = jnp.sum(t_onehot * counts[None, :], axis=1)
    t_first = jnp.sum(t_onehot * (jnp.cumsum(counts) - counts)[None, :], axis=1)
    r = t_rank0[:, None] + jnp.arange(tm, dtype=jnp.int32)[None, :]
    valid = (r < t_count[:, None]).reshape(P)
    u = jnp.clip(t_first[:, None] + r, 0, n_assign - 1).reshape(P)
    src_tok = jnp.where(valid, tok_sorted.at[u].get(mode="promise_in_bounds"), 0)
    row_w = jnp.where(valid, w_sorted.at[u].get(mode="promise_in_bounds"), 0.0)
    return src_tok, row_w.reshape(P, 1), tile_expert, n_used, pos


def _combine_ln_kernel(x_ref, y0_ref, y1_ref, g_ref, b_ref, tok_ref, o_ref):
    y = ALPHA * x_ref[...] + y0_ref[...].astype(F32) + y1_ref[...].astype(F32)
    o_ref[...] = _layer_norm(y, g_ref[...], b_ref[...])


def _combine_ln(x2d, y0, y1, ln_g, ln_b, token):
    T = x2d.shape[0]
    tm = ROW_TILE
    row = pl.BlockSpec((tm, D_MODEL), lambda i: (i, 0))
    return pl.pallas_call(
        _combine_ln_kernel,
        grid=(T // tm,),
        in_specs=[row, row, row, _full(ln_g.shape), _full(ln_b.shape), _TOKEN_SPEC],
        out_specs=row,
        out_shape=jax.ShapeDtypeStruct((T, D_MODEL), F32),
        compiler_params=_params("parallel"),
        cost_estimate=_cost(12 * T * D_MODEL, 12 * T * D_MODEL),
        name="combine_ln",
    )(x2d, y0, y1, ln_g, ln_b, token)


def _experts_stage(i, routed, prm, token):
    _, x1b, (src_tok, row_w, tile_expert, n_used, _) = routed
    xs = x1b.at[src_tok].get(mode="promise_in_bounds")
    return _moe_experts(xs, prm["moe_w1"], prm["moe_w3"], prm["moe_w2"], i, row_w, tile_expert, n_used, token)


def _combine_stage(i, routed, ys, prm, token):
    x1, _, plan = routed
    T = x1.shape[0]
    pos = plan[4]
    y0 = ys.at[pos[:T]].get(mode="promise_in_bounds")
    y1 = ys.at[pos[T:]].get(mode="promise_in_bounds")
    return _combine_ln(x1, y0, y1, prm["ln2_g"][i:i + 1], prm["ln2_b"][i:i + 1], token)


def _rope_block(x, c, sa, sb):
    return x * c + pltpu.roll(x, LANES - 32, 1) * sa + pltpu.roll(x, 32, 1) * sb


def _mla_in_kernel(x_ref, win_ref, qg_ref, kvg_ref, wq_ref, wk_ref, wv_ref, c_ref, sa_ref, sb_ref,
                   tok_ref, q_ref, k_ref, v_ref):
    x = x_ref[...].astype(BF16)
    h = _mm(x, win_ref[...])
    c_q = h[:, :MLA_Q_LORA]
    c_kv = h[:, MLA_Q_LORA:MLA_Q_LORA + MLA_KV_LORA]
    k_rope = h[:, MLA_Q_LORA + MLA_KV_LORA:]
    cqn = c_q * lax.rsqrt(jnp.mean(c_q * c_q, axis=-1, keepdims=True) + RMS_EPS) * qg_ref[...]
    ckn = c_kv * lax.rsqrt(jnp.mean(c_kv * c_kv, axis=-1, keepdims=True) + RMS_EPS) * kvg_ref[...]
    cqn, ckn = cqn.astype(BF16), ckn.astype(BF16)
    scale = (MLA_NOPE + MLA_ROPE) ** -0.5 * LOG2_E
    q = _mm(cqn, wq_ref[...]) * scale
    kn = _mm(ckn, wk_ref[...])
    v = _mm(ckn, wv_ref[...]).astype(v_ref.dtype)
    c, sa, sb = c_ref[...], sa_ref[...], sb_ref[...]
    kr = _rope_block(k_rope, c, sa, sb).astype(k_ref.dtype)
    ones = jnp.ones((x.shape[0], LANES), v_ref.dtype)
    for g in range(MLA_HEADS // 2):
        lo = 2 * LANES * g
        v_ref[:, lo:lo + LANES] = v[:, LANES * g:LANES * (g + 1)]
        v_ref[:, lo + LANES:lo + 2 * LANES] = ones
        q_ref[:, lo:lo + LANES] = q[:, lo:lo + LANES].astype(q_ref.dtype)
        q_ref[:, lo + LANES:lo + 2 * LANES] = _rope_block(q[:, lo + LANES:lo + 2 * LANES], c, sa, sb
                                                          ).astype(q_ref.dtype)
        k_ref[:, lo:lo + LANES] = kn[:, LANES * g:LANES * (g + 1)].astype(k_ref.dtype)
        k_ref[:, lo + LANES:lo + 2 * LANES] = kr


def _mla_in(x2d, win, qg, kvg, wq, wk, wv, c, sa, sb, S, token):
    T = x2d.shape[0]
    tm = ROW_TILE
    row = lambda w: pl.BlockSpec((tm, w), lambda i: (i, 0))
    n_s = S // tm
    tab = pl.BlockSpec((tm, LANES), lambda i: (i % n_s, 0))
    pairs = MLA_HEADS // 2
    return pl.pallas_call(
        _mla_in_kernel,
        grid=(T // tm,),
        in_specs=[row(D_MODEL), _full(win.shape), _full(qg.shape), _full(kvg.shape), _full(wq.shape),
                  _full(wk.shape), _full(wv.shape), tab, tab, tab, _TOKEN_SPEC],
        out_specs=[row(pairs * 2 * LANES)] * 3,
        out_shape=[jax.ShapeDtypeStruct((T, pairs * 2 * LANES), BF16)] * 3,
        compiler_params=_params("parallel"),
        cost_estimate=_cost(2 * T * (D_MODEL * win.shape[1] + MLA_Q_LORA * wq.shape[1]
                                     + MLA_KV_LORA * (wk.shape[1] + wv.shape[1])),
                            T * (4 * D_MODEL + 3 * 2 * pairs * 2 * LANES)),
        name="mla_in_proj",
    )(x2d, win, qg, kvg, wq, wk, wv, c, sa, sb, token)


def _mla_attn_kernel(q_ref, k_ref, v_ref, o_ref, sa_ref, sb_ref, *, seq):
    lane2 = lax.broadcasted_iota(jnp.int32, (1, 2 * LANES), 1)
    rl = lane2 - LANES
    in_a = (lane2 < MLA_NOPE) | ((rl >= 0) & (rl < 16)) | ((rl >= 32) & (rl < 48))
    in_b = ((lane2 >= MLA_NOPE) & (lane2 < LANES)) | ((rl >= 16) & (rl < 32)) | ((rl >= 48) & (rl < 64))
    lane = lax.broadcasted_iota(jnp.int32, (1, LANES), 1)
    q = q_ref[...]
    tq = q.shape[0]
    tk = min(ATT_TK, seq)
    n_k = seq // tk
    q_heads = (jnp.where(in_a, q, jnp.zeros_like(q)), jnp.where(in_b, q, jnp.zeros_like(q)))
    s_refs = (sa_ref, sb_ref)

    def score_chunk(h, j, m):
        s = _mm_nt(q_heads[h], k_ref[j * tk:(j + 1) * tk, :])
        s_refs[h][:, j * tk:(j + 1) * tk] = s
        for c in range(tk // LANES):
            m = jnp.maximum(m, s[:, c * LANES:(c + 1) * LANES])
        return m

    def value_chunk(h, j, m_row, acc):
        p = jnp.exp2(s_refs[h][:, j * tk:(j + 1) * tk] - m_row).astype(BF16)
        return acc + _mm(p, v_ref[j * tk:(j + 1) * tk, :])

    neg = jnp.full((tq, LANES), -jnp.inf, F32)
    m = neg
    for j in range(n_k):
        m = score_chunk(0, j, m)
    m_a = jnp.max(m, axis=-1, keepdims=True)
    acc_a = jnp.zeros((tq, 2 * LANES), F32)
    m = neg
    for j in range(n_k):
        acc_a = value_chunk(0, j, m_a, acc_a)
        m = score_chunk(1, j, m)
    m_b = jnp.max(m, axis=-1, keepdims=True)
    acc_b = jnp.zeros((tq, 2 * LANES), F32)
    for j in range(n_k):
        acc_b = value_chunk(1, j, m_b, acc_b)
    out_a = acc_a[:, :LANES] / acc_a[:, LANES:]
    out_b = acc_b[:, :LANES] / acc_b[:, LANES:]
    o_ref[...] = jnp.where(lane < MLA_V, out_a, out_b).astype(o_ref.dtype)


def _mla_attn(q, k, v, B, S):
    pairs = MLA_HEADS // 2
    tq = min(ATT_TQ, S)
    return pl.pallas_call(
        functools.partial(_mla_attn_kernel, seq=S),
        grid=(B, pairs, S // tq),
        in_specs=[pl.BlockSpec((None, tq, 2 * LANES), lambda b, g, i: (b, i, g)),
                  pl.BlockSpec((None, S, 2 * LANES), lambda b, g, i: (b, 0, g)),
                  pl.BlockSpec((None, S, 2 * LANES), lambda b, g, i: (b, 0, g))],
        out_specs=pl.BlockSpec((None, tq, LANES), lambda b, g, i: (b, i, g)),
        out_shape=jax.ShapeDtypeStruct((B, S, pairs * LANES), BF16),
        scratch_shapes=[pltpu.VMEM((tq, S), F32), pltpu.VMEM((tq, S), F32)],
        compiler_params=_params("parallel", "parallel", "arbitrary"),
        cost_estimate=_cost(B * MLA_HEADS * S * S * 2 * 2 * 2 * LANES,
                            B * S * pairs * LANES * (3 * 2 * 2 + 2), B * MLA_HEADS * S * S),
        name="mla_attention",
    )(q, k, v)


def _prep_even(w_in, wa_f, ba_f, wa_b, ba_b, norm_g, w_out):
    o_q, o_k, o_v, o_r = 0, GLA_QK, 2 * GLA_QK, 2 * GLA_QK + GLA_VW
    o_af = o_r + GLA_VW
    o_ab = o_af + GLA_RANK
    o_qb = o_ab + GLA_RANK
    wa = w_in[:, :o_af].astype(BF16)
    wb = w_in[:, o_qb:].astype(BF16)
    wg = jnp.zeros((D_MODEL, LANES), F32).at[:, :2 * GLA_RANK].set(w_in[:, o_af:o_qb]).astype(BF16)
    wf = jnp.zeros((LANES, GLA_QK), F32).at[:GLA_RANK].set(wa_f)
    wb_gate = jnp.zeros((LANES, GLA_QK), F32).at[GLA_RANK:2 * GLA_RANK].set(wa_b)
    return dict(wa=wa, wb=wb, wg=wg, wf=wf, bf=ba_f.reshape(1, -1), wb_gate=wb_gate,
                bb=ba_b.reshape(1, -1), norm_g=norm_g.reshape(1, -1),
                wo_a=w_out[:GLA_VW].astype(BF16), wo_b=w_out[GLA_VW:].astype(BF16))


def _prep_odd(w_in, q_norm, kv_norm, w_uq, w_ukv, w_out, S):
    half = MLA_ROPE // 2
    pairs = MLA_HEADS // 2
    kr = w_in[:, MLA_Q_LORA + MLA_KV_LORA:]
    kr_rep = jnp.concatenate([kr[:, :half], kr[:, :half], kr[:, half:], kr[:, half:],
                              jnp.zeros((D_MODEL, LANES - 2 * MLA_ROPE), F32)], axis=1)
    win = jnp.concatenate([w_in[:, :MLA_Q_LORA + MLA_KV_LORA], kr_rep], axis=1).astype(BF16)
    uq = w_uq.reshape(MLA_Q_LORA, MLA_HEADS, MLA_NOPE + MLA_ROPE)
    nope = uq[:, :, :MLA_NOPE].reshape(MLA_Q_LORA, pairs, 2 * MLA_NOPE)
    r1 = uq[:, :, MLA_NOPE:MLA_NOPE + half].reshape(MLA_Q_LORA, pairs, 2 * half)
    r2 = uq[:, :, MLA_NOPE + half:].reshape(MLA_Q_LORA, pairs, 2 * half)
    pad = jnp.zeros((MLA_Q_LORA, pairs, LANES - 2 * MLA_ROPE), F32)
    wq = jnp.concatenate([nope, r1, r2, pad], axis=2).reshape(MLA_Q_LORA, pairs * 2 * LANES).astype(BF16)
    ukv = w_ukv.reshape(MLA_KV_LORA, MLA_HEADS, MLA_NOPE + MLA_V)
    wk = ukv[:, :, :MLA_NOPE].reshape(MLA_KV_LORA, MLA_HEADS * MLA_NOPE).astype(BF16)
    wv = ukv[:, :, MLA_NOPE:].reshape(MLA_KV_LORA, MLA_HEADS * MLA_V).astype(BF16)
    inv = 1.0 / (ROPE_BASE ** (jnp.arange(0, MLA_ROPE, 2, dtype=F32) / MLA_ROPE))
    ang = jnp.arange(S, dtype=F32)[:, None] * inv[None, :]
    cos, sin = jnp.cos(ang), jnp.sin(ang)
    z16 = jnp.zeros((S, half), F32)
    z64 = jnp.zeros((S, LANES - 2 * MLA_ROPE), F32)
    c = jnp.concatenate([cos, cos, cos, cos, z64], axis=1)
    sa = jnp.concatenate([-sin, -sin, z16, z16, z64], axis=1)
    sb = jnp.concatenate([z16, z16, sin, sin, z64], axis=1)
    return dict(win=win, qg=q_norm.reshape(1, -1), kvg=kv_norm.reshape(1, -1), wq=wq, wk=wk, wv=wv,
                c=c, sa=sa, sb=sb, wo=w_out.astype(BF16))


def _prep_moe(wg, bg, we, be):
    wr = jnp.zeros((32, D_MODEL), F32).at[:MOE_GROUPS].set(wg.T).at[MOE_GROUPS:MOE_GROUPS + MOE_EXPERTS].set(we.T)
    br = jnp.zeros((32, 1), F32).at[:MOE_GROUPS, 0].set(bg).at[MOE_GROUPS:MOE_GROUPS + MOE_EXPERTS, 0].set(be)
    return dict(wr=wr, br=br)


def _mix_route(i, x2d, B, S, prm, token):
    ev, od, moe, ln1_g, ln1_b = prm["ev"], prm["od"], prm["moe"], prm["ln1_g"], prm["ln1_b"]
    T = B * S
    if i % 2 == 0:
        p = ev[i // 2]
        h_a, h_b, gate = _even_in(x2d, p["wa"], p["wb"], p["wg"], token)
        o_a = _gla(h_a.reshape(B, S, -1), gate.reshape(B, S, -1), p["wf"], p["bf"], p["wb_gate"],
                   p["bb"], p["norm_g"], B, S)
        o_b = _dilated(h_b.reshape(B, S, -1), B, S)
        outs = [o_a.reshape(T, -1), o_b.reshape(T, -1)]
        ws = [p["wo_a"], p["wo_b"]]
    else:
        p = od[i // 2]
        q, k, v = _mla_in(x2d, p["win"], p["qg"], p["kvg"], p["wq"], p["wk"], p["wv"],
                          p["c"], p["sa"], p["sb"], S, token)
        o = _mla_attn(q.reshape(B, S, -1), k.reshape(B, S, -1), v.reshape(B, S, -1), B, S)
        outs = [o.reshape(T, -1)]
        ws = [p["wo"]]
    m = moe[i]
    x1, x1b, ids, wts = _proj_ln_route(x2d, outs, ws, ln1_g[i:i + 1], ln1_b[i:i + 1], m["wr"], m["br"])
    return x1, x1b, _moe_plan(ids, wts, T)


def _plan_token(routed):
    return routed[2][0]


def _rows_token(rows):
    return rows[:8, :LANES]


def _trunk_pair(xa, xb, prm):
    (Ba, S, D), Bb = xa.shape, xb.shape[0]
    a = xa.reshape(Ba * S, D)
    b = xb.reshape(Bb * S, D)
    tok = jnp.zeros((1,), jnp.int32)
    for i in range(DEPTH):
        routed_a = _mix_route(i, a, Ba, S, prm, tok)
        if i > 0:
            b = _combine_stage(i - 1, routed_b, ys_b, prm, _plan_token(routed_a))
        routed_b = _mix_route(i, b, Bb, S, prm, _plan_token(routed_a))
        ys_a = _experts_stage(i, routed_a, prm, _plan_token(routed_b))
        ys_b = _experts_stage(i, routed_b, prm, _rows_token(ys_a))
        a = _combine_stage(i, routed_a, ys_a, prm, _rows_token(ys_b))
        tok = _rows_token(a)
    b = _combine_stage(DEPTH - 1, routed_b, ys_b, prm, tok)
    return a.reshape(Ba, S, D), b.reshape(Bb, S, D)


def kernel(x_prompt, x_sample, ev_w_in, ev_wa_f, ev_ba_f, ev_wa_b, ev_ba_b, ev_gla_norm, ev_w_out,
           od_w_in, od_q_norm, od_kv_norm, od_w_uq, od_w_ukv, od_w_out, ln1_g, ln1_b, ln2_g, ln2_b,
           moe_wg, moe_bg, moe_we, moe_be, moe_w1, moe_w3, moe_w2):
    S = x_prompt.shape[1]
    ev = [_prep_even(ev_w_in[j], ev_wa_f[j], ev_ba_f[j], ev_wa_b[j], ev_ba_b[j], ev_gla_norm[j], ev_w_out[j])
          for j in range(ev_w_in.shape[0])]
    od = [_prep_odd(od_w_in[j], od_q_norm[j], od_kv_norm[j], od_w_uq[j], od_w_ukv[j], od_w_out[j], S)
          for j in range(od_w_in.shape[0])]
    moe = [_prep_moe(moe_wg[i], moe_bg[i], moe_we[i], moe_be[i]) for i in range(DEPTH)]
    prm = dict(ev=ev, od=od, moe=moe, ln1_g=ln1_g, ln1_b=ln1_b, ln2_g=ln2_g, ln2_b=ln2_b,
               moe_w1=moe_w1, moe_w3=moe_w3, moe_w2=moe_w2)
    y_sample, y_prompt = _trunk_pair(x_sample, x_prompt, prm)
    return y_prompt, y_sample
```

```python
import functools

import jax
import jax.numpy as jnp
from jax import lax
from jax.experimental import pallas as pl
from jax.experimental.pallas import tpu as pltpu

F32 = jnp.float32
BF16 = jnp.bfloat16

D_MODEL = 1024
DEPTH = 2
GLA_HEADS, GLA_DK, GLA_DV = 4, 64, 128
GLA_QK, GLA_VW = GLA_HEADS * GLA_DK, GLA_HEADS * GLA_DV
GLA_RANK, GLA_TAU, GLA_CHUNK = 16, 16.0, 64
DIL_HEADS, DIL_DH = 8, 64
DIL_W = DIL_HEADS * DIL_DH
DIL_PATTERNS = ((128, 1), (512, 4), (2048, 16))
DIL_RADIUS = 64
MASK_VALUE = -1e30
MLA_HEADS, MLA_NOPE, MLA_ROPE, MLA_V = 16, 64, 32, 64
MLA_Q_LORA, MLA_KV_LORA = 384, 128
ROPE_BASE = 10000.0
MOE_GROUPS, MOE_EPG, MOE_EXPERTS, MOE_FF = 4, 4, 16, 512
ALPHA = (2 * DEPTH) ** 0.25
LN_EPS = 1e-5
RMS_EPS = 1e-6
LOG2_E = 1.4426950408889634

LANES = 128
VMEM_LIMIT = 56 * 1024 * 1024
ROW_TILE = 512
MOE_TILE = 512
ATT_TQ = 1024
ATT_TK = 512
DIL_UNROLL = 2
GLA_UNROLL = 4
GLA_TILE = 256
DIL_TQ = 256


def _params(*sem):
    return pltpu.CompilerParams(dimension_semantics=sem, vmem_limit_bytes=VMEM_LIMIT)


def _full(shape):
    n = len(shape)
    return pl.BlockSpec(shape, lambda *_: (0,) * n)


_TOKEN_SPEC = pl.BlockSpec(memory_space=pl.ANY)


def _cost(flops, bytes_accessed, transcendentals=0):
    return pl.CostEstimate(flops=int(flops), bytes_accessed=int(bytes_accessed),
                           transcendentals=int(transcendentals))


def _mm(a, b):
    return jnp.dot(a, b, preferred_element_type=F32)


def _mm_nt(a, b):
    return lax.dot_general(a, b, (((1,), (1,)), ((), ())), preferred_element_type=F32)


def _layer_norm(y, g, b):
    mu = jnp.mean(y, axis=-1, keepdims=True)
    yc = y - mu
    var = jnp.mean(yc * yc, axis=-1, keepdims=True)
    return yc * lax.rsqrt(var + LN_EPS) * g + b


def _log_sigmoid(z):
    return jnp.minimum(z, 0.0) - jnp.log(1.0 + jnp.exp(-jnp.abs(z)))


def _even_in_kernel(x_ref, wa_ref, wb_ref, wg_ref, tok_ref, oa_ref, ob_ref, og_ref):
    x = x_ref[...].astype(BF16)
    oa_ref[...] = _mm(x, wa_ref[...]).astype(oa_ref.dtype)
    ob_ref[...] = _mm(x, wb_ref[...])
    og_ref[...] = _mm(x, wg_ref[...])


def _even_in(x2d, wa, wb, wg, token):
    T = x2d.shape[0]
    tm = ROW_TILE
    return pl.pallas_call(
        _even_in_kernel,
        grid=(T // tm,),
        in_specs=[pl.BlockSpec((tm, D_MODEL), lambda i: (i, 0)),
                  _full(wa.shape), _full(wb.shape), _full(wg.shape), _TOKEN_SPEC],
        out_specs=[pl.BlockSpec((tm, wa.shape[1]), lambda i: (i, 0)),
                   pl.BlockSpec((tm, wb.shape[1]), lambda i: (i, 0)),
                   pl.BlockSpec((tm, wg.shape[1]), lambda i: (i, 0))],
        out_shape=[jax.ShapeDtypeStruct((T, wa.shape[1]), BF16),
                   jax.ShapeDtypeStruct((T, wb.shape[1]), F32),
                   jax.ShapeDtypeStruct((T, wg.shape[1]), F32)],
        compiler_params=_params("parallel"),
        cost_estimate=_cost(2 * T * D_MODEL * (wa.shape[1] + wb.shape[1] + wg.shape[1]),
                            T * (4 * D_MODEL + 2 * wa.shape[1] + 4 * wb.shape[1] + 4 * wg.shape[1])),
        name="even_in_proj",
    )(x2d, wa, wb, wg, token)


def _gla_kernel(q_ref, k_ref, v_ref, r_ref, gate_ref, wf_ref, bf_ref, wb_ref, bb_ref, ng_ref,
                o_ref, la_ref, qd_ref, ke_ref, vt_ref, tot_ref, of_ref, ob_ref, *, seq):
    C = GLA_CHUNK
    n_chunks = seq // C
    tile = min(GLA_TILE, seq)
    cpt = tile // C
    gate = gate_ref[...]
    gate_hi = gate.astype(BF16)
    gate_lo = (gate - gate_hi.astype(F32)).astype(BF16)
    for d, (w_ref, b_ref) in enumerate(((wf_ref, bf_ref), (wb_ref, bb_ref))):
        w = w_ref[...]
        w_hi = w.astype(BF16)
        w_lo = (w - w_hi.astype(F32)).astype(BF16)
        z = _mm(gate_hi, w_hi) + _mm(gate_hi, w_lo) + _mm(gate_lo, w_hi) + b_ref[...]
        la_ref[d] = _log_sigmoid(z) * (1.0 / GLA_TAU)

    trow = lax.broadcasted_iota(jnp.int32, (tile, tile), 0)
    tcol = lax.broadcasted_iota(jnp.int32, (tile, tile), 1)
    same_chunk = (trow // C) == (tcol // C)
    keep = (same_chunk & (trow >= tcol), same_chunk & (tcol >= trow))
    rmod = lax.broadcasted_iota(jnp.int32, (tile, LANES), 0) % C
    lane = lax.broadcasted_iota(jnp.int32, (1, LANES), 1)
    head_lane = (lane < GLA_DK, lane >= GLA_DK)
    srow = lax.broadcasted_iota(jnp.int32, (2 * GLA_DV, LANES), 0)
    scol = lax.broadcasted_iota(jnp.int32, (2 * GLA_DV, LANES), 1)
    diag = (srow < GLA_DV) == (scol < GLA_DK)
    scale = GLA_DK ** -0.5

    def chunk_scan(x, d):
        step = 1
        while step < C:
            if d == 0:
                x = x + jnp.where(rmod >= step, pltpu.roll(x, step, 0), 0.0)
            else:
                x = x + jnp.where(rmod < C - step, pltpu.roll(x, tile - step, 0), 0.0)
            step *= 2
        return x

    def intra(j, _):
        r0 = pl.multiple_of(j * tile, tile)
        rows = pl.ds(r0, tile)
        q = q_ref[rows, :].astype(F32) * scale
        k = k_ref[rows, :].astype(F32)
        v = v_ref[rows, :]
        vf = v.astype(F32)
        for c in range(cpt):
            vt_ref[j * cpt + c] = vf[c * C:(c + 1) * C, :].T.astype(BF16)
        for d, out_ref in enumerate((of_ref, ob_ref)):
            la = la_ref[d, rows, :]
            b = chunk_scan(la, d)
            tots = [jnp.sum(la[c * C:(c + 1) * C, :], axis=0, keepdims=True) for c in range(cpt)]
            for c in range(cpt):
                tot_ref[d, pl.ds(j * cpt + c, 1), :] = tots[c]
            tot = jnp.concatenate([jnp.broadcast_to(t, (C, LANES)) for t in tots], axis=0)
            qd = (q * jnp.exp(b)).astype(BF16)
            kd = (k * jnp.exp(-b)).astype(BF16)
            qd_ref[d, rows, :] = qd
            ke_ref[d, rows, :] = (k * jnp.exp(tot - b)).astype(BF16)
            parts = []
            for h in range(2):
                qh = jnp.where(head_lane[h], qd, jnp.zeros_like(qd))
                s = jnp.where(keep[d], _mm_nt(qh, kd), 0.0)
                parts.append(_mm(s.astype(BF16), v[:, h * GLA_DV:(h + 1) * GLA_DV]))
            out_ref[rows, :] = jnp.concatenate(parts, axis=1)
        return 0

    lax.fori_loop(0, seq // tile, intra, 0, unroll=2)

    def inter(i, carry):
        new = []
        for d, (state, out_ref) in enumerate(zip(carry, (of_ref, ob_ref))):
            c = i if d == 0 else n_chunks - 1 - i
            rows = pl.ds(pl.multiple_of(c * C, C), C)
            out_ref[rows, :] += _mm_nt(qd_ref[d, rows, :], state.astype(BF16))
            upd = jnp.where(diag, _mm(vt_ref[c], ke_ref[d, rows, :]), 0.0)
            new.append(jnp.exp(tot_ref[d, pl.ds(c, 1), :]) * state + upd)
        return tuple(new)

    zero = jnp.zeros((2 * GLA_DV, LANES), F32)
    lax.fori_loop(0, n_chunks, inter, (zero, zero), unroll=GLA_UNROLL)

    blk = 256

    def finish(j, _):
        rows = pl.ds(pl.multiple_of(j * blk, blk), blk)
        o = of_ref[rows, :] + ob_ref[rows, :]
        g = ng_ref[...]
        outs = []
        for h in range(2):
            oh = o[:, h * GLA_DV:(h + 1) * GLA_DV]
            ms = jnp.mean(oh * oh, axis=-1, keepdims=True)
            outs.append(oh * lax.rsqrt(ms + RMS_EPS) * g[:, h * GLA_DV:(h + 1) * GLA_DV])
        r = r_ref[rows, :].astype(F32)
        o_ref[rows, :] = (jnp.concatenate(outs, axis=1) * (r * jax.nn.sigmoid(r))).astype(o_ref.dtype)
        return 0

    lax.fori_loop(0, seq // blk, finish, 0)


def _gla(h_a, gate, wf, bf, wb, bb, norm_g, B, S):
    pairs = GLA_HEADS // 2
    kq, kv = 2 * GLA_DK, 2 * GLA_DV
    sq = pl.BlockSpec((None, S, kq), lambda b, g: (b, 0, g))
    sk = pl.BlockSpec((None, S, kq), lambda b, g: (b, 0, pairs + g))
    sv = pl.BlockSpec((None, S, kv), lambda b, g: (b, 0, (2 * GLA_QK) // kv + g))
    sr = pl.BlockSpec((None, S, kv), lambda b, g: (b, 0, (2 * GLA_QK + GLA_VW) // kv + g))
    sg = pl.BlockSpec((None, S, LANES), lambda b, g: (b, 0, 0))
    sw = pl.BlockSpec((LANES, kq), lambda b, g: (0, g))
    sb = pl.BlockSpec((1, kq), lambda b, g: (0, g))
    sn = pl.BlockSpec((1, kv), lambda b, g: (0, g))
    return pl.pallas_call(
        functools.partial(_gla_kernel, seq=S),
        grid=(B, pairs),
        in_specs=[sq, sk, sv, sr, sg, sw, sb, sw, sb, sn],
        out_specs=pl.BlockSpec((None, S, kv), lambda b, g: (b, 0, g)),
        out_shape=jax.ShapeDtypeStruct((B, S, GLA_VW), BF16),
        scratch_shapes=[pltpu.VMEM((2, S, kq), F32),
                        pltpu.VMEM((2, S, kq), BF16),
                        pltpu.VMEM((2, S, kq), BF16),
                        pltpu.VMEM((S // GLA_CHUNK, kv, GLA_CHUNK), BF16),
                        pltpu.VMEM((2, S // GLA_CHUNK, kq), F32),
                        pltpu.VMEM((S, kv), F32), pltpu.VMEM((S, kv), F32)],
        compiler_params=_params("parallel", "parallel"),
        cost_estimate=_cost(B * pairs * S * (12 * LANES * kq + 8 * GLA_TILE * (kq + kv // 2) + 8 * kq * kv),
                            B * S * (2 * 1536 + 4 * pairs * LANES + 2 * GLA_VW), 8 * B * pairs * S * kq),
        name="gla_mixer",
    )(h_a, h_a, h_a, h_a, gate, wf, bf, wb, bb, norm_g)


def _dil_kernel(q_ref, k_ref, v_ref, o_ref, m_ref, l_ref, acc_ref, bias_ref, s_ref, *, seq):
    R = DIL_RADIUS
    pair = pl.program_id(1)
    lane = lax.broadcasted_iota(jnp.int32, (1, LANES), 1)
    first_head = lane < DIL_DH
    scale = DIL_DH ** -0.5

    for p_idx, (window, dil) in enumerate(DIL_PATTERNS):
        assert window // (2 * dil) == R
        length = seq // dil
        tq = min(DIL_TQ, length)
        nk = tq + 2 * R
        tpr = length // tq
        shift = tpr.bit_length() - 1
        assert tpr == 1 << shift and tq % R == 0
        qi = lax.broadcasted_iota(jnp.int32, (tq, nk), 0)
        kj = lax.broadcasted_iota(jnp.int32, (tq, nk), 1)
        dist = jnp.abs(kj - R - qi)
        for h in range(2):
            slope = 1.0 / jnp.left_shift(jnp.ones((tq, nk), jnp.int32), 2 * pair + h + 1).astype(F32)
            bias_ref[h, :tq, :nk] = jnp.where(dist <= R, -(dil * dist).astype(F32) * slope, MASK_VALUE)
        ones = jnp.ones((nk, LANES), BF16)

        def rows_at(start, n, dil=dil):
            return pl.ds(start, n, stride=dil) if dil > 1 else pl.ds(pl.multiple_of(start, R), n)

        n_tiles = seq // tq
        assert n_tiles % 2 == 0

        def tile_pos(idx, dil=dil, tq=tq, tpr=tpr, shift=shift):
            res = lax.shift_right_logical(idx, shift)
            t = jnp.bitwise_and(idx, tpr - 1)
            start = res + dil * tq * t
            has_prev = t > 0
            has_next = t < tpr - 1
            prev = jnp.where(has_prev, start - dil * R, start)
            nxt = jnp.where(has_next, start + dil * tq, start)
            return start, prev, nxt, has_prev, has_next

        def scores(idx, slot, tq=tq, nk=nk, rows_at=rows_at, tile_pos=tile_pos):
            start, prev, nxt, _, _ = tile_pos(idx)
            q = (q_ref[rows_at(start, tq), :] * scale).astype(BF16)
            kcat = jnp.concatenate([k_ref[rows_at(prev, R), :], k_ref[rows_at(start, tq), :],
                                    k_ref[rows_at(nxt, R), :]], axis=0).astype(BF16)
            for h in range(2):
                qh = jnp.where(first_head if h == 0 else ~first_head, q, jnp.zeros_like(q))
                s_ref[slot, h, :tq, :nk] = _mm_nt(qh, kcat)

        def values(idx, slot, tq=tq, nk=nk, p_idx=p_idx, rows_at=rows_at, kj=kj, ones=ones,
                   tile_pos=tile_pos):
            start, prev, nxt, has_prev, has_next = tile_pos(idx)
            vcat = jnp.concatenate([v_ref[rows_at(prev, R), :], v_ref[rows_at(start, tq), :],
                                    v_ref[rows_at(nxt, R), :]], axis=0).astype(BF16)
            vones = jnp.concatenate([vcat, ones], axis=1)
            in_seq = ((kj >= R) | has_prev) & ((kj < R + tq) | has_next)
            ms, pvs = [], []
            for h in range(2):
                s = jnp.where(in_seq, s_ref[slot, h, :tq, :nk] + bias_ref[h, :tq, :nk], MASK_VALUE)
                m = jnp.max(s, axis=-1, keepdims=True)
                ms.append(m)
                pvs.append(_mm(jnp.exp(s - m).astype(BF16), vones))
            out_rows = rows_at(start, tq)
            m_ref[p_idx, out_rows, :] = jnp.where(first_head, ms[0], ms[1])
            l_ref[p_idx, out_rows, :] = jnp.where(first_head, pvs[0][:, LANES:], pvs[1][:, LANES:])
            acc_ref[p_idx, out_rows, :] = jnp.where(first_head, pvs[0][:, :LANES], pvs[1][:, :LANES])

        scores(0, 0)

        def body(i2, _, n_tiles=n_tiles, scores=scores, values=values):
            for u in range(2):
                idx = 2 * i2 + u
                scores(jnp.minimum(idx + 1, n_tiles - 1), 1 - u)
                values(idx, u)
            return 0

        lax.fori_loop(0, n_tiles // 2, body, 0, unroll=DIL_UNROLL)

    blk = 256
    n_pat = len(DIL_PATTERNS)

    def finish(j, _):
        rows = pl.ds(pl.multiple_of(j * blk, blk), blk)
        ms = [m_ref[p, rows, :] for p in range(n_pat)]
        m = functools.reduce(jnp.maximum, ms)
        ws = [jnp.exp(x - m) for x in ms]
        l = functools.reduce(lambda a, b: a + b, [w * l_ref[p, rows, :] for p, w in enumerate(ws)])
        acc = functools.reduce(lambda a, b: a + b, [w * acc_ref[p, rows, :] for p, w in enumerate(ws)])
        o_ref[rows, :] = (acc / l).astype(o_ref.dtype)
        return 0

    lax.fori_loop(0, seq // blk, finish, 0)


def _dilated(h_b, B, S):
    pairs = DIL_HEADS // 2
    sq = pl.BlockSpec((None, S, LANES), lambda b, g: (b, 0, g))
    sk = pl.BlockSpec((None, S, LANES), lambda b, g: (b, 0, pairs + g))
    sv = pl.BlockSpec((None, S, LANES), lambda b, g: (b, 0, 2 * pairs + g))
    return pl.pallas_call(
        functools.partial(_dil_kernel, seq=S),
        grid=(B, pairs),
        in_specs=[sq, sk, sv],
        out_specs=pl.BlockSpec((None, S, LANES), lambda b, g: (b, 0, g)),
        out_shape=jax.ShapeDtypeStruct((B, S, DIL_W), BF16),
        scratch_shapes=[pltpu.VMEM((len(DIL_PATTERNS), S, LANES), F32)] * 3
                       + [pltpu.VMEM((2, DIL_TQ, DIL_TQ + 2 * DIL_RADIUS), F32),
                          pltpu.VMEM((2, 2, DIL_TQ, DIL_TQ + 2 * DIL_RADIUS), F32)],
        compiler_params=_params("parallel", "parallel"),
        cost_estimate=_cost(B * DIL_HEADS * S * len(DIL_PATTERNS) * 2 * (DIL_TQ + 2 * DIL_RADIUS) * 3 * LANES,
                            B * S * (4 * 3 * DIL_W + 2 * DIL_W),
                            B * DIL_HEADS * S * len(DIL_PATTERNS) * (DIL_TQ + 2 * DIL_RADIUS)),
        name="dilated_mixer",
    )(h_b, h_b, h_b)


def _route(lt):
    g = [lt[i:i + 1, :] for i in range(MOE_GROUPS)]
    gmax = functools.reduce(jnp.maximum, g)
    gexp = [jnp.exp(x - gmax) for x in g]
    gsum = functools.reduce(lambda a, b: a + b, gexp)
    gprob = [x / gsum for x in gexp]
    g_val, g_idx = gprob[0], jnp.zeros_like(gprob[0], dtype=jnp.int32)
    for i in range(1, MOE_GROUPS):
        better = gprob[i] > g_val
        g_idx = jnp.where(better, i, g_idx)
        g_val = jnp.where(better, gprob[i], g_val)
    e = []
    for j in range(MOE_EPG):
        x = lt[MOE_GROUPS + j:MOE_GROUPS + j + 1, :]
        for gi in range(1, MOE_GROUPS):
            r0 = MOE_GROUPS + gi * MOE_EPG + j
            x = jnp.where(g_idx == gi, lt[r0:r0 + 1, :], x)
        e.append(x)
    emax = functools.reduce(jnp.maximum, e)
    eexp = [jnp.exp(x - emax) for x in e]
    esum = functools.reduce(lambda a, b: a + b, eexp)
    eprob = [x / esum for x in eexp]
    v1, i1 = eprob[0], jnp.zeros_like(g_idx)
    for j in range(1, MOE_EPG):
        better = eprob[j] > v1
        i1 = jnp.where(better, j, i1)
        v1 = jnp.where(better, eprob[j], v1)
    v2, i2 = jnp.full_like(v1, -1.0), jnp.zeros_like(g_idx)
    for j in range(MOE_EPG):
        cand = jnp.where(i1 == j, -1.0, eprob[j])
        better = cand > v2
        i2 = jnp.where(better, j, i2)
        v2 = jnp.where(better, cand, v2)
    den = v1 + v2
    ids = jnp.concatenate([g_idx * MOE_EPG + i1, g_idx * MOE_EPG + i2], axis=0)
    wts = jnp.concatenate([g_val * (v1 / den), g_val * (v2 / den)], axis=0)
    return ids, wts


def _proj_ln_route_kernel(*refs, n_in):
    x_ref = refs[0]
    o_refs = refs[1:1 + n_in]
    w_refs = refs[1 + n_in:1 + 2 * n_in]
    g_ref, b_ref, wr_ref, br_ref, x1_ref, x1b_ref, ids_ref, wts_ref = refs[1 + 2 * n_in:]
    m = _mm(o_refs[0][...], w_refs[0][...])
    for o_ref, w_ref in zip(o_refs[1:], w_refs[1:]):
        m = m + _mm(o_ref[...], w_ref[...])
    x1 = _layer_norm(ALPHA * x_ref[...] + m, g_ref[...], b_ref[...])
    x1_ref[...] = x1
    x_hi = x1.astype(BF16)
    x1b_ref[...] = x_hi
    x_lo = (x1 - x_hi.astype(F32)).astype(BF16)
    wr = wr_ref[...]
    w_hi = wr.astype(BF16)
    w_lo = (wr - w_hi.astype(F32)).astype(BF16)
    lt = _mm_nt(w_hi, x_hi) + _mm_nt(w_hi, x_lo) + _mm_nt(w_lo, x_hi) + br_ref[...]
    ids, wts = _route(lt)
    ids_ref[...] = ids
    wts_ref[...] = wts


def _proj_ln_route(x2d, outs, ws, ln_g, ln_b, wr, br):
    T = x2d.shape[0]
    tm = ROW_TILE
    n_in = len(outs)
    row = lambda w: pl.BlockSpec((tm, w), lambda i: (i, 0))
    tok = pl.BlockSpec((2, tm), lambda i: (0, i))
    return pl.pallas_call(
        functools.partial(_proj_ln_route_kernel, n_in=n_in),
        grid=(T // tm,),
        in_specs=[row(D_MODEL)] + [row(o.shape[1]) for o in outs] + [_full(w.shape) for w in ws]
                 + [_full(ln_g.shape), _full(ln_b.shape), _full(wr.shape), _full(br.shape)],
        out_specs=[row(D_MODEL), row(D_MODEL), tok, tok],
        out_shape=[jax.ShapeDtypeStruct((T, D_MODEL), F32), jax.ShapeDtypeStruct((T, D_MODEL), BF16),
                   jax.ShapeDtypeStruct((2, T), jnp.int32), jax.ShapeDtypeStruct((2, T), F32)],
        compiler_params=_params("parallel"),
        cost_estimate=_cost(2 * T * D_MODEL * (sum(o.shape[1] for o in outs) + 3 * 32),
                            T * (4 * D_MODEL + 2 * sum(o.shape[1] for o in outs) + 6 * D_MODEL + 16)),
        name="proj_ln_route",
    )(x2d, *outs, *ws, ln_g, ln_b, wr, br)


def _moe_kernel(te_ref, nu_ref, xs_ref, w1_ref, w3_ref, w2_ref, rw_ref, tok_ref, ys_ref, w13_s, w2_s):
    i = pl.program_id(0)

    @pl.when((i == 0) | (te_ref[i] != te_ref[jnp.maximum(i - 1, 0)]))
    def _():
        w13_s[:, :MOE_FF] = w1_ref[...].astype(BF16)
        w13_s[:, MOE_FF:] = w3_ref[...].astype(BF16)
        w2_s[...] = w2_ref[...].astype(BF16)

    @pl.when(i < nu_ref[0])
    def _():
        h = _mm(xs_ref[...], w13_s[...])
        h1, h3 = h[:, :MOE_FF], h[:, MOE_FF:]
        hidden = (h1 * jax.nn.sigmoid(h1)) * h3
        y = _mm(hidden.astype(BF16), w2_s[...])
        ys_ref[...] = (y * rw_ref[...]).astype(ys_ref.dtype)

    @pl.when(i >= nu_ref[0])
    def _():
        ys_ref[...] = jnp.zeros_like(ys_ref)


def _moe_experts(xs, w1, w3, w2, layer, row_w, tile_expert, n_used, token):
    P = xs.shape[0]
    tm = MOE_TILE
    grid_spec = pltpu.PrefetchScalarGridSpec(
        num_scalar_prefetch=2,
        grid=(P // tm,),
        in_specs=[pl.BlockSpec((tm, D_MODEL), lambda i, te, nu: (i, 0)),
                  pl.BlockSpec((None, None, D_MODEL, MOE_FF), lambda i, te, nu: (layer, te[i], 0, 0)),
                  pl.BlockSpec((None, None, D_MODEL, MOE_FF), lambda i, te, nu: (layer, te[i], 0, 0)),
                  pl.BlockSpec((None, None, MOE_FF, D_MODEL), lambda i, te, nu: (layer, te[i], 0, 0)),
                  pl.BlockSpec((tm, 1), lambda i, te, nu: (i, 0)), _TOKEN_SPEC],
        out_specs=pl.BlockSpec((tm, D_MODEL), lambda i, te, nu: (i, 0)),
        scratch_shapes=[pltpu.VMEM((D_MODEL, 2 * MOE_FF), BF16), pltpu.VMEM((MOE_FF, D_MODEL), BF16)],
    )
    return pl.pallas_call(
        _moe_kernel,
        grid_spec=grid_spec,
        out_shape=jax.ShapeDtypeStruct((P, D_MODEL), BF16),
        compiler_params=_params("arbitrary"),
        cost_estimate=_cost(6 * P * D_MODEL * MOE_FF, 4 * P * D_MODEL + 12 * MOE_EXPERTS * D_MODEL * MOE_FF,
                            P * MOE_FF),
        name="moe_experts",
    )(tile_expert, n_used, xs, w1, w3, w2, row_w, token)


def _moe_plan(ids, wts, T):
    tm = MOE_TILE
    n_assign = 2 * T
    e_flat = ids.reshape(n_assign)
    onehot = (e_flat[:, None] == jnp.arange(MOE_EXPERTS, dtype=jnp.int32)[None, :]).astype(jnp.int32)
    csum = jnp.cumsum(onehot, axis=0)
    counts = csum[-1]
    padded = ((counts + tm - 1) // tm) * tm
    ends = jnp.cumsum(padded)
    offs = ends - padded
    pos = jnp.sum(onehot * (csum - onehot + offs[None, :]), axis=1)
    P = n_assign + MOE_EXPERTS * tm
    n_tiles = P // tm
    tile_start = jnp.arange(n_tiles, dtype=jnp.int32) * tm
    tile_expert = jnp.minimum(jnp.sum((tile_start[:, None] >= ends[None, :]).astype(jnp.int32), axis=1),
                              MOE_EXPERTS - 1).astype(jnp.int32)
    n_used = (ends[-1:] // tm).astype(jnp.int32)
    tok = jnp.arange(n_assign, dtype=jnp.int32) % T
    _, tok_sorted, w_sorted = lax.sort((e_flat, tok, wts.reshape(n_assign)), num_keys=1, is_stable=True)
    t_onehot = (tile_expert[:, None] == jnp.arange(MOE_EXPERTS, dtype=jnp.int32)[None, :]).astype(jnp.int32)
    t_rank0 = tile_start - jnp.sum(t_onehot * offs[None, :], axis=1)
    t_count = jnp.sum(t_onehot * counts[None, :], axis=1)
    t_first = jnp.sum(t_onehot * (jnp.cumsum(counts) - counts)[None, :], axis=1)
    r = t_rank0[:, None] + jnp.arange(tm, dtype=jnp.int32)[None, :]
    valid = (r < t_count[:, None]).reshape(P)
    u = jnp.clip(t_first[:, None] + r, 0, n_assign - 1).reshape(P)
    src_tok = jnp.where(valid, tok_sorted.at[u].get(mode="promise_in_bounds"), 0)
    row_w = jnp.where(valid, w_sorted.at[u].get(mode="promise_in_bounds"), 0.0)
    return src_tok, row_w.reshape(P, 1), tile_expert, n_used, pos


def _combine_ln_kernel(x_ref, y0_ref, y1_ref, g_ref, b_ref, tok_ref, o_ref):
    y = ALPHA * x_ref[...] + y0_ref[...].astype(F32) + y1_ref[...].astype(F32)
    o_ref[...] = _layer_norm(y, g_ref[...], b_ref[...])


def _combine_ln(x2d, y01, ln_g, ln_b, token):
    T = x2d.shape[0]
    tm = ROW_TILE
    row = pl.BlockSpec((tm, D_MODEL), lambda i: (i, 0))
    second = pl.BlockSpec((tm, D_MODEL), lambda i: (i + T // tm, 0))
    y0 = y1 = y01
    return pl.pallas_call(
        _combine_ln_kernel,
        grid=(T // tm,),
        in_specs=[row, row, second, _full(ln_g.shape), _full(ln_b.shape), _TOKEN_SPEC],
        out_specs=row,
        out_shape=jax.ShapeDtypeStruct((T, D_MODEL), F32),
        compiler_params=_params("parallel"),
        cost_estimate=_cost(12 * T * D_MODEL, 12 * T * D_MODEL),
        name="combine_ln",
    )(x2d, y0, y1, ln_g, ln_b, token)


def _experts_stage(i, routed, prm, token):
    _, x1b, (src_tok, row_w, tile_expert, n_used, _) = routed
    xs = x1b.at[src_tok].get(mode="promise_in_bounds")
    return _moe_experts(xs, prm["moe_w1"], prm["moe_w3"], prm["moe_w2"], i, row_w, tile_expert, n_used, token)


def _combine_stage(i, routed, ys, prm, token):
    x1, _, plan = routed
    T = x1.shape[0]
    pos = plan[4]
    y01 = ys.at[pos].get(mode="promise_in_bounds")
    return _combine_ln(x1, y01, prm["ln2_g"][i:i + 1], prm["ln2_b"][i:i + 1], token)


def _rope_block(x, c, sa, sb):
    return x * c + pltpu.roll(x, LANES - 32, 1) * sa + pltpu.roll(x, 32, 1) * sb


def _mla_in_kernel(x_ref, win_ref, qg_ref, kvg_ref, wq_ref, wk_ref, wv_ref, c_ref, sa_ref, sb_ref,
                   tok_ref, q_ref, k_ref, v_ref):
    x = x_ref[...].astype(BF16)
    h = _mm(x, win_ref[...])
    c_q = h[:, :MLA_Q_LORA]
    c_kv = h[:, MLA_Q_LORA:MLA_Q_LORA + MLA_KV_LORA]
    k_rope = h[:, MLA_Q_LORA + MLA_KV_LORA:]
    cqn = c_q * lax.rsqrt(jnp.mean(c_q * c_q, axis=-1, keepdims=True) + RMS_EPS) * qg_ref[...]
    ckn = c_kv * lax.rsqrt(jnp.mean(c_kv * c_kv, axis=-1, keepdims=True) + RMS_EPS) * kvg_ref[...]
    cqn, ckn = cqn.astype(BF16), ckn.astype(BF16)
    scale = (MLA_NOPE + MLA_ROPE) ** -0.5 * LOG2_E
    q = _mm(cqn, wq_ref[...]) * scale
    kn = _mm(ckn, wk_ref[...])
    v = _mm(ckn, wv_ref[...]).astype(v_ref.dtype)
    c, sa, sb = c_ref[...], sa_ref[...], sb_ref[...]
    kr = _rope_block(k_rope, c, sa, sb).astype(k_ref.dtype)
    ones = jnp.ones((x.shape[0], LANES), v_ref.dtype)
    for g in range(MLA_HEADS // 2):
        lo = 2 * LANES * g
        v_ref[:, lo:lo + LANES] = v[:, LANES * g:LANES * (g + 1)]
        v_ref[:, lo + LANES:lo + 2 * LANES] = ones
        q_ref[:, lo:lo + LANES] = q[:, lo:lo + LANES].astype(q_ref.dtype)
        q_ref[:, lo + LANES:lo + 2 * LANES] = _rope_block(q[:, lo + LANES:lo + 2 * LANES], c, sa, sb
                                                          ).astype(q_ref.dtype)
        k_ref[:, lo:lo + LANES] = kn[:, LANES * g:LANES * (g + 1)].astype(k_ref.dtype)
        k_ref[:, lo + LANES:lo + 2 * LANES] = kr


def _mla_in(x2d, win, qg, kvg, wq, wk, wv, c, sa, sb, S, token):
    T = x2d.shape[0]
    tm = ROW_TILE
    row = lambda w: pl.BlockSpec((tm, w), lambda i: (i, 0))
    n_s = S // tm
    tab = pl.BlockSpec((tm, LANES), lambda i: (i % n_s, 0))
    pairs = MLA_HEADS // 2
    return pl.pallas_call(
        _mla_in_kernel,
        grid=(T // tm,),
        in_specs=[row(D_MODEL), _full(win.shape), _full(qg.shape), _full(kvg.shape), _full(wq.shape),
                  _full(wk.shape), _full(wv.shape), tab, tab, tab, _TOKEN_SPEC],
        out_specs=[row(pairs * 2 * LANES)] * 3,
        out_shape=[jax.ShapeDtypeStruct((T, pairs * 2 * LANES), BF16)] * 3,
        compiler_params=_params("parallel"),
        cost_estimate=_cost(2 * T * (D_MODEL * win.shape[1] + MLA_Q_LORA * wq.shape[1]
                                     + MLA_KV_LORA * (wk.shape[1] + wv.shape[1])),
                            T * (4 * D_MODEL + 3 * 2 * pairs * 2 * LANES)),
        name="mla_in_proj",
    )(x2d, win, qg, kvg, wq, wk, wv, c, sa, sb, token)


def _mla_attn_kernel(q_ref, k_ref, v_ref, o_ref, sa_ref, sb_ref, *, seq):
    lane2 = lax.broadcasted_iota(jnp.int32, (1, 2 * LANES), 1)
    rl = lane2 - LANES
    in_a = (lane2 < MLA_NOPE) | ((rl >= 0) & (rl < 16)) | ((rl >= 32) & (rl < 48))
    in_b = ((lane2 >= MLA_NOPE) & (lane2 < LANES)) | ((rl >= 16) & (rl < 32)) | ((rl >= 48) & (rl < 64))
    lane = lax.broadcasted_iota(jnp.int32, (1, LANES), 1)
    q = q_ref[...]
    tq = q.shape[0]
    tk = min(ATT_TK, seq)
    n_k = seq // tk
    q_heads = (jnp.where(in_a, q, jnp.zeros_like(q)), jnp.where(in_b, q, jnp.zeros_like(q)))
    s_refs = (sa_ref, sb_ref)

    def score_chunk(h, j, m):
        s = _mm_nt(q_heads[h], k_ref[j * tk:(j + 1) * tk, :])
        s_refs[h][:, j * tk:(j + 1) * tk] = s
        for c in range(tk // LANES):
            m = jnp.maximum(m, s[:, c * LANES:(c + 1) * LANES])
        return m

    def value_chunk(h, j, m_row, acc):
        p = jnp.exp2(s_refs[h][:, j * tk:(j + 1) * tk] - m_row).astype(BF16)
        return acc + _mm(p, v_ref[j * tk:(j + 1) * tk, :])

    neg = jnp.full((tq, LANES), -jnp.inf, F32)
    m = neg
    for j in range(n_k):
        m = score_chunk(0, j, m)
    m_a = jnp.max(m, axis=-1, keepdims=True)
    acc_a = jnp.zeros((tq, 2 * LANES), F32)
    m = neg
    for j in range(n_k):
        acc_a = value_chunk(0, j, m_a, acc_a)
        m = score_chunk(1, j, m)
    m_b = jnp.max(m, axis=-1, keepdims=True)
    acc_b = jnp.zeros((tq, 2 * LANES), F32)
    for j in range(n_k):
        acc_b = value_chunk(1, j, m_b, acc_b)
    out_a = acc_a[:, :LANES] / acc_a[:, LANES:]
    out_b = acc_b[:, :LANES] / acc_b[:, LANES:]
    o_ref[...] = jnp.where(lane < MLA_V, out_a, out_b).astype(o_ref.dtype)


def _mla_attn(q, k, v, B, S):
    pairs = MLA_HEADS // 2
    tq = min(ATT_TQ, S)
    return pl.pallas_call(
        functools.partial(_mla_attn_kernel, seq=S),
        grid=(B, pairs, S // tq),
        in_specs=[pl.BlockSpec((None, tq, 2 * LANES), lambda b, g, i: (b, i, g)),
                  pl.BlockSpec((None, S, 2 * LANES), lambda b, g, i: (b, 0, g)),
                  pl.BlockSpec((None, S, 2 * LANES), lambda b, g, i: (b, 0, g))],
        out_specs=pl.BlockSpec((None, tq, LANES), lambda b, g, i: (b, i, g)),
        out_shape=jax.ShapeDtypeStruct((B, S, pairs * LANES), BF16),
        scratch_shapes=[pltpu.VMEM((tq, S), F32), pltpu.VMEM((tq, S), F32)],
        compiler_params=_params("parallel", "parallel", "arbitrary"),
        cost_estimate=_cost(B * MLA_HEADS * S * S * 2 * 2 * 2 * LANES,
                            B * S * pairs * LANES * (3 * 2 * 2 + 2), B * MLA_HEADS * S * S),
        name="mla_attention",
    )(q, k, v)


def _prep_even(w_in, wa_f, ba_f, wa_b, ba_b, norm_g, w_out):
    o_af = 2 * GLA_QK + 2 * GLA_VW
    o_ab = o_af + GLA_RANK
    o_qb = o_ab + GLA_RANK
    wa = w_in[:, :o_af].astype(BF16)
    wb = w_in[:, o_qb:].astype(BF16)
    wg = jnp.zeros((D_MODEL, LANES), F32).at[:, :2 * GLA_RANK].set(w_in[:, o_af:o_qb]).astype(BF16)
    wf = jnp.zeros((LANES, GLA_QK), F32).at[:GLA_RANK].set(wa_f)
    wb_gate = jnp.zeros((LANES, GLA_QK), F32).at[GLA_RANK:2 * GLA_RANK].set(wa_b)
    return dict(wa=wa, wb=wb, wg=wg, wf=wf, bf=ba_f.reshape(1, -1), wb_gate=wb_gate,
                bb=ba_b.reshape(1, -1), norm_g=norm_g.reshape(1, -1),
                wo_a=w_out[:GLA_VW].astype(BF16), wo_b=w_out[GLA_VW:].astype(BF16))


def _prep_odd(w_in, q_norm, kv_norm, w_uq, w_ukv, w_out, S):
    half = MLA_ROPE // 2
    pairs = MLA_HEADS // 2
    kr = w_in[:, MLA_Q_LORA + MLA_KV_LORA:]
    kr_rep = jnp.concatenate([kr[:, :half], kr[:, :half], kr[:, half:], kr[:, half:],
                              jnp.zeros((D_MODEL, LANES - 2 * MLA_ROPE), F32)], axis=1)
    win = jnp.concatenate([w_in[:, :MLA_Q_LORA + MLA_KV_LORA], kr_rep], axis=1).astype(BF16)
    uq = w_uq.reshape(MLA_Q_LORA, MLA_HEADS, MLA_NOPE + MLA_ROPE)
    nope = uq[:, :, :MLA_NOPE].reshape(MLA_Q_LORA, pairs, 2 * MLA_NOPE)
    r1 = uq[:, :, MLA_NOPE:MLA_NOPE + half].reshape(MLA_Q_LORA, pairs, 2 * half)
    r2 = uq[:, :, MLA_NOPE + half:].reshape(MLA_Q_LORA, pairs, 2 * half)
    pad = jnp.zeros((MLA_Q_LORA, pairs, LANES - 2 * MLA_ROPE), F32)
    wq = jnp.concatenate([nope, r1, r2, pad], axis=2).reshape(MLA_Q_LORA, pairs * 2 * LANES).astype(BF16)
    ukv = w_ukv.reshape(MLA_KV_LORA, MLA_HEADS, MLA_NOPE + MLA_V)
    wk = ukv[:, :, :MLA_NOPE].reshape(MLA_KV_LORA, MLA_HEADS * MLA_NOPE).astype(BF16)
    wv = ukv[:, :, MLA_NOPE:].reshape(MLA_KV_LORA, MLA_HEADS * MLA_V).astype(BF16)
    inv = 1.0 / (ROPE_BASE ** (jnp.arange(0, MLA_ROPE, 2, dtype=F32) / MLA_ROPE))
    ang = jnp.arange(S, dtype=F32)[:, None] * inv[None, :]
    cos, sin = jnp.cos(ang), jnp.sin(ang)
    z16 = jnp.zeros((S, half), F32)
    z64 = jnp.zeros((S, LANES - 2 * MLA_ROPE), F32)
    c = jnp.concatenate([cos, cos, cos, cos, z64], axis=1)
    sa = jnp.concatenate([-sin, -sin, z16, z16, z64], axis=1)
    sb = jnp.concatenate([z16, z16, sin, sin, z64], axis=1)
    return dict(win=win, qg=q_norm.reshape(1, -1), kvg=kv_norm.reshape(1, -1), wq=wq, wk=wk, wv=wv,
                c=c, sa=sa, sb=sb, wo=w_out.astype(BF16))


def _prep_moe(wg, bg, we, be):
    wr = jnp.zeros((32, D_MODEL), F32).at[:MOE_GROUPS].set(wg.T).at[MOE_GROUPS:MOE_GROUPS + MOE_EXPERTS].set(we.T)
    br = jnp.zeros((32, 1), F32).at[:MOE_GROUPS, 0].set(bg).at[MOE_GROUPS:MOE_GROUPS + MOE_EXPERTS, 0].set(be)
    return dict(wr=wr, br=br)


def _mix_route(i, x2d, B, S, prm, token):
    ev, od, moe, ln1_g, ln1_b = prm["ev"], prm["od"], prm["moe"], prm["ln1_g"], prm["ln1_b"]
    T = B * S
    if i % 2 == 0:
        p = ev[i // 2]
        h_a, h_b, gate = _even_in(x2d, p["wa"], p["wb"], p["wg"], token)
        o_a = _gla(h_a.reshape(B, S, -1), gate.reshape(B, S, -1), p["wf"], p["bf"], p["wb_gate"],
                   p["bb"], p["norm_g"], B, S)
        o_b = _dilated(h_b.reshape(B, S, -1), B, S)
        outs = [o_a.reshape(T, -1), o_b.reshape(T, -1)]
        ws = [p["wo_a"], p["wo_b"]]
    else:
        p = od[i // 2]
        q, k, v = _mla_in(x2d, p["win"], p["qg"], p["kvg"], p["wq"], p["wk"], p["wv"],
                          p["c"], p["sa"], p["sb"], S, token)
        o = _mla_attn(q.reshape(B, S, -1), k.reshape(B, S, -1), v.reshape(B, S, -1), B, S)
        outs = [o.reshape(T, -1)]
        ws = [p["wo"]]
    m = moe[i]
    x1, x1b, ids, wts = _proj_ln_route(x2d, outs, ws, ln1_g[i:i + 1], ln1_b[i:i + 1], m["wr"], m["br"])
    return x1, x1b, _moe_plan(ids, wts, T)


def _plan_token(routed):
    return routed[2][0]


def _rows_token(rows):
    return rows[:8, :LANES]


def _trunk_pair(xa, xb, prm):
    (Ba, S, D), Bb = xa.shape, xb.shape[0]
    a = xa.reshape(Ba * S, D)
    b = xb.reshape(Bb * S, D)
    tok = jnp.zeros((1,), jnp.int32)
    for i in range(DEPTH):
        routed_a = _mix_route(i, a, Ba, S, prm, tok)
        if i > 0:
            b = _combine_stage(i - 1, routed_b, ys_b, prm, _plan_token(routed_a))
        routed_b = _mix_route(i, b, Bb, S, prm, _plan_token(routed_a))
        ys_a = _experts_stage(i, routed_a, prm, _rows_token(routed_b[0]))
        ys_b = _experts_stage(i, routed_b, prm, _rows_token(ys_a))
        a = _combine_stage(i, routed_a, ys_a, prm, _rows_token(ys_b))
        tok = _rows_token(a)
    b = _combine_stage(DEPTH - 1, routed_b, ys_b, prm, tok)
    return a.reshape(Ba, S, D), b.reshape(Bb, S, D)


def kernel(x_prompt, x_sample, ev_w_in, ev_wa_f, ev_ba_f, ev_wa_b, ev_ba_b, ev_gla_norm, ev_w_out,
           od_w_in, od_q_norm, od_kv_norm, od_w_uq, od_w_ukv, od_w_out, ln1_g, ln1_b, ln2_g, ln2_b,
           moe_wg, moe_bg, moe_we, moe_be, moe_w1, moe_w3, moe_w2):
    S = x_prompt.shape[1]
    ev = [_prep_even(ev_w_in[j], ev_wa_f[j], ev_ba_f[j], ev_wa_b[j], ev_ba_b[j], ev_gla_norm[j], ev_w_out[j])
          for j in range(ev_w_in.shape[0])]
    od = [_prep_odd(od_w_in[j], od_q_norm[j], od_kv_norm[j], od_w_uq[j], od_w_ukv[j], od_w_out[j], S)
          for j in range(od_w_in.shape[0])]
    moe = [_prep_moe(moe_wg[i], moe_bg[i], moe_we[i], moe_be[i]) for i in range(DEPTH)]
    prm = dict(ev=ev, od=od, moe=moe, ln1_g=ln1_g, ln1_b=ln1_b, ln2_g=ln2_g, ln2_b=ln2_b,
               moe_w1=moe_w1, moe_w3=moe_w3, moe_w2=moe_w2)
    return _trunk_pair(x_prompt, x_sample, prm)
```
